```python
import jax, jax.numpy as jnp
from jax import lax
import numpy as np

D_MODEL = 2048
BATCH = 16
SEQ = 2048
DEPTH = 2

GRID_W = 64
CTX_LEN = 256
EPS = 1e-6
A_HEADS = 12
A_QK_NOPE = 128
A_QK_ROPE = 64
A_V_DIM = 128
A_Q_RANK = 768
A_KV_RANK = 512
ROPE_BASE = 10000.0
Q_BLOCK = 128
B_GROUPS = 4
B_GROUP_DIM = 128
C_HEADS = 8
C_HEAD_DIM = 128
CHUNK = 128
D_GROUPS = 8
D_GROUP_DIM = 128
CONV_W = 3
A_WIDTH = A_HEADS * A_V_DIM
B_WIDTH = B_GROUPS * B_GROUP_DIM
C_WIDTH = C_HEADS * C_HEAD_DIM
D_WIDTH = D_GROUPS * D_GROUP_DIM
EVEN_IN = A_Q_RANK + A_KV_RANK + A_QK_ROPE + B_WIDTH
ODD_IN = 2 * C_WIDTH + 3 * D_WIDTH
EVEN_OUT = A_WIDTH + B_WIDTH
ODD_OUT = C_WIDTH + D_WIDTH
N_EXPERTS = 64
TOP_K = 6
N_GROUPS = 8
TOPK_GROUPS = 4
EXPERT_FF = 704
SHARED_FF = 704
ROUTED_SCALE = 2.5
EXPERT_BLOCK = 128
N_EVEN = (DEPTH + 1) // 2
N_ODD = DEPTH // 2

kernel_name = 'hybrid_mla_fourier_sgu_shortconv_moe_dit'


def rmsnorm(x, g):
    xf = x.astype(jnp.float32)
    y = xf * lax.rsqrt(jnp.mean(xf * xf, axis=-1, keepdims=True) + EPS)
    return (y * g.astype(jnp.float32)).astype(x.dtype)


def adaln(cond, w, b):
    m = jax.nn.silu(cond) @ w + b
    return [p[:, None, :] for p in jnp.split(m, 6, axis=-1)]


def modulate(h, shift, scale):
    return h * (1.0 + scale) + shift


def axial_rope(rows):
    row = jnp.repeat(jnp.arange(rows, dtype=jnp.float32), GRID_W)
    col = jnp.tile(jnp.arange(GRID_W, dtype=jnp.float32), rows)
    per_axis = A_QK_ROPE // 4
    inv = ROPE_BASE ** (-jnp.arange(per_axis, dtype=jnp.float32) / per_axis)
    ang = jnp.concatenate([row[:, None] * inv, col[:, None] * inv], axis=-1)
    return jnp.cos(ang), jnp.sin(ang)


def apply_rope(x, cos, sin):
    xf = x.astype(jnp.float32)
    x1, x2 = jnp.split(xf, 2, axis=-1)
    return jnp.concatenate([x1 * cos - x2 * sin, x2 * cos + x1 * sin], axis=-1).astype(x.dtype)


def mla_queries(c_q, g_q, w_uq):
    b, l, _ = c_q.shape
    q = (rmsnorm(c_q, g_q) @ w_uq).reshape(b, l, A_HEADS, A_QK_NOPE + A_QK_ROPE)
    return jnp.split(q, [A_QK_NOPE], axis=-1)


def mla_kv_heads(c_kv, g_kv, w_ukv):
    b, l, _ = c_kv.shape
    kv = (rmsnorm(c_kv, g_kv) @ w_ukv).reshape(b, l, A_HEADS, A_QK_NOPE + A_V_DIM)
    return jnp.split(kv, [A_QK_NOPE], axis=-1)


def mla_attend(q_nope, q_pe, k_nope, k_pe, v):
    b, lq, h, _ = q_nope.shape
    nblk = lq // Q_BLOCK
    scale = (A_QK_NOPE + A_QK_ROPE) ** -0.5

    def to_blocks(q):
        return q.reshape(b, nblk, Q_BLOCK, h, q.shape[-1]).swapaxes(0, 1)

    def block(qb):
        qn, qp = qb
        s = (jnp.einsum('bqhd,bkhd->bhqk', qn, k_nope, preferred_element_type=jnp.float32)
             + jnp.einsum('bqhr,bkr->bhqk', qp, k_pe, preferred_element_type=jnp.float32))
        p = jax.nn.softmax(s * scale, axis=-1).astype(v.dtype)
        return jnp.einsum('bhqk,bkhd->bqhd', p, v)

    o = lax.map(block, (to_blocks(q_nope), to_blocks(q_pe)))
    return o.swapaxes(0, 1).reshape(b, lq, h * A_V_DIM)


def fourier_mix(u):
    b, l, _ = u.shape
    ug = u.reshape(b, l, B_GROUPS, B_GROUP_DIM).astype(jnp.float32)
    z = jnp.fft.fftn(ug, axes=(1, 3), norm='ortho').real
    return z.reshape(b, l, B_WIDTH).astype(u.dtype)


def context_kv(h, w_in, g_kv, w_ukv):
    z = h @ w_in[:, A_Q_RANK:A_Q_RANK + A_KV_RANK + A_QK_ROPE]
    c_kv, k_pe = jnp.split(z, [A_KV_RANK], axis=-1)
    k_nope, v = mla_kv_heads(c_kv, g_kv, w_ukv)
    return (k_nope, k_pe, v)


def even_mixer(h, ctx_kv, rope_cs, w_in, g_q, w_uq, g_kv, w_ukv, w_out):
    z = h @ w_in
    c_q, c_kv, k_pe, u_f = jnp.split(
        z, [A_Q_RANK, A_Q_RANK + A_KV_RANK, A_Q_RANK + A_KV_RANK + A_QK_ROPE], axis=-1)
    q_nope, q_pe = mla_queries(c_q, g_q, w_uq)
    k_nope, v = mla_kv_heads(c_kv, g_kv, w_ukv)
    if rope_cs is not None:
        cos, sin = rope_cs
        q_pe = apply_rope(q_pe, cos[:, None, :], sin[:, None, :])
        k_pe = apply_rope(k_pe, cos, sin)
    own_kv = (k_nope, k_pe, v)
    if ctx_kv is not None:
        k_nope, k_pe, v = [jnp.concatenate([a, o], axis=1) for a, o in zip(ctx_kv, own_kv)]
    o_a = mla_attend(q_nope, q_pe, k_nope, k_pe, v)
    o_b = fourier_mix(u_f)
    return jnp.concatenate([o_a, o_b], axis=-1) @ w_out, own_kv


def short_conv(s, w):
    return lax.conv_general_dilated(
        s, w.astype(s.dtype)[:, None, :], window_strides=(1,), padding=((CONV_W // 2, CONV_W // 2),),
        dimension_numbers=('NWC', 'WIO', 'NWC'), feature_group_count=s.shape[-1])


def odd_mixer(h, w_in, g_v, w_s, b_s, conv_w, w_out):
    b, l, _ = h.shape
    z = h @ w_in
    uv, hd, gb, gc = jnp.split(z, [2 * C_WIDTH, 2 * C_WIDTH + D_WIDTH, 2 * C_WIDTH + 2 * D_WIDTH], axis=-1)
    u, v = jnp.split(jax.nn.gelu(uv), 2, axis=-1)
    v = rmsnorm(v.reshape(b, l, C_HEADS, C_HEAD_DIM), g_v.reshape(C_HEADS, C_HEAD_DIM))
    v = v.reshape(b, l // CHUNK, CHUNK, C_HEADS, C_HEAD_DIM)
    sv = jnp.einsum('hpq,bnqhc->bnphc', w_s, v) + b_s.T[:, :, None]
    o_c = u * sv.reshape(b, l, C_WIDTH)
    o_d = gb * short_conv(gc * hd, conv_w)
    return jnp.concatenate([o_c, o_d], axis=-1) @ w_out


def swiglu(t, w_gate, w_up, w_down):
    return (jax.nn.silu(t @ w_gate) * (t @ w_up)) @ w_down


def moe_ffn(t, w_router, b_router, w_gate, w_up, w_down, ws_gate, ws_up, ws_down):
    n_tok, d = t.shape
    scores = jax.nn.sigmoid(jnp.dot(t, w_router, preferred_element_type=jnp.float32))
    sel = scores + b_router.astype(jnp.float32)
    grp = sel.reshape(n_tok, N_GROUPS, N_EXPERTS // N_GROUPS)
    grp_score = lax.top_k(grp, 2)[0].sum(-1)
    _, top_groups = lax.top_k(grp_score, TOPK_GROUPS)
    group_mask = jax.nn.one_hot(top_groups, N_GROUPS, dtype=jnp.float32).sum(1) > 0
    expert_mask = jnp.repeat(group_mask, N_EXPERTS // N_GROUPS, axis=1)
    sel = jnp.where(expert_mask, sel, -jnp.inf)
    _, top_e = lax.top_k(sel, TOP_K)
    wts = jnp.take_along_axis(scores, top_e, axis=1)
    wts = wts / wts.sum(-1, keepdims=True) * ROUTED_SCALE

    n_asg = n_tok * TOP_K
    flat_e = top_e.reshape(n_asg)
    flat_tok = jnp.repeat(jnp.arange(n_tok, dtype=jnp.int32), TOP_K)
    flat_w = wts.reshape(n_asg)
    order = jnp.argsort(flat_e)
    e_sorted = flat_e[order]
    counts = jnp.bincount(flat_e, length=N_EXPERTS)
    starts = jnp.cumsum(counts) - counts
    padded = (counts + EXPERT_BLOCK - 1) // EXPERT_BLOCK * EXPERT_BLOCK
    pad_end = jnp.cumsum(padded)
    pad_start = pad_end - padded
    dest = pad_start[e_sorted] + (jnp.arange(n_asg, dtype=jnp.int32) - starts[e_sorted])
    n_blocks = -(-n_asg // EXPERT_BLOCK) + N_EXPERTS
    n_rows = n_blocks * EXPERT_BLOCK
    row_tok = jnp.full((n_rows,), n_tok, jnp.int32).at[dest].set(flat_tok[order])
    row_w = jnp.zeros((n_rows,), jnp.float32).at[dest].set(flat_w[order])
    block_e = jnp.minimum(
        jnp.searchsorted(pad_end, jnp.arange(n_blocks, dtype=jnp.int32) * EXPERT_BLOCK, side='right'),
        N_EXPERTS - 1)
    t_pad = jnp.concatenate([t, jnp.zeros((1, d), t.dtype)], axis=0)

    def body(acc, blk):
        e, tok, wt = blk
        xb = t_pad[tok]
        hb = jax.nn.silu(xb @ w_gate[e]) * (xb @ w_up[e])
        yb = jnp.dot(hb, w_down[e], preferred_element_type=jnp.float32) * wt[:, None]
        return acc.at[tok].add(yb), None

    acc, _ = lax.scan(body, jnp.zeros((n_tok + 1, d), jnp.float32),
                      (block_e, row_tok.reshape(n_blocks, EXPERT_BLOCK), row_w.reshape(n_blocks, EXPERT_BLOCK)))
    return acc[:n_tok].astype(t.dtype) + swiglu(t, ws_gate, ws_up, ws_down)


def setup_inputs(seed: int = 0) -> dict:
    key = jax.random.key(seed)
    ks = jax.random.split(key, 32)
    f32 = jnp.float32

    def nrm(k, shape, scale):
        return jax.random.normal(k, shape, f32) * scale

    def gain(k, shape):
        return 1.0 + 0.05 * jax.random.normal(k, shape, f32)

    d = D_MODEL
    return {
        'x': nrm(ks[0], (BATCH, SEQ, d), 1.0),
        'c': nrm(ks[1], (BATCH, d), 1.0),
        'ctx': nrm(ks[2], (BATCH, CTX_LEN, d), 1.0),
        'c_ctx': nrm(ks[3], (d,), 1.0),
        'mod_w': nrm(ks[4], (DEPTH, d, 6 * d), 0.5 * d ** -0.5),
        'mod_b': nrm(ks[5], (DEPTH, 6 * d), 0.02),
        'norm_g': gain(ks[6], (DEPTH, 4, d)),
        'a_w_in': nrm(ks[7], (N_EVEN, d, EVEN_IN), d ** -0.5),
        'a_g_q': gain(ks[8], (N_EVEN, A_Q_RANK)),
        'a_w_uq': nrm(ks[9], (N_EVEN, A_Q_RANK, A_HEADS * (A_QK_NOPE + A_QK_ROPE)), A_Q_RANK ** -0.5),
        'a_g_kv': gain(ks[10], (N_EVEN, A_KV_RANK)),
        'a_w_ukv': nrm(ks[11], (N_EVEN, A_KV_RANK, A_HEADS * (A_QK_NOPE + A_V_DIM)), A_KV_RANK ** -0.5),
        'a_w_out': nrm(ks[12], (N_EVEN, EVEN_OUT, d), EVEN_OUT ** -0.5),
        'o_w_in': nrm(ks[13], (N_ODD, d, ODD_IN), d ** -0.5),
        'o_g_v': gain(ks[14], (N_ODD, C_WIDTH)),
        'o_w_s': nrm(ks[15], (N_ODD, C_HEADS, CHUNK, CHUNK), CHUNK ** -0.5),
        'o_b_s': 1.0 + nrm(ks[16], (N_ODD, C_HEADS, CHUNK), 0.02),
        'o_conv_w': nrm(ks[17], (N_ODD, CONV_W, D_WIDTH), CONV_W ** -0.5),
        'o_w_out': nrm(ks[18], (N_ODD, ODD_OUT, d), ODD_OUT ** -0.5),
        'moe_w_router': nrm(ks[19], (DEPTH, d, N_EXPERTS), d ** -0.5),
        'moe_b_router': nrm(ks[20], (DEPTH, N_EXPERTS), 0.01),
        'moe_w_gate': nrm(ks[21], (DEPTH, N_EXPERTS, d, EXPERT_FF), d ** -0.5),
        'moe_w_up': nrm(ks[22], (DEPTH, N_EXPERTS, d, EXPERT_FF), d ** -0.5),
        'moe_w_down': nrm(ks[23], (DEPTH, N_EXPERTS, EXPERT_FF, d), EXPERT_FF ** -0.5),
        'sh_w_gate': nrm(ks[24], (DEPTH, d, SHARED_FF), d ** -0.5),
        'sh_w_up': nrm(ks[25], (DEPTH, d, SHARED_FF), d ** -0.5),
        'sh_w_down': nrm(ks[26], (DEPTH, SHARED_FF, d), SHARED_FF ** -0.5),
    }


def reference(x, c, ctx, c_ctx, mod_w, mod_b, norm_g, a_w_in, a_g_q, a_w_uq, a_g_kv, a_w_ukv, a_w_out,
              o_w_in, o_g_v, o_w_s, o_b_s, o_conv_w, o_w_out, moe_w_router, moe_b_router,
              moe_w_gate, moe_w_up, moe_w_down, sh_w_gate, sh_w_up, sh_w_down):
    b, s, d = x.shape
    rows = s // GRID_W
    rope_cs = axial_rope(rows)
    ctx_s = ctx
    for l in range(DEPTH):
        ctx_needed = any(j % 2 == 0 for j in range(l + 1, DEPTH))
        ctx_active = (l % 2 == 0) or ctx_needed
        sh1, sc1, gt1, sh2, sc2, gt2 = adaln(c, mod_w[l], mod_b[l])
        g_pre_mix, g_post_mix, g_pre_ffn, g_post_ffn = norm_g[l][0], norm_g[l][1], norm_g[l][2], norm_g[l][3]
        hx = modulate(rmsnorm(x, g_pre_mix), sh1, sc1)
        if ctx_active:
            csh1, csc1, cgt1, csh2, csc2, cgt2 = adaln(c_ctx[None, :], mod_w[l], mod_b[l])
            hc = modulate(rmsnorm(ctx_s, g_pre_mix), csh1, csc1)
        if l % 2 == 0:
            e = l // 2
            wa = (a_w_in[e], a_g_q[e], a_w_uq[e], a_g_kv[e], a_w_ukv[e], a_w_out[e])
            if ctx_needed:
                mix_c, ctx_kv = even_mixer(hc, None, None, *wa)
            else:
                ctx_kv = context_kv(hc, a_w_in[e], a_g_kv[e], a_w_ukv[e])
            mix_x, _ = even_mixer(hx, ctx_kv, rope_cs, *wa)
        else:
            o = l // 2
            wo = (o_w_in[o], o_g_v[o], o_w_s[o], o_b_s[o], o_conv_w[o], o_w_out[o])
            mix_x = odd_mixer(hx, *wo)
            if ctx_needed:
                mix_c = odd_mixer(hc, *wo)
        x = x + gt1 * rmsnorm(mix_x, g_post_mix)
        moe_w = (moe_w_router[l], moe_b_router[l], moe_w_gate[l], moe_w_up[l], moe_w_down[l],
                 sh_w_gate[l], sh_w_up[l], sh_w_down[l])
        hx2 = modulate(rmsnorm(x, g_pre_ffn), sh2, sc2)
        if ctx_needed:
            ctx_s = ctx_s + cgt1 * rmsnorm(mix_c, g_post_mix)
            hc2 = modulate(rmsnorm(ctx_s, g_pre_ffn), csh2, csc2)
            n_ctx = b * ctx_s.shape[1]
            tok = jnp.concatenate([hc2.reshape(n_ctx, d), hx2.reshape(b * s, d)], axis=0)
            ff = moe_ffn(tok, *moe_w)
            ctx_s = ctx_s + cgt2 * rmsnorm(ff[:n_ctx].reshape(ctx_s.shape), g_post_ffn)
            ff_x = ff[n_ctx:].reshape(b, s, d)
        else:
            ff_x = moe_ffn(hx2.reshape(b * s, d), *moe_w).reshape(b, s, d)
        x = x + gt2 * rmsnorm(ff_x, g_post_ffn)
    return x
```

```python
import functools

import jax
import jax.numpy as jnp
from jax import lax
from jax.experimental import pallas as pl
from jax.experimental.pallas import tpu as pltpu

F32 = jnp.float32
BF16 = jnp.bfloat16
U32 = jnp.uint32

D_MODEL = 2048
GRID_W = 64
EPS = 1e-6
A_HEADS = 12
A_QK_NOPE = 128
A_QK_ROPE = 64
A_V_DIM = 128
A_Q_RANK = 768
A_KV_RANK = 512
ROPE_BASE = 10000.0
B_GROUPS = 4
B_GROUP_DIM = 128
B_WIDTH = B_GROUPS * B_GROUP_DIM
A_WIDTH = A_HEADS * A_V_DIM
C_HEADS = 8
C_HEAD_DIM = 128
CHUNK = 128
C_WIDTH = C_HEADS * C_HEAD_DIM
D_WIDTH = 1024
N_EXPERTS = 64
TOP_K = 6
N_GROUPS = 8
GROUP_SIZE = N_EXPERTS // N_GROUPS
TOPK_GROUPS = 4
EXPERT_FF = 704
ROUTED_SCALE = 2.5

LANES = 128
VMEM_BYTES_V7X = 64 * 1024 * 1024
FF_PAD = 768
HALF = D_MODEL // 2
QK_PAD = 256
COND_ROWS = 32
ROW_BLOCK = 256
EXPERT_BLOCK = 256
ROUTE_BLOCK = 512


def _cparams(semantics, vmem_mib):
    assert vmem_mib * 1024 * 1024 < VMEM_BYTES_V7X
    return pltpu.CompilerParams(dimension_semantics=semantics, vmem_limit_bytes=vmem_mib * 1024 * 1024)


def _resident(shape):
    nd = len(shape)
    return pl.BlockSpec(shape, lambda *_: (0,) * nd, pipeline_mode=pl.Buffered(1))


def _dot(a, b):
    return jnp.dot(a, b, preferred_element_type=F32)


def _dot_nt(a, b):
    return lax.dot_general(a, b, (((1,), (1,)), ((), ())), preferred_element_type=F32)


def _rms(xf, g):
    return xf * lax.rsqrt(jnp.mean(xf * xf, axis=-1, keepdims=True) + EPS) * g


def _split_bf16(a):
    hi = a.astype(BF16)
    lo = (a - hi.astype(F32)).astype(BF16)
    return hi, lo


def _pack_halves(lo_f32, hi_f32):
    lo = lax.bitcast_convert_type(lo_f32.astype(BF16).astype(F32), U32) >> 16
    hi = lax.bitcast_convert_type(hi_f32.astype(BF16).astype(F32), U32) & jnp.uint32(0xFFFF0000)
    return lo | hi


def _unpack_halves(w):
    lo = lax.bitcast_convert_type(w << 16, F32)
    hi = lax.bitcast_convert_type(w & jnp.uint32(0xFFFF0000), F32)
    return lo, hi


def _silu(x):
    return x * jax.nn.sigmoid(x)


def _adaln_kernel(c_ref, w_ref, b_ref, o_ref):
    a_hi, a_lo = _split_bf16(_silu(c_ref[...]))
    w_hi, w_lo = _split_bf16(w_ref[0])
    o_ref[0] = _dot(a_hi, w_hi) + _dot(a_lo, w_hi) + _dot(a_hi, w_lo) + b_ref[0]


def _adaln(cond, mod_w, mod_b):
    depth, d, n = mod_w.shape
    tn = 512
    return pl.pallas_call(
        _adaln_kernel,
        grid=(depth, n // tn),
        in_specs=[
            pl.BlockSpec((COND_ROWS, d), lambda l, j: (0, 0)),
            pl.BlockSpec((1, d, tn), lambda l, j: (l, 0, j)),
            pl.BlockSpec((1, 1, tn), lambda l, j: (l, 0, j)),
        ],
        out_specs=pl.BlockSpec((1, COND_ROWS, tn), lambda l, j: (l, 0, j)),
        out_shape=jax.ShapeDtypeStruct((depth, COND_ROWS, n), F32),
        compiler_params=_cparams(("parallel", "parallel"), 32),
        name="adaln",
    )(cond, mod_w, mod_b.reshape(depth, 1, n))


NOPE_W = A_HEADS * A_QK_NOPE
ROPE_W = A_HEADS * A_QK_ROPE
EVEN_IN_PAD = A_Q_RANK + A_KV_RANK + B_WIDTH + 2 * A_QK_ROPE


def _even_in_kernel(x_ref, sh_ref, sc_ref, g_ref, win_ref, gq_ref, wuq_ref, gkv_ref, wukv_ref,
                    cos_ref, sin_ref, bdc_ref, bds_ref, q_ref, k_ref, v_ref, uc_ref, us_ref):
    tm = x_ref.shape[1]
    h = _rms(x_ref[0], g_ref[...]) * (1.0 + sc_ref[0]) + sh_ref[0]
    z = _dot(h.astype(BF16), win_ref[...])
    c_q = z[:, :A_Q_RANK]
    c_kv = z[:, A_Q_RANK:A_Q_RANK + A_KV_RANK]
    o = A_Q_RANK + A_KV_RANK
    u_f = z[:, o:o + B_WIDTH].astype(BF16)
    k_pe = z[:, o + B_WIDTH:o + B_WIDTH + A_QK_ROPE]
    k_pe_sw = z[:, o + B_WIDTH + A_QK_ROPE:]
    cos = cos_ref[...]
    sin = sin_ref[...]
    scale = (A_QK_NOPE + A_QK_ROPE) ** -0.5

    q = _dot(_rms(c_q, gq_ref[...]).astype(BF16), wuq_ref[...]) * scale
    q_pe = q[:, NOPE_W:NOPE_W + ROPE_W] * cos + q[:, NOPE_W + ROPE_W:] * sin
    kv = _dot(_rms(c_kv, gkv_ref[...]).astype(BF16), wukv_ref[...])
    k_pe_r = (k_pe * cos[:, :A_QK_ROPE] + k_pe_sw * sin[:, :A_QK_ROPE]).astype(BF16)
    zeros = jnp.zeros((tm, QK_PAD - A_QK_NOPE - A_QK_ROPE), BF16)
    for hd in range(A_HEADS):
        n0 = hd * A_QK_NOPE
        r0 = hd * A_QK_ROPE
        q_ref[0, hd, :, :A_QK_NOPE] = q[:, n0:n0 + A_QK_NOPE].astype(BF16)
        q_ref[0, hd, :, A_QK_NOPE:A_QK_NOPE + A_QK_ROPE] = q_pe[:, r0:r0 + A_QK_ROPE].astype(BF16)
        q_ref[0, hd, :, A_QK_NOPE + A_QK_ROPE:] = zeros
        k_ref[0, hd, :, :A_QK_NOPE] = kv[:, n0:n0 + A_QK_NOPE].astype(BF16)
        k_ref[0, hd, :, A_QK_NOPE:A_QK_NOPE + A_QK_ROPE] = k_pe_r
        k_ref[0, hd, :, A_QK_NOPE + A_QK_ROPE:] = zeros
        v_ref[0, hd] = kv[:, NOPE_W + n0:NOPE_W + n0 + A_V_DIM].astype(BF16)
    uc_ref[0] = _dot(u_f, bdc_ref[...]).astype(BF16)
    us_ref[0] = _dot(u_f, bds_ref[...]).astype(BF16)


def _even_in(x, sh, sc, g, win, gq, wuq, gkv, wukv, cos, sin, bdc, bds):
    b, l, d = x.shape
    tm = ROW_BLOCK
    row = lambda bi, i: (bi, i, 0)
    per_b = lambda bi, i: (bi, 0, 0)
    heads = lambda bi, i: (bi, 0, i, 0)
    return pl.pallas_call(
        _even_in_kernel,
        grid=(b, l // tm),
        in_specs=[
            pl.BlockSpec((1, tm, d), row),
            pl.BlockSpec((1, 1, d), per_b),
            pl.BlockSpec((1, 1, d), per_b),
            _resident(g.shape), _resident(win.shape), _resident(gq.shape), _resident(wuq.shape),
            _resident(gkv.shape), _resident(wukv.shape),
            pl.BlockSpec((tm, ROPE_W), lambda bi, i: (i, 0)),
            pl.BlockSpec((tm, ROPE_W), lambda bi, i: (i, 0)),
            _resident(bdc.shape), _resident(bds.shape),
        ],
        out_specs=[
            pl.BlockSpec((1, A_HEADS, tm, QK_PAD), heads),
            pl.BlockSpec((1, A_HEADS, tm, QK_PAD), heads),
            pl.BlockSpec((1, A_HEADS, tm, A_V_DIM), heads),
            pl.BlockSpec((1, tm, B_WIDTH), row),
            pl.BlockSpec((1, tm, B_WIDTH), row),
        ],
        out_shape=[
            jax.ShapeDtypeStruct((b, A_HEADS, l, QK_PAD), BF16),
            jax.ShapeDtypeStruct((b, A_HEADS, l, QK_PAD), BF16),
            jax.ShapeDtypeStruct((b, A_HEADS, l, A_V_DIM), BF16),
            jax.ShapeDtypeStruct((b, l, B_WIDTH), BF16),
            jax.ShapeDtypeStruct((b, l, B_WIDTH), BF16),
        ],
        compiler_params=_cparams(("parallel", "parallel"), 56),
        name="even_in",
    )(x, sh, sc, g, win, gq, wuq, gkv, wukv, cos, sin, bdc, bds)


ATTN_Q_BLOCK = 512


def _attn_kernel(q_ref, kc_ref, ko_ref, vc_ref, vo_ref, o_ref):
    kc = kc_ref[0, 0]
    ko = ko_ref[0, 0]
    vc = vc_ref[0, 0]
    vo = vo_ref[0, 0]
    tq = ATTN_Q_BLOCK

    def body(i, carry):
        r0 = pl.multiple_of(i * tq, tq)
        q = q_ref[0, 0, pl.ds(r0, tq), :]
        s_c = _dot_nt(q, kc)
        s_o = _dot_nt(q, ko)
        m = jnp.maximum(jnp.max(s_c, axis=-1, keepdims=True), jnp.max(s_o, axis=-1, keepdims=True))
        p_c = jnp.exp(s_c - m)
        p_o = jnp.exp(s_o - m)
        denom = jnp.sum(p_c, axis=-1, keepdims=True) + jnp.sum(p_o, axis=-1, keepdims=True)
        o = _dot(p_c.astype(BF16), vc) + _dot(p_o.astype(BF16), vo)
        o_ref[0, pl.ds(r0, tq), :] = (o / denom).astype(BF16)
        return carry

    lax.fori_loop(0, q_ref.shape[2] // tq, body, 0)


def _attention(q, kc, ko, vc, vo):
    b, h, s, _ = q.shape
    lc = kc.shape[2]
    bh = lambda bi, hi: (bi, hi, 0, 0)
    return pl.pallas_call(
        _attn_kernel,
        grid=(b, h),
        in_specs=[
            pl.BlockSpec((1, 1, s, QK_PAD), bh),
            pl.BlockSpec((1, 1, lc, QK_PAD), bh),
            pl.BlockSpec((1, 1, s, QK_PAD), bh),
            pl.BlockSpec((1, 1, lc, A_V_DIM), bh),
            pl.BlockSpec((1, 1, s, A_V_DIM), bh),
        ],
        out_specs=pl.BlockSpec((1, s, A_V_DIM), lambda bi, hi: (bi, 0, hi)),
        out_shape=jax.ShapeDtypeStruct((b, s, h * A_V_DIM), BF16),
        compiler_params=_cparams(("parallel", "parallel"), 48),
        name="attention",
    )(q, kc, ko, vc, vo)


def _fourier_kernel(ac_ref, as_ref, uc_ref, us_ref, o_ref):
    o_ref[0] = (_dot(ac_ref[...], uc_ref[0]) + _dot(as_ref[...], us_ref[0])).astype(BF16)


def _fourier(a_c, a_s, uc, us):
    b, s, w = uc.shape
    tm = 512
    return pl.pallas_call(
        _fourier_kernel,
        grid=(b, s // tm),
        in_specs=[
            pl.BlockSpec((tm, s), lambda bi, i: (i, 0)),
            pl.BlockSpec((tm, s), lambda bi, i: (i, 0)),
            pl.BlockSpec((1, s, w), lambda bi, i: (bi, 0, 0)),
            pl.BlockSpec((1, s, w), lambda bi, i: (bi, 0, 0)),
        ],
        out_specs=pl.BlockSpec((1, tm, w), lambda bi, i: (bi, i, 0)),
        out_shape=jax.ShapeDtypeStruct((b, s, w), BF16),
        compiler_params=_cparams(("parallel", "parallel"), 32),
        name="fourier",
    )(a_c, a_s, uc, us)


def _mix_epilogue(mix, x_ref, gt_ref, sh_ref, sc_ref, gpost_ref, gffn_ref, wrh_ref, wrl_ref,
                  xo_ref, hx_ref, lg_ref):
    xn = x_ref[0] + gt_ref[0] * _rms(mix, gpost_ref[...])
    xo_ref[0] = xn
    t = _rms(xn, gffn_ref[...]) * (1.0 + sc_ref[0]) + sh_ref[0]
    t_hi, t_lo = _split_bf16(t)
    wrh = wrh_ref[...]
    lg = _dot(t_hi, wrh) + _dot(t_lo, wrh) + _dot(t_hi, wrl_ref[...])
    lg_ref[...] = lg.T[:N_EXPERTS, :]
    tq = t_hi.astype(F32)
    hx_ref[0] = _pack_halves(tq[:, :HALF], tq[:, HALF:])


def _even_out_kernel(oa_ref, ob_ref, w_ref, *rest):
    mix = _dot(oa_ref[0], w_ref[:A_WIDTH, :]) + _dot(ob_ref[0], w_ref[A_WIDTH:, :])
    _mix_epilogue(mix, *rest)


def _odd_out_kernel(oc_ref, s_ref, sp_ref, sn_ref, gb_ref, cw_ref, w_ref, *rest):
    i = pl.program_id(1)
    tm = s_ref.shape[1]
    s = s_ref[0].astype(F32)
    prev_row = jnp.where(i > 0, sp_ref[0, 15:16, :].astype(F32), 0.0)
    next_row = jnp.where(i < pl.num_programs(1) - 1, sn_ref[0, 0:1, :].astype(F32), 0.0)
    row = lax.broadcasted_iota(jnp.int32, (tm, 1), 0)
    s_dn = jnp.where(row == 0, prev_row, pltpu.roll(s, 1, axis=0))
    s_up = jnp.where(row == tm - 1, next_row, pltpu.roll(s, tm - 1, axis=0))
    cw = cw_ref[...]
    conv = cw[0:1, :] * s_dn + cw[1:2, :] * s + cw[2:3, :] * s_up
    od = (gb_ref[0].astype(F32) * conv).astype(BF16)
    mix = _dot(oc_ref[0], w_ref[:C_WIDTH, :]) + _dot(od, w_ref[C_WIDTH:, :])
    _mix_epilogue(mix, *rest)


def _epilogue_specs(b, s, d, tm):
    per_b = lambda bi, i: (bi, 0, 0)
    row = lambda bi, i: (bi, i, 0)
    nblk = s // tm
    in_specs = [
        pl.BlockSpec((1, tm, d), row),
        pl.BlockSpec((1, 1, d), per_b),
        pl.BlockSpec((1, 1, d), per_b),
        pl.BlockSpec((1, 1, d), per_b),
        _resident((1, d)), _resident((1, d)),
        _resident((d, LANES)), _resident((d, LANES)),
    ]
    out_specs = [
        pl.BlockSpec((1, tm, d), row),
        pl.BlockSpec((1, tm, HALF), row),
        pl.BlockSpec((N_EXPERTS, tm), lambda bi, i: (0, bi * nblk + i)),
    ]
    out_shape = [
        jax.ShapeDtypeStruct((b, s, d), F32),
        jax.ShapeDtypeStruct((b, s, HALF), U32),
        jax.ShapeDtypeStruct((N_EXPERTS, b * s), F32),
    ]
    return in_specs, out_specs, out_shape


def _even_out(oa, ob, w_out, x, gt, sh, sc, gpost, gffn, wrh, wrl):
    b, s, d = x.shape
    tm = ROW_BLOCK
    row = lambda bi, i: (bi, i, 0)
    ep_in, out_specs, out_shape = _epilogue_specs(b, s, d, tm)
    return pl.pallas_call(
        _even_out_kernel,
        grid=(b, s // tm),
        in_specs=[pl.BlockSpec((1, tm, A_WIDTH), row), pl.BlockSpec((1, tm, B_WIDTH), row),
                  _resident(w_out.shape)] + ep_in,
        out_specs=out_specs, out_shape=out_shape,
        compiler_params=_cparams(("parallel", "parallel"), 48),
        name="even_out",
    )(oa, ob, w_out, x, gt, sh, sc, gpost, gffn, wrh, wrl)


def _odd_out(oc, s_, gb, conv_w, w_out, x, gt, sh, sc, gpost, gffn, wrh, wrl):
    b, s, d = x.shape
    tm = ROW_BLOCK
    halo = 16
    nh = s // halo
    row = lambda bi, i: (bi, i, 0)
    ep_in, out_specs, out_shape = _epilogue_specs(b, s, d, tm)
    return pl.pallas_call(
        _odd_out_kernel,
        grid=(b, s // tm),
        in_specs=[
            pl.BlockSpec((1, tm, C_WIDTH), row),
            pl.BlockSpec((1, tm, D_WIDTH), row),
            pl.BlockSpec((1, halo, D_WIDTH), lambda bi, i: (bi, jnp.maximum(i * (tm // halo) - 1, 0), 0)),
            pl.BlockSpec((1, halo, D_WIDTH), lambda bi, i: (bi, jnp.minimum((i + 1) * (tm // halo), nh - 1), 0)),
            pl.BlockSpec((1, tm, D_WIDTH), row),
            _resident(conv_w.shape),
            _resident(w_out.shape),
        ] + ep_in,
        out_specs=out_specs, out_shape=out_shape,
        compiler_params=_cparams(("parallel", "parallel"), 48),
        name="odd_out",
    )(oc, s_, s_, s_, gb, conv_w, w_out, x, gt, sh, sc, gpost, gffn, wrh, wrl)


def _odd_in_kernel(x_ref, sh_ref, sc_ref, g_ref, win_ref, gv_ref, ws_ref, bs_ref, oc_ref, s_ref, gb_ref):
    tm = x_ref.shape[1]
    hb = (_rms(x_ref[0], g_ref[...]) * (1.0 + sc_ref[0]) + sh_ref[0]).astype(BF16)
    u = jax.nn.gelu(_dot(hb, win_ref[:, :C_WIDTH]))
    v = jax.nn.gelu(_dot(hb, win_ref[:, C_WIDTH:2 * C_WIDTH]))
    gv = gv_ref[...]
    bs = bs_ref[...]
    for hd in range(C_HEADS):
        c0 = hd * C_HEAD_DIM
        vh = _rms(v[:, c0:c0 + C_HEAD_DIM], gv[:, c0:c0 + C_HEAD_DIM]).astype(BF16)
        w = ws_ref[hd]
        bias = bs[:, hd:hd + 1]
        for n in range(tm // CHUNK):
            p0 = n * CHUNK
            sv = _dot(w, vh[p0:p0 + CHUNK, :]) + bias
            oc_ref[0, p0:p0 + CHUNK, c0:c0 + C_HEAD_DIM] = (u[p0:p0 + CHUNK, c0:c0 + C_HEAD_DIM] * sv).astype(BF16)
    o = 2 * C_WIDTH
    hd_ = _dot(hb, win_ref[:, o:o + D_WIDTH])
    gb_ref[0] = _dot(hb, win_ref[:, o + D_WIDTH:o + 2 * D_WIDTH]).astype(BF16)
    gc = _dot(hb, win_ref[:, o + 2 * D_WIDTH:])
    s_ref[0] = (gc * hd_).astype(BF16)


def _odd_in(x, sh, sc, g, win, gv, ws, bs_t):
    b, s, d = x.shape
    tm = ROW_BLOCK
    row = lambda bi, i: (bi, i, 0)
    per_b = lambda bi, i: (bi, 0, 0)
    out = jax.ShapeDtypeStruct((b, s, C_WIDTH), BF16)
    return pl.pallas_call(
        _odd_in_kernel,
        grid=(b, s // tm),
        in_specs=[
            pl.BlockSpec((1, tm, d), row),
            pl.BlockSpec((1, 1, d), per_b),
            pl.BlockSpec((1, 1, d), per_b),
            _resident(g.shape), _resident(win.shape), _resident(gv.shape), _resident(ws.shape),
            _resident(bs_t.shape),
        ],
        out_specs=[pl.BlockSpec((1, tm, C_WIDTH), row)] * 3,
        out_shape=[out, out, out],
        compiler_params=_cparams(("parallel", "parallel"), 56),
        name="odd_in",
    )(x, sh, sc, g, win, gv, ws, bs_t)


def _first_index_of_max(vals, idx, sentinel):
    m = jnp.max(vals, axis=0, keepdims=True)
    first = jnp.min(jnp.where(vals == m, idx, sentinel), axis=0, keepdims=True)
    return m, first


def _route_kernel(lg_ref, b_ref, e_ref, w_ref):
    tn = lg_ref.shape[1]
    scores = jax.nn.sigmoid(lg_ref[...])
    sel = scores + b_ref[...]
    row8 = lax.broadcasted_iota(jnp.int32, (8, tn), 0).astype(F32)
    neg = -jnp.inf
    gs = jnp.zeros((N_GROUPS, tn), F32)
    for g in range(N_GROUPS):
        blk = sel[g * GROUP_SIZE:(g + 1) * GROUP_SIZE, :]
        m1, i1 = _first_index_of_max(blk, row8, float(GROUP_SIZE))
        m2 = jnp.max(jnp.where(row8 == i1, neg, blk), axis=0, keepdims=True)
        gs = jnp.where(row8 == float(g), m1 + m2, gs)
    keep = jnp.zeros((N_GROUPS, tn), F32)
    cur = gs
    for _ in range(TOPK_GROUPS):
        _, first = _first_index_of_max(cur, row8, float(N_GROUPS))
        pick = row8 == first
        keep = jnp.where(pick, 1.0, keep)
        cur = jnp.where(pick, neg, cur)
    masked = jnp.concatenate(
        [jnp.where(jnp.max(jnp.where(row8 == float(g), keep, 0.0), axis=0, keepdims=True) > 0.5,
                   sel[g * GROUP_SIZE:(g + 1) * GROUP_SIZE, :], neg) for g in range(N_GROUPS)],
        axis=0)
    eidx = lax.broadcasted_iota(jnp.int32, (N_EXPERTS, tn), 0).astype(F32)
    e_out = jnp.zeros((8, tn), F32)
    w_out = jnp.zeros((8, tn), F32)
    total = jnp.zeros((1, tn), F32)
    for k in range(TOP_K):
        _, first = _first_index_of_max(masked, eidx, float(N_EXPERTS))
        pick = eidx == first
        wk = jnp.sum(jnp.where(pick, scores, 0.0), axis=0, keepdims=True)
        masked = jnp.where(pick, neg, masked)
        e_out = jnp.where(row8 == float(k), first, e_out)
        w_out = jnp.where(row8 == float(k), wk, w_out)
        total = total + wk
    e_ref[...] = e_out.astype(jnp.int32)
    w_ref[...] = w_out / total * ROUTED_SCALE


def _route(logits_t, bias):
    e, t = logits_t.shape
    tn = ROUTE_BLOCK
    return pl.pallas_call(
        _route_kernel,
        grid=(t // tn,),
        in_specs=[pl.BlockSpec((e, tn), lambda i: (0, i)), pl.BlockSpec((e, 1), lambda i: (0, 0))],
        out_specs=[pl.BlockSpec((8, tn), lambda i: (0, i))] * 2,
        out_shape=[jax.ShapeDtypeStruct((8, t), jnp.int32), jax.ShapeDtypeStruct((8, t), F32)],
        compiler_params=_cparams(("parallel",), 32),
        name="route",
    )(logits_t, bias.reshape(e, 1))


def _dispatch_tables(top_e, n_tok):
    bm = EXPERT_BLOCK
    n_asg = n_tok * TOP_K
    n_blocks = n_asg // bm + N_EXPERTS
    flat_e = top_e[:TOP_K].T.reshape(n_asg)
    order = jnp.argsort(flat_e, stable=True).astype(jnp.int32)
    e_sorted = flat_e[order]
    experts = jnp.arange(N_EXPERTS, dtype=jnp.int32)
    starts = jnp.searchsorted(e_sorted, experts, side='left').astype(jnp.int32)
    ends = jnp.searchsorted(e_sorted, experts, side='right').astype(jnp.int32)
    counts = ends - starts
    blocks_per_e = (counts + bm - 1) // bm
    blk_end = jnp.cumsum(blocks_per_e).astype(jnp.int32)
    blk_start = blk_end - blocks_per_e
    n_used = blk_end[-1]
    blk = jnp.arange(n_blocks, dtype=jnp.int32)
    block_e = jnp.minimum(jnp.searchsorted(blk_end, jnp.minimum(blk, n_used - 1), side='right'),
                          N_EXPERTS - 1).astype(jnp.int32)
    p = jnp.arange(n_blocks * bm, dtype=jnp.int32)
    pb = p // bm
    pe = block_e[pb]
    j = p - blk_start[pe] * bm
    valid = (j < counts[pe]) & (pb < n_used)
    a = order[jnp.clip(starts[pe] + j, 0, n_asg - 1)]
    tok = a // TOP_K
    src = jnp.where(valid, tok, 0)
    dst = jnp.where(valid, (a % TOP_K) * n_tok + tok, n_asg + pe * bm + p % bm)
    src = jnp.concatenate([src, jnp.zeros((bm,), jnp.int32)]).reshape(n_blocks + 1, 1, bm)
    first_dst = n_asg + N_EXPERTS * bm + jnp.arange(bm, dtype=jnp.int32)
    dst = jnp.concatenate([first_dst, dst]).reshape(n_blocks + 1, 1, bm)
    return src, dst, block_e, n_used.reshape(1)


def _swiglu_packed(xw, wg_ref, wu_ref, wd_ref):
    lo, hi = _unpack_halves(xw)
    lo = lo.astype(BF16)
    hi = hi.astype(BF16)
    g = _dot(lo, wg_ref[:HALF, :]) + _dot(hi, wg_ref[HALF:, :])
    u = _dot(lo, wu_ref[:HALF, :]) + _dot(hi, wu_ref[HALF:, :])
    return _dot((_silu(g) * u).astype(BF16), wd_ref[...])


def _expert_kernel(be_ref, nu_ref, src0_ref, src1_ref, dst_ref, hx_hbm, wg_ref, wu_ref, wd_ref,
                   ys_hbm, xbuf, ybuf, gsem, ssem):
    bm = EXPERT_BLOCK
    i = pl.program_id(0)
    n_used = nu_ref[0]
    slot = lax.rem(i, 2)
    other = 1 - slot

    def gather(src_ref, s):
        for r in range(bm):
            pltpu.make_async_copy(hx_hbm.at[pl.ds(src_ref[0, 0, r], 1), :], xbuf.at[s, pl.ds(r, 1), :],
                                  gsem.at[s]).start()

    def wait_gather(s):
        pltpu.make_async_copy(hx_hbm.at[pl.ds(0, bm), :], xbuf.at[s], gsem.at[s]).wait()

    def scatter(s):
        for r in range(bm):
            pltpu.make_async_copy(ybuf.at[s, pl.ds(r, 1), :], ys_hbm.at[pl.ds(dst_ref[0, 0, r], 1), :],
                                  ssem.at[0]).start()

    def wait_scatter(s):
        pltpu.make_async_copy(ybuf.at[s], ys_hbm.at[pl.ds(0, bm), :], ssem.at[0]).wait()

    @pl.when(i < n_used)
    def _():
        @pl.when(i == 0)
        def _():
            ybuf[1] = jnp.zeros(ybuf.shape[1:], U32)
            gather(src0_ref, 0)

        @pl.when(i > 0)
        def _():
            wait_scatter(slot)

        wait_gather(slot)
        gather(src1_ref, other)
        scatter(other)
        y = _swiglu_packed(xbuf[slot], wg_ref.at[0], wu_ref.at[0], wd_ref.at[0])
        ybuf[slot] = _pack_halves(y[:, :HALF], y[:, HALF:])

    @pl.when(i == n_used)
    def _():
        wait_scatter(slot)
        wait_gather(slot)
        scatter(other)
        wait_scatter(other)


def _experts(hx, src, dst, block_e, n_used, wg, wu, wd, n_out_rows):
    bm = EXPERT_BLOCK
    n_blocks = src.shape[0] - 1
    smem = functools.partial(pl.BlockSpec, (1, 1, bm), memory_space=pltpu.SMEM)
    wmap = lambda i, be, nu: (be[jnp.minimum(i, n_blocks - 1)], 0, 0)
    grid_spec = pltpu.PrefetchScalarGridSpec(
        num_scalar_prefetch=2,
        grid=(n_blocks + 1,),
        in_specs=[
            smem(index_map=lambda i, be, nu: (i, 0, 0)),
            smem(index_map=lambda i, be, nu: (jnp.minimum(i + 1, n_blocks), 0, 0)),
            smem(index_map=lambda i, be, nu: (i, 0, 0)),
            pl.BlockSpec(memory_space=pl.ANY),
            pl.BlockSpec((1, D_MODEL, FF_PAD), wmap),
            pl.BlockSpec((1, D_MODEL, FF_PAD), wmap),
            pl.BlockSpec((1, FF_PAD, D_MODEL), wmap),
        ],
        out_specs=pl.BlockSpec(memory_space=pl.ANY),
        scratch_shapes=[
            pltpu.VMEM((2, bm, HALF), U32),
            pltpu.VMEM((2, bm, HALF), U32),
            pltpu.SemaphoreType.DMA((2,)),
            pltpu.SemaphoreType.DMA((1,)),
        ],
    )
    return pl.pallas_call(
        _expert_kernel,
        grid_spec=grid_spec,
        out_shape=jax.ShapeDtypeStruct((n_out_rows, HALF), U32),
        compiler_params=_cparams(("arbitrary",), 48),
        name="experts",
    )(block_e, n_used, src, src, dst, hx, wg, wu, wd)


def _combine_kernel(hx_ref, y0, y1, y2, y3, y4, y5, wt_ref, wg_ref, wu_ref, wd_ref, x_ref, gt_ref, g_ref, o_ref):
    shared = _swiglu_packed(hx_ref[0], wg_ref, wu_ref, wd_ref)
    wt = wt_ref[...]
    acc_lo = None
    acc_hi = None
    for k, y_ref in enumerate((y0, y1, y2, y3, y4, y5)):
        lo, hi = _unpack_halves(y_ref[...])
        wk = wt[:, k:k + 1]
        acc_lo = wk * lo if acc_lo is None else acc_lo + wk * lo
        acc_hi = wk * hi if acc_hi is None else acc_hi + wk * hi
    ff = jnp.concatenate([acc_lo, acc_hi], axis=-1) + shared
    o_ref[0] = x_ref[0] + gt_ref[0] * _rms(ff, g_ref[...])


def _combine(hx, ys, wts_t, wg, wu, wd, x, gt, g):
    b, s, d = x.shape
    tm = ROW_BLOCK
    nblk = s // tm
    n_tok = b * s
    row = lambda bi, i: (bi, i, 0)
    y_specs = [pl.BlockSpec((tm, HALF), functools.partial(lambda bi, i, k: (k * (n_tok // tm) + bi * nblk + i, 0), k=k))
               for k in range(TOP_K)]
    return pl.pallas_call(
        _combine_kernel,
        grid=(b, nblk),
        in_specs=[pl.BlockSpec((1, tm, HALF), row)] + y_specs + [
            pl.BlockSpec((tm, 8), lambda bi, i: (bi * nblk + i, 0)),
            _resident(wg.shape), _resident(wu.shape), _resident(wd.shape),
            pl.BlockSpec((1, tm, d), row),
            pl.BlockSpec((1, 1, d), lambda bi, i: (bi, 0, 0)),
            _resident(g.shape),
        ],
        out_specs=pl.BlockSpec((1, tm, d), row),
        out_shape=jax.ShapeDtypeStruct((b, s, d), F32),
        compiler_params=_cparams(("parallel", "parallel"), 48),
        name="combine",
    )(hx, ys, ys, ys, ys, ys, ys, wts_t, wg, wu, wd, x, gt, g)


def _rope_tables(seq):
    t = jnp.arange(seq, dtype=jnp.int32)
    row = (t // GRID_W).astype(F32)
    col = (t % GRID_W).astype(F32)
    per_axis = A_QK_ROPE // 4
    inv = ROPE_BASE ** (-jnp.arange(per_axis, dtype=F32) / per_axis)
    ang = jnp.concatenate([row[:, None] * inv, col[:, None] * inv], axis=-1)
    cos = jnp.cos(ang)
    sin = jnp.sin(ang)
    cos64 = jnp.concatenate([cos, cos], axis=-1)
    sin64 = jnp.concatenate([-sin, sin], axis=-1)
    return jnp.tile(cos64, (1, A_HEADS)), jnp.tile(sin64, (1, A_HEADS))


def _dft_tables(n):
    k = jnp.arange(n, dtype=jnp.int32)
    ang = ((k[:, None] * k[None, :]) % n).astype(F32) * (2.0 * jnp.pi / n)
    scale = n ** -0.5
    return jnp.cos(ang) * scale, -jnp.sin(ang) * scale


def _swap_halves(w):
    half = w.shape[-1] // 2
    return jnp.concatenate([w[..., half:], w[..., :half]], axis=-1)


def _pad_ff(w, axis):
    pad = [(0, 0)] * w.ndim
    pad[axis] = (0, FF_PAD - EXPERT_FF)
    return jnp.pad(w.astype(BF16), pad)


def _router_parts(w_router):
    w = jnp.pad(w_router, ((0, 0), (0, LANES - N_EXPERTS)))
    hi = w.astype(BF16)
    lo = (w - hi.astype(F32)).astype(BF16)
    return hi, lo


def _moe(hx, logits_t, wts_args, x, gt, g_post):
    (b_router, wg, wu, wd, sg, su, sd) = wts_args
    b, s, d = x.shape
    n_tok = b * s
    top_e, wts = _route(logits_t, b_router)
    src, dst, block_e, n_used = _dispatch_tables(top_e, n_tok)
    n_out_rows = n_tok * TOP_K + (N_EXPERTS + 1) * EXPERT_BLOCK
    ys = _experts(hx.reshape(n_tok, HALF), src, dst, block_e, n_used, wg, wu, wd, n_out_rows)
    return _combine(hx, ys, wts.T, sg, su, sd, x, gt, g_post)


def kernel(x, c, ctx, c_ctx, mod_w, mod_b, norm_g, a_w_in, a_g_q, a_w_uq, a_g_kv, a_w_ukv, a_w_out, o_w_in, o_g_v, o_w_s, o_b_s, o_conv_w, o_w_out, moe_w_router, moe_b_router, moe_w_gate, moe_w_up, moe_w_down, sh_w_gate, sh_w_up, sh_w_down):
    b, s, d = x.shape
    lc = ctx.shape[1]

    cond = jnp.concatenate([c, c_ctx[None, :], jnp.zeros((COND_ROWS - b - 1, d), F32)], axis=0)
    mod = _adaln(cond, mod_w, mod_b)

    def mod_rows(layer, rows):
        m = mod[layer, rows]
        return [m[:, None, j * d:(j + 1) * d] for j in range(6)]

    def moe_weights(layer):
        return (moe_b_router[layer], _pad_ff(moe_w_gate[layer], 2), _pad_ff(moe_w_up[layer], 2),
                _pad_ff(moe_w_down[layer], 1), _pad_ff(sh_w_gate[layer], 1), _pad_ff(sh_w_up[layer], 1),
                _pad_ff(sh_w_down[layer], 0))

    sh1, sc1, gt1, sh2, sc2, gt2 = mod_rows(0, slice(0, b))
    csh1, csc1 = [jnp.broadcast_to(m, (b, 1, d)) for m in mod_rows(0, slice(b, b + 1))[:2]]
    g = norm_g[0]
    w_in = a_w_in[0]
    o = A_Q_RANK + A_KV_RANK
    k_pe_w = w_in[:, o:o + A_QK_ROPE]
    win = jnp.concatenate([w_in[:, :o], w_in[:, o + A_QK_ROPE:], k_pe_w, _swap_halves(k_pe_w)], axis=1).astype(BF16)
    wq = a_w_uq[0].reshape(A_Q_RANK, A_HEADS, A_QK_NOPE + A_QK_ROPE)
    wq_pe = wq[:, :, A_QK_NOPE:]
    wuq = jnp.concatenate([wq[:, :, :A_QK_NOPE].reshape(A_Q_RANK, NOPE_W), wq_pe.reshape(A_Q_RANK, ROPE_W),
                           _swap_halves(wq_pe).reshape(A_Q_RANK, ROPE_W)], axis=1).astype(BF16)
    wkv = a_w_ukv[0].reshape(A_KV_RANK, A_HEADS, A_QK_NOPE + A_V_DIM)
    wukv = jnp.concatenate([wkv[:, :, :A_QK_NOPE].reshape(A_KV_RANK, NOPE_W),
                            wkv[:, :, A_QK_NOPE:].reshape(A_KV_RANK, A_WIDTH)], axis=1).astype(BF16)
    gq = a_g_q[0][None, :]
    gkv = a_g_kv[0][None, :]
    cos, sin = _rope_tables(s)
    cc, cs = _dft_tables(B_GROUP_DIM)
    eye = jnp.eye(B_GROUPS, dtype=F32)
    bdc = jnp.kron(eye, cc).astype(BF16)
    bds = jnp.kron(eye, -cs).astype(BF16)
    a_c, a_s = _dft_tables(s)
    a_c = a_c.astype(BF16)
    a_s = a_s.astype(BF16)

    ones = jnp.ones((lc, ROPE_W), F32)
    _, kc, vc, _, _ = _even_in(ctx, csh1, csc1, g[0][None, :], win, gq, wuq, gkv, wukv, ones, jnp.zeros_like(ones),
                               bdc, bds)
    q, ko, vo, uc, us = _even_in(x, sh1, sc1, g[0][None, :], win, gq, wuq, gkv, wukv, cos, sin, bdc, bds)
    o_a = _attention(q, kc, ko, vc, vo)
    o_b = _fourier(a_c, a_s, uc, us)
    wrh, wrl = _router_parts(moe_w_router[0])
    x, hx, lg = _even_out(o_a, o_b, a_w_out[0].astype(BF16), x, gt1, sh2, sc2, g[1][None, :], g[2][None, :], wrh, wrl)
    x = _moe(hx, lg, moe_weights(0), x, gt2, g[3][None, :])

    sh1, sc1, gt1, sh2, sc2, gt2 = mod_rows(1, slice(0, b))
    g = norm_g[1]
    oc, s_, gb = _odd_in(x, sh1, sc1, g[0][None, :], o_w_in[0].astype(BF16), o_g_v[0][None, :],
                         o_w_s[0].astype(BF16), o_b_s[0].T)
    wrh, wrl = _router_parts(moe_w_router[1])
    x, hx, lg = _odd_out(oc, s_, gb, o_conv_w[0], o_w_out[0].astype(BF16), x, gt1, sh2, sc2, g[1][None, :],
                         g[2][None, :], wrh, wrl)
    x = _moe(hx, lg, moe_weights(1), x, gt2, g[3][None, :])
    return x
```

```python
import functools

import jax
import jax.numpy as jnp
from jax import lax
from jax.experimental import pallas as pl
from jax.experimental.pallas import tpu as pltpu

F32 = jnp.float32
BF16 = jnp.bfloat16
U32 = jnp.uint32

D_MODEL = 2048
GRID_W = 64
EPS = 1e-6
A_HEADS = 12
A_QK_NOPE = 128
A_QK_ROPE = 64
A_V_DIM = 128
A_Q_RANK = 768
A_KV_RANK = 512
ROPE_BASE = 10000.0
B_GROUPS = 4
B_GROUP_DIM = 128
B_WIDTH = B_GROUPS * B_GROUP_DIM
A_WIDTH = A_HEADS * A_V_DIM
C_HEADS = 8
C_HEAD_DIM = 128
CHUNK = 128
C_WIDTH = C_HEADS * C_HEAD_DIM
D_WIDTH = 1024
N_EXPERTS = 64
TOP_K = 6
N_GROUPS = 8
GROUP_SIZE = N_EXPERTS // N_GROUPS
TOPK_GROUPS = 4
EXPERT_FF = 704
ROUTED_SCALE = 2.5

LANES = 128
VMEM_BYTES_V7X = 64 * 1024 * 1024
FF_PAD = 768
HALF = D_MODEL // 2
TILE_ROWS = 8
QK_PAD = 256
COND_ROWS = 32
ROW_BLOCK = 256
EXPERT_BLOCK = 256
ROUTE_BLOCK = 512


def _cparams(semantics, vmem_mib):
    assert vmem_mib * 1024 * 1024 < VMEM_BYTES_V7X
    return pltpu.CompilerParams(dimension_semantics=semantics, vmem_limit_bytes=vmem_mib * 1024 * 1024)


def _resident(shape):
    nd = len(shape)
    return pl.BlockSpec(shape, lambda *_: (0,) * nd, pipeline_mode=pl.Buffered(1))


def _dot(a, b):
    return jnp.dot(a, b, preferred_element_type=F32)


def _dot_nt(a, b):
    return lax.dot_general(a, b, (((1,), (1,)), ((), ())), preferred_element_type=F32)


def _rms(xf, g):
    return xf * lax.rsqrt(jnp.mean(xf * xf, axis=-1, keepdims=True) + EPS) * g


def _split_bf16(a):
    hi = a.astype(BF16)
    lo = (a - hi.astype(F32)).astype(BF16)
    return hi, lo


def _pack_halves(lo_f32, hi_f32):
    lo = lax.bitcast_convert_type(lo_f32.astype(BF16).astype(F32), U32) >> 16
    hi = lax.bitcast_convert_type(hi_f32.astype(BF16).astype(F32), U32) & jnp.uint32(0xFFFF0000)
    return lo | hi


def _unpack_halves(w):
    lo = lax.bitcast_convert_type(w << 16, F32)
    hi = lax.bitcast_convert_type(w & jnp.uint32(0xFFFF0000), F32)
    return lo, hi


def _store_token_tiles(ref, packed):
    m = packed.shape[0]
    for s in range(TILE_ROWS):
        ref[pl.ds(s, m, stride=TILE_ROWS), :] = packed[:, s * LANES:(s + 1) * LANES]


def _load_token_tiles(ref, m):
    return [ref[pl.ds(s, m, stride=TILE_ROWS), :] for s in range(TILE_ROWS)]


def _unpack_token_tiles(ref, m):
    los, his = [], []
    for w in _load_token_tiles(ref, m):
        lo, hi = _unpack_halves(w)
        los.append(lo.astype(BF16))
        his.append(hi.astype(BF16))
    return jnp.concatenate(los, axis=1), jnp.concatenate(his, axis=1)


def _silu(x):
    return x * jax.nn.sigmoid(x)


def _adaln_kernel(c_ref, w_ref, b_ref, o_ref):
    a_hi, a_lo = _split_bf16(_silu(c_ref[...]))
    w_hi, w_lo = _split_bf16(w_ref[0])
    o_ref[0] = _dot(a_hi, w_hi) + _dot(a_lo, w_hi) + _dot(a_hi, w_lo) + b_ref[0]


def _adaln(cond, mod_w, mod_b):
    depth, d, n = mod_w.shape
    tn = 512
    return pl.pallas_call(
        _adaln_kernel,
        grid=(depth, n // tn),
        in_specs=[
            pl.BlockSpec((COND_ROWS, d), lambda l, j: (0, 0)),
            pl.BlockSpec((1, d, tn), lambda l, j: (l, 0, j)),
            pl.BlockSpec((1, 1, tn), lambda l, j: (l, 0, j)),
        ],
        out_specs=pl.BlockSpec((1, COND_ROWS, tn), lambda l, j: (l, 0, j)),
        out_shape=jax.ShapeDtypeStruct((depth, COND_ROWS, n), F32),
        compiler_params=_cparams(("parallel", "parallel"), 32),
        name="adaln",
    )(cond, mod_w, mod_b.reshape(depth, 1, n))


NOPE_W = A_HEADS * A_QK_NOPE
ROPE_W = A_HEADS * A_QK_ROPE
EVEN_IN_PAD = A_Q_RANK + A_KV_RANK + B_WIDTH + 2 * A_QK_ROPE


def _even_in_kernel(x_ref, sh_ref, sc_ref, g_ref, win_ref, gq_ref, wuq_ref, gkv_ref, wukv_ref,
                    cos_ref, sin_ref, bdc_ref, bds_ref, q_ref, k_ref, v_ref, uc_ref, us_ref):
    tm = x_ref.shape[1]
    h = _rms(x_ref[0], g_ref[...]) * (1.0 + sc_ref[0]) + sh_ref[0]
    z = _dot(h.astype(BF16), win_ref[...])
    c_q = z[:, :A_Q_RANK]
    c_kv = z[:, A_Q_RANK:A_Q_RANK + A_KV_RANK]
    o = A_Q_RANK + A_KV_RANK
    u_f = z[:, o:o + B_WIDTH].astype(BF16)
    k_pe = z[:, o + B_WIDTH:o + B_WIDTH + A_QK_ROPE]
    k_pe_sw = z[:, o + B_WIDTH + A_QK_ROPE:]
    cos = cos_ref[...]
    sin = sin_ref[...]
    scale = (A_QK_NOPE + A_QK_ROPE) ** -0.5

    q = _dot(_rms(c_q, gq_ref[...]).astype(BF16), wuq_ref[...]) * scale
    q_pe = q[:, NOPE_W:NOPE_W + ROPE_W] * cos + q[:, NOPE_W + ROPE_W:] * sin
    kv = _dot(_rms(c_kv, gkv_ref[...]).astype(BF16), wukv_ref[...])
    k_pe_r = (k_pe * cos[:, :A_QK_ROPE] + k_pe_sw * sin[:, :A_QK_ROPE]).astype(BF16)
    zeros = jnp.zeros((tm, QK_PAD - A_QK_NOPE - A_QK_ROPE), BF16)
    for hd in range(A_HEADS):
        n0 = hd * A_QK_NOPE
        r0 = hd * A_QK_ROPE
        q_ref[0, hd, :, :A_QK_NOPE] = q[:, n0:n0 + A_QK_NOPE].astype(BF16)
        q_ref[0, hd, :, A_QK_NOPE:A_QK_NOPE + A_QK_ROPE] = q_pe[:, r0:r0 + A_QK_ROPE].astype(BF16)
        q_ref[0, hd, :, A_QK_NOPE + A_QK_ROPE:] = zeros
        k_ref[0, hd, :, :A_QK_NOPE] = kv[:, n0:n0 + A_QK_NOPE].astype(BF16)
        k_ref[0, hd, :, A_QK_NOPE:A_QK_NOPE + A_QK_ROPE] = k_pe_r
        k_ref[0, hd, :, A_QK_NOPE + A_QK_ROPE:] = zeros
        v_ref[0, hd] = kv[:, NOPE_W + n0:NOPE_W + n0 + A_V_DIM].astype(BF16)
    uc_ref[0] = _dot(u_f, bdc_ref[...]).astype(BF16)
    us_ref[0] = _dot(u_f, bds_ref[...]).astype(BF16)


def _even_in(x, sh, sc, g, win, gq, wuq, gkv, wukv, cos, sin, bdc, bds):
    b, l, d = x.shape
    tm = ROW_BLOCK
    row = lambda bi, i: (bi, i, 0)
    per_b = lambda bi, i: (bi, 0, 0)
    heads = lambda bi, i: (bi, 0, i, 0)
    return pl.pallas_call(
        _even_in_kernel,
        grid=(b, l // tm),
        in_specs=[
            pl.BlockSpec((1, tm, d), row),
            pl.BlockSpec((1, 1, d), per_b),
            pl.BlockSpec((1, 1, d), per_b),
            _resident(g.shape), _resident(win.shape), _resident(gq.shape), _resident(wuq.shape),
            _resident(gkv.shape), _resident(wukv.shape),
            pl.BlockSpec((tm, ROPE_W), lambda bi, i: (i, 0)),
            pl.BlockSpec((tm, ROPE_W), lambda bi, i: (i, 0)),
            _resident(bdc.shape), _resident(bds.shape),
        ],
        out_specs=[
            pl.BlockSpec((1, A_HEADS, tm, QK_PAD), heads),
            pl.BlockSpec((1, A_HEADS, tm, QK_PAD), heads),
            pl.BlockSpec((1, A_HEADS, tm, A_V_DIM), heads),
            pl.BlockSpec((1, tm, B_WIDTH), row),
            pl.BlockSpec((1, tm, B_WIDTH), row),
        ],
        out_shape=[
            jax.ShapeDtypeStruct((b, A_HEADS, l, QK_PAD), BF16),
            jax.ShapeDtypeStruct((b, A_HEADS, l, QK_PAD), BF16),
            jax.ShapeDtypeStruct((b, A_HEADS, l, A_V_DIM), BF16),
            jax.ShapeDtypeStruct((b, l, B_WIDTH), BF16),
            jax.ShapeDtypeStruct((b, l, B_WIDTH), BF16),
        ],
        compiler_params=_cparams(("parallel", "parallel"), 56),
        name="even_in",
    )(x, sh, sc, g, win, gq, wuq, gkv, wukv, cos, sin, bdc, bds)


ATTN_Q_BLOCK = 512


def _attn_kernel(q_ref, kc_ref, ko_ref, vc_ref, vo_ref, o_ref):
    kc = kc_ref[0, 0]
    ko = ko_ref[0, 0]
    vc = vc_ref[0, 0]
    vo = vo_ref[0, 0]
    tq = ATTN_Q_BLOCK

    def body(i, carry):
        r0 = pl.multiple_of(i * tq, tq)
        q = q_ref[0, 0, pl.ds(r0, tq), :]
        s_c = _dot_nt(q, kc)
        s_o = _dot_nt(q, ko)
        m = jnp.maximum(jnp.max(s_c, axis=-1, keepdims=True), jnp.max(s_o, axis=-1, keepdims=True))
        p_c = jnp.exp(s_c - m)
        p_o = jnp.exp(s_o - m)
        denom = jnp.sum(p_c, axis=-1, keepdims=True) + jnp.sum(p_o, axis=-1, keepdims=True)
        o = _dot(p_c.astype(BF16), vc) + _dot(p_o.astype(BF16), vo)
        o_ref[0, pl.ds(r0, tq), :] = (o / denom).astype(BF16)
        return carry

    lax.fori_loop(0, q_ref.shape[2] // tq, body, 0)


def _attention(q, kc, ko, vc, vo):
    b, h, s, _ = q.shape
    lc = kc.shape[2]
    bh = lambda bi, hi: (bi, hi, 0, 0)
    return pl.pallas_call(
        _attn_kernel,
        grid=(b, h),
        in_specs=[
            pl.BlockSpec((1, 1, s, QK_PAD), bh),
            pl.BlockSpec((1, 1, lc, QK_PAD), bh),
            pl.BlockSpec((1, 1, s, QK_PAD), bh),
            pl.BlockSpec((1, 1, lc, A_V_DIM), bh),
            pl.BlockSpec((1, 1, s, A_V_DIM), bh),
        ],
        out_specs=pl.BlockSpec((1, s, A_V_DIM), lambda bi, hi: (bi, 0, hi)),
        out_shape=jax.ShapeDtypeStruct((b, s, h * A_V_DIM), BF16),
        compiler_params=_cparams(("parallel", "parallel"), 48),
        name="attention",
    )(q, kc, ko, vc, vo)


def _fourier_kernel(ac_ref, as_ref, uc_ref, us_ref, o_ref):
    o_ref[0] = (_dot(ac_ref[...], uc_ref[0]) + _dot(as_ref[...], us_ref[0])).astype(BF16)


def _fourier(a_c, a_s, uc, us):
    b, s, w = uc.shape
    tm = 512
    return pl.pallas_call(
        _fourier_kernel,
        grid=(b, s // tm),
        in_specs=[
            pl.BlockSpec((tm, s), lambda bi, i: (i, 0)),
            pl.BlockSpec((tm, s), lambda bi, i: (i, 0)),
            pl.BlockSpec((1, s, w), lambda bi, i: (bi, 0, 0)),
            pl.BlockSpec((1, s, w), lambda bi, i: (bi, 0, 0)),
        ],
        out_specs=pl.BlockSpec((1, tm, w), lambda bi, i: (bi, i, 0)),
        out_shape=jax.ShapeDtypeStruct((b, s, w), BF16),
        compiler_params=_cparams(("parallel", "parallel"), 32),
        name="fourier",
    )(a_c, a_s, uc, us)


def _mix_epilogue(mix, x_ref, gt_ref, sh_ref, sc_ref, gpost_ref, gffn_ref, wrh_ref, wrl_ref,
                  xo_ref, hx_ref, lg_ref):
    xn = x_ref[0] + gt_ref[0] * _rms(mix, gpost_ref[...])
    xo_ref[0] = xn
    t = _rms(xn, gffn_ref[...]) * (1.0 + sc_ref[0]) + sh_ref[0]
    t_hi, t_lo = _split_bf16(t)
    wrh = wrh_ref[...]
    lg = _dot(t_hi, wrh) + _dot(t_lo, wrh) + _dot(t_hi, wrl_ref[...])
    lg_ref[...] = lg.T[:N_EXPERTS, :]
    tq = t_hi.astype(F32)
    _store_token_tiles(hx_ref, _pack_halves(tq[:, :HALF], tq[:, HALF:]))


def _even_out_kernel(oa_ref, ob_ref, w_ref, *rest):
    mix = _dot(oa_ref[0], w_ref[:A_WIDTH, :]) + _dot(ob_ref[0], w_ref[A_WIDTH:, :])
    _mix_epilogue(mix, *rest)


def _odd_out_kernel(oc_ref, s_ref, sp_ref, sn_ref, gb_ref, cw_ref, w_ref, *rest):
    i = pl.program_id(1)
    tm = s_ref.shape[1]
    s = s_ref[0].astype(F32)
    prev_row = jnp.where(i > 0, sp_ref[0, 15:16, :].astype(F32), 0.0)
    next_row = jnp.where(i < pl.num_programs(1) - 1, sn_ref[0, 0:1, :].astype(F32), 0.0)
    row = lax.broadcasted_iota(jnp.int32, (tm, 1), 0)
    s_dn = jnp.where(row == 0, prev_row, pltpu.roll(s, 1, axis=0))
    s_up = jnp.where(row == tm - 1, next_row, pltpu.roll(s, tm - 1, axis=0))
    cw = cw_ref[...]
    conv = cw[0:1, :] * s_dn + cw[1:2, :] * s + cw[2:3, :] * s_up
    od = (gb_ref[0].astype(F32) * conv).astype(BF16)
    mix = _dot(oc_ref[0], w_ref[:C_WIDTH, :]) + _dot(od, w_ref[C_WIDTH:, :])
    _mix_epilogue(mix, *rest)


def _epilogue_specs(b, s, d, tm):
    per_b = lambda bi, i: (bi, 0, 0)
    row = lambda bi, i: (bi, i, 0)
    nblk = s // tm
    in_specs = [
        pl.BlockSpec((1, tm, d), row),
        pl.BlockSpec((1, 1, d), per_b),
        pl.BlockSpec((1, 1, d), per_b),
        pl.BlockSpec((1, 1, d), per_b),
        _resident((1, d)), _resident((1, d)),
        _resident((d, LANES)), _resident((d, LANES)),
    ]
    out_specs = [
        pl.BlockSpec((1, tm, d), row),
        pl.BlockSpec((tm * TILE_ROWS, LANES), lambda bi, i: (bi * nblk + i, 0)),
        pl.BlockSpec((N_EXPERTS, tm), lambda bi, i: (0, bi * nblk + i)),
    ]
    out_shape = [
        jax.ShapeDtypeStruct((b, s, d), F32),
        jax.ShapeDtypeStruct((b * s * TILE_ROWS, LANES), U32),
        jax.ShapeDtypeStruct((N_EXPERTS, b * s), F32),
    ]
    return in_specs, out_specs, out_shape


def _even_out(oa, ob, w_out, x, gt, sh, sc, gpost, gffn, wrh, wrl):
    b, s, d = x.shape
    tm = ROW_BLOCK
    row = lambda bi, i: (bi, i, 0)
    ep_in, out_specs, out_shape = _epilogue_specs(b, s, d, tm)
    return pl.pallas_call(
        _even_out_kernel,
        grid=(b, s // tm),
        in_specs=[pl.BlockSpec((1, tm, A_WIDTH), row), pl.BlockSpec((1, tm, B_WIDTH), row),
                  _resident(w_out.shape)] + ep_in,
        out_specs=out_specs, out_shape=out_shape,
        compiler_params=_cparams(("parallel", "parallel"), 48),
        name="even_out",
    )(oa, ob, w_out, x, gt, sh, sc, gpost, gffn, wrh, wrl)


def _odd_out(oc, s_, gb, conv_w, w_out, x, gt, sh, sc, gpost, gffn, wrh, wrl):
    b, s, d = x.shape
    tm = ROW_BLOCK
    halo = 16
    nh = s // halo
    row = lambda bi, i: (bi, i, 0)
    ep_in, out_specs, out_shape = _epilogue_specs(b, s, d, tm)
    return pl.pallas_call(
        _odd_out_kernel,
        grid=(b, s // tm),
        in_specs=[
            pl.BlockSpec((1, tm, C_WIDTH), row),
            pl.BlockSpec((1, tm, D_WIDTH), row),
            pl.BlockSpec((1, halo, D_WIDTH), lambda bi, i: (bi, jnp.maximum(i * (tm // halo) - 1, 0), 0)),
            pl.BlockSpec((1, halo, D_WIDTH), lambda bi, i: (bi, jnp.minimum((i + 1) * (tm // halo), nh - 1), 0)),
            pl.BlockSpec((1, tm, D_WIDTH), row),
            _resident(conv_w.shape),
            _resident(w_out.shape),
        ] + ep_in,
        out_specs=out_specs, out_shape=out_shape,
        compiler_params=_cparams(("parallel", "parallel"), 48),
        name="odd_out",
    )(oc, s_, s_, s_, gb, conv_w, w_out, x, gt, sh, sc, gpost, gffn, wrh, wrl)


def _odd_in_kernel(x_ref, sh_ref, sc_ref, g_ref, win_ref, gv_ref, ws_ref, bs_ref, oc_ref, s_ref, gb_ref):
    tm = x_ref.shape[1]
    hb = (_rms(x_ref[0], g_ref[...]) * (1.0 + sc_ref[0]) + sh_ref[0]).astype(BF16)
    u = jax.nn.gelu(_dot(hb, win_ref[:, :C_WIDTH]))
    v = jax.nn.gelu(_dot(hb, win_ref[:, C_WIDTH:2 * C_WIDTH]))
    gv = gv_ref[...]
    bs = bs_ref[...]
    for hd in range(C_HEADS):
        c0 = hd * C_HEAD_DIM
        vh = _rms(v[:, c0:c0 + C_HEAD_DIM], gv[:, c0:c0 + C_HEAD_DIM]).astype(BF16)
        w = ws_ref[hd]
        bias = bs[:, hd:hd + 1]
        for n in range(tm // CHUNK):
            p0 = n * CHUNK
            sv = _dot(w, vh[p0:p0 + CHUNK, :]) + bias
            oc_ref[0, p0:p0 + CHUNK, c0:c0 + C_HEAD_DIM] = (u[p0:p0 + CHUNK, c0:c0 + C_HEAD_DIM] * sv).astype(BF16)
    o = 2 * C_WIDTH
    hd_ = _dot(hb, win_ref[:, o:o + D_WIDTH])
    gb_ref[0] = _dot(hb, win_ref[:, o + D_WIDTH:o + 2 * D_WIDTH]).astype(BF16)
    gc = _dot(hb, win_ref[:, o + 2 * D_WIDTH:])
    s_ref[0] = (gc * hd_).astype(BF16)


def _odd_in(x, sh, sc, g, win, gv, ws, bs_t):
    b, s, d = x.shape
    tm = ROW_BLOCK
    row = lambda bi, i: (bi, i, 0)
    per_b = lambda bi, i: (bi, 0, 0)
    out = jax.ShapeDtypeStruct((b, s, C_WIDTH), BF16)
    return pl.pallas_call(
        _odd_in_kernel,
        grid=(b, s // tm),
        in_specs=[
            pl.BlockSpec((1, tm, d), row),
            pl.BlockSpec((1, 1, d), per_b),
            pl.BlockSpec((1, 1, d), per_b),
            _resident(g.shape), _resident(win.shape), _resident(gv.shape), _resident(ws.shape),
            _resident(bs_t.shape),
        ],
        out_specs=[pl.BlockSpec((1, tm, C_WIDTH), row)] * 3,
        out_shape=[out, out, out],
        compiler_params=_cparams(("parallel", "parallel"), 56),
        name="odd_in",
    )(x, sh, sc, g, win, gv, ws, bs_t)


def _first_index_of_max(vals, idx, sentinel):
    m = jnp.max(vals, axis=0, keepdims=True)
    first = jnp.min(jnp.where(vals == m, idx, sentinel), axis=0, keepdims=True)
    return m, first


def _route_kernel(lg_ref, b_ref, e_ref, w_ref, c_ref):
    tn = lg_ref.shape[1]
    scores = jax.nn.sigmoid(lg_ref[...])
    sel = scores + b_ref[...]
    row8 = lax.broadcasted_iota(jnp.int32, (8, tn), 0).astype(F32)
    neg = -jnp.inf
    gs = jnp.zeros((N_GROUPS, tn), F32)
    for g in range(N_GROUPS):
        blk = sel[g * GROUP_SIZE:(g + 1) * GROUP_SIZE, :]
        m1, i1 = _first_index_of_max(blk, row8, float(GROUP_SIZE))
        m2 = jnp.max(jnp.where(row8 == i1, neg, blk), axis=0, keepdims=True)
        gs = jnp.where(row8 == float(g), m1 + m2, gs)
    keep = jnp.zeros((N_GROUPS, tn), F32)
    cur = gs
    for _ in range(TOPK_GROUPS):
        _, first = _first_index_of_max(cur, row8, float(N_GROUPS))
        pick = row8 == first
        keep = jnp.where(pick, 1.0, keep)
        cur = jnp.where(pick, neg, cur)
    masked = jnp.concatenate(
        [jnp.where(jnp.max(jnp.where(row8 == float(g), keep, 0.0), axis=0, keepdims=True) > 0.5,
                   sel[g * GROUP_SIZE:(g + 1) * GROUP_SIZE, :], neg) for g in range(N_GROUPS)],
        axis=0)
    eidx = lax.broadcasted_iota(jnp.int32, (N_EXPERTS, tn), 0).astype(F32)
    e_out = jnp.zeros((8, tn), F32)
    w_out = jnp.zeros((8, tn), F32)
    total = jnp.zeros((1, tn), F32)
    picked = jnp.zeros((N_EXPERTS, tn), F32)
    for k in range(TOP_K):
        _, first = _first_index_of_max(masked, eidx, float(N_EXPERTS))
        pick = eidx == first
        wk = jnp.sum(jnp.where(pick, scores, 0.0), axis=0, keepdims=True)
        masked = jnp.where(pick, neg, masked)
        picked = jnp.where(pick, 1.0, picked)
        e_out = jnp.where(row8 == float(k), first, e_out)
        w_out = jnp.where(row8 == float(k), wk, w_out)
        total = total + wk
    e_ref[...] = e_out.astype(jnp.int32)
    w_ref[...] = w_out / total * ROUTED_SCALE
    c_ref[...] = jnp.broadcast_to(jnp.sum(picked, axis=1, keepdims=True), c_ref.shape).astype(jnp.int32)


def _route(logits_t, bias):
    e, t = logits_t.shape
    tn = ROUTE_BLOCK
    nblk = t // tn
    top_e, wts, cnt = pl.pallas_call(
        _route_kernel,
        grid=(nblk,),
        in_specs=[pl.BlockSpec((e, tn), lambda i: (0, i)), pl.BlockSpec((e, 1), lambda i: (0, 0))],
        out_specs=[pl.BlockSpec((8, tn), lambda i: (0, i))] * 2 + [pl.BlockSpec((e, LANES), lambda i: (0, i))],
        out_shape=[jax.ShapeDtypeStruct((8, t), jnp.int32), jax.ShapeDtypeStruct((8, t), F32),
                   jax.ShapeDtypeStruct((e, nblk * LANES), jnp.int32)],
        compiler_params=_cparams(("parallel",), 32),
        name="route",
    )(logits_t, bias.reshape(e, 1))
    counts = jnp.sum(cnt.reshape(e, nblk, LANES)[:, :, 0], axis=1)
    return top_e, wts, counts


def _dispatch_tables(top_e, counts, n_tok):
    bm = EXPERT_BLOCK
    n_asg = n_tok * TOP_K
    n_blk = n_asg // bm
    n_steps = n_blk + N_EXPERTS
    i32 = jnp.int32
    flat_e = top_e[:TOP_K].T.reshape(n_asg)
    order = jnp.argsort(flat_e, stable=True).astype(i32)
    tok = order // TOP_K
    k = order - tok * TOP_K
    src = jnp.concatenate([tok * TILE_ROWS, jnp.zeros((bm,), i32)]).reshape(n_blk + 1, 1, bm)
    spare = (n_asg + jnp.arange(bm, dtype=i32)) * TILE_ROWS
    dst = jnp.concatenate([spare, (k * n_tok + tok) * TILE_ROWS]).reshape(n_blk + 1, 1, bm)

    experts = jnp.arange(N_EXPERTS, dtype=i32)
    ends = jnp.cumsum(counts).astype(i32)
    starts = ends - counts
    first_blk = starts // bm
    n_e_blk = jnp.where(counts > 0, (ends - 1) // bm - first_blk + 1, 0)
    item_end = jnp.cumsum(n_e_blk).astype(i32)
    item_start = item_end - n_e_blk
    n_items = item_end[-1]
    e_last = jnp.max(jnp.where(counts > 0, experts, 0))
    w = jnp.arange(n_steps, dtype=i32)
    active = w < n_items
    e_w = jnp.where(active, jnp.sum((item_end[None, :] <= w[:, None]).astype(i32), axis=1), e_last)
    blk_w = jnp.where(active, first_blk[e_w] + w - item_start[e_w], n_blk)
    lo_w = jnp.clip(starts[e_w] - blk_w * bm, 0, bm)
    hi_w = jnp.clip(ends[e_w] - blk_w * bm, 0, bm)
    neg1 = jnp.full((1,), -1, i32)
    first_of_blk = (blk_w != jnp.concatenate([neg1, blk_w[:-1]])).astype(i32)
    first_of_e = ((e_w != jnp.concatenate([neg1, e_w[:-1]])) & active).astype(i32)
    cand = jnp.where(counts > 0, experts, N_EXPERTS)
    suffix_min = lax.cummin(cand[::-1])[::-1]
    nxt = jnp.concatenate([suffix_min[1:], jnp.full((1,), N_EXPERTS, i32)])
    nxt = jnp.where(nxt >= N_EXPERTS, -1, nxt)
    return src, dst, (blk_w, e_w, lo_w, hi_w, first_of_blk, first_of_e, nxt[e_w], n_items.reshape(1))


FF_CHUNK = 256
DOWN_CHUNK = 512


def _swiglu_tiles(x_ref, m, wg_ref, wu_ref, wd_ref, between=lambda: None):
    x_lo, x_hi = _unpack_token_tiles(x_ref, m)
    hs = []
    for c in range(FF_PAD // FF_CHUNK):
        cols = slice(c * FF_CHUNK, (c + 1) * FF_CHUNK)
        g = _dot(x_lo, wg_ref[:HALF, cols]) + _dot(x_hi, wg_ref[HALF:, cols])
        between()
        u = _dot(x_lo, wu_ref[:HALF, cols]) + _dot(x_hi, wu_ref[HALF:, cols])
        between()
        hs.append((_silu(g) * u).astype(BF16))
    h = jnp.concatenate(hs, axis=1)
    ys = []
    for n in range(D_MODEL // DOWN_CHUNK):
        ys.append(_dot(h, wd_ref[:, n * DOWN_CHUNK:(n + 1) * DOWN_CHUNK]))
        between()
    return ys


def _packed_chunks(ys):
    per = DOWN_CHUNK // LANES
    half_chunks = HALF // DOWN_CHUNK
    out = []
    for s in range(TILE_ROWS):
        c0 = (s % per) * LANES
        lo = ys[s // per][:, c0:c0 + LANES]
        hi = ys[half_chunks + s // per][:, c0:c0 + LANES]
        out.append(_pack_halves(lo, hi))
    return out


N_DMA_SLOTS = (FF_PAD // FF_CHUNK) * 2 + D_MODEL // DOWN_CHUNK


def _expert_kernel(blk_ref, exp_ref, lo_ref, hi_ref, fb_ref, fe_ref, nxt_ref, ni_ref,
                   src0_ref, src1_ref, dst_ref, hx_hbm, wg_hbm, wu_hbm, wd_hbm,
                   ys_hbm,
                   xbuf0, xbuf1, ybuf0, ybuf1, sg, su, sd, wg, wu, wd, gsem, ssem, wsem, *, layer, n_blk):
    bm = EXPERT_BLOCK
    w = pl.program_id(0)
    n_items = ni_ref[0]
    blk = blk_ref[w]
    expert = exp_ref[w]
    active = w < n_items
    xbufs = (xbuf0, xbuf1)
    ybufs = (ybuf0, ybuf1)
    tile = lambda r: pl.ds(r * TILE_ROWS, TILE_ROWS)

    def weight_copies(e):
        return (pltpu.make_async_copy(wg_hbm.at[layer, e], sg, wsem.at[0]),
                pltpu.make_async_copy(wu_hbm.at[layer, e], su, wsem.at[1]),
                pltpu.make_async_copy(wd_hbm.at[layer, e], sd, wsem.at[2]))

    def gather_starts(src_ref, p):
        def one(r):
            row = pl.multiple_of(src_ref[0, 0, r], TILE_ROWS)
            pltpu.make_async_copy(hx_hbm.at[pl.ds(row, TILE_ROWS), :], xbufs[p].at[tile(r), :], gsem.at[p]).start()
        return [functools.partial(one, r) for r in range(bm)]

    def scatter_starts(p):
        def one(r):
            row = pl.multiple_of(dst_ref[0, 0, r], TILE_ROWS)
            pltpu.make_async_copy(ybufs[p].at[tile(r), :], ys_hbm.at[pl.ds(row, TILE_ROWS), :], ssem.at[0]).start()
        return [functools.partial(one, r) for r in range(bm)]

    def wait_gather(p):
        pltpu.make_async_copy(hx_hbm.at[pl.ds(0, bm * TILE_ROWS), :], xbufs[p], gsem.at[p]).wait()

    def wait_scatter(p):
        pltpu.make_async_copy(ybufs[p], ys_hbm.at[pl.ds(0, bm * TILE_ROWS), :], ssem.at[0]).wait()

    @pl.when(w == 0)
    def _():
        wg[:, EXPERT_FF:] = jnp.zeros((D_MODEL, FF_PAD - EXPERT_FF), BF16)
        wu[:, EXPERT_FF:] = jnp.zeros((D_MODEL, FF_PAD - EXPERT_FF), BF16)
        wd[EXPERT_FF:, :] = jnp.zeros((FF_PAD - EXPERT_FF, D_MODEL), BF16)
        ybuf1[...] = jnp.zeros(ybuf1.shape, U32)
        for c in weight_copies(expert):
            c.start()
        for start in gather_starts(src0_ref, 0):
            start()

    @pl.when(jnp.logical_and(active, fe_ref[w] == 1))
    def _():
        for c in weight_copies(expert):
            c.wait()
        rows = 256

        def cast_in(i, carry):
            r = pl.ds(pl.multiple_of(i * rows, rows), rows)
            wg[r, :EXPERT_FF] = sg[r, :].astype(BF16)
            wu[r, :EXPERT_FF] = su[r, :].astype(BF16)
            return carry

        lax.fori_loop(0, D_MODEL // rows, cast_in, 0)
        rows_d = 64

        def cast_down(i, carry):
            r = pl.ds(pl.multiple_of(i * rows_d, rows_d), rows_d)
            wd[r, :] = sd[r, :].astype(BF16)
            return carry

        lax.fori_loop(0, EXPERT_FF // rows_d, cast_down, 0)

        @pl.when(nxt_ref[w] >= 0)
        def _():
            for c in weight_copies(nxt_ref[w]):
                c.start()

    def first_item_of_block(p):
        @pl.when(blk > 0)
        def _():
            wait_scatter(p)
        wait_gather(p)
        pending = gather_starts(src1_ref, 1 - p) + scatter_starts(1 - p)
        per_slot = -(-len(pending) // N_DMA_SLOTS)

        def between():
            for start in pending[:per_slot]:
                start()
            del pending[:per_slot]

        ys = _swiglu_tiles(xbufs[p], bm, wg, wu, wd, between)
        assert not pending
        for s, chunk in enumerate(_packed_chunks(ys)):
            ybufs[p][pl.ds(s, bm, stride=TILE_ROWS), :] = chunk

    def later_item_of_block(p):
        ys = _swiglu_tiles(xbufs[p], bm, wg, wu, wd)
        row = lax.broadcasted_iota(jnp.int32, (bm, 1), 0)
        mine = jnp.logical_and(row >= lo_ref[w], row < hi_ref[w])
        for s, chunk in enumerate(_packed_chunks(ys)):
            idx = pl.ds(s, bm, stride=TILE_ROWS)
            ybufs[p][idx, :] = jnp.where(mine, chunk, ybufs[p][idx, :])

    parity = lax.rem(blk, 2)
    first = fb_ref[w] == 1
    for p in range(2):
        pl.when(jnp.logical_and(active, jnp.logical_and(first, parity == p)))(
            functools.partial(first_item_of_block, p))
        pl.when(jnp.logical_and(active, jnp.logical_and(jnp.logical_not(first), parity == p)))(
            functools.partial(later_item_of_block, p))

    @pl.when(w == n_items)
    def _():
        last = (n_blk - 1) % 2
        wait_scatter(1 - last)
        wait_gather(1 - last)
        for start in scatter_starts(last):
            start()
        wait_scatter(last)


def _experts(hx, src, dst, items, w_gate, w_up, w_down, layer):
    bm = EXPERT_BLOCK
    n_blk = src.shape[0] - 1
    n_steps = items[0].shape[0]
    n_out_rows = (n_blk + 1) * bm * TILE_ROWS
    smem = functools.partial(pl.BlockSpec, (1, 1, bm), memory_space=pltpu.SMEM)
    grid_spec = pltpu.PrefetchScalarGridSpec(
        num_scalar_prefetch=len(items),
        grid=(n_steps,),
        in_specs=[
            smem(index_map=lambda w, *_: (0, 0, 0)),
            smem(index_map=lambda w, blk, *_: (jnp.minimum(blk[w] + 1, n_blk), 0, 0)),
            smem(index_map=lambda w, blk, *_: (blk[w], 0, 0)),
            pl.BlockSpec(memory_space=pl.ANY),
            pl.BlockSpec(memory_space=pl.ANY),
            pl.BlockSpec(memory_space=pl.ANY),
            pl.BlockSpec(memory_space=pl.ANY),
        ],
        out_specs=pl.BlockSpec(memory_space=pl.ANY),
        scratch_shapes=[
            pltpu.VMEM((bm * TILE_ROWS, LANES), U32), pltpu.VMEM((bm * TILE_ROWS, LANES), U32),
            pltpu.VMEM((bm * TILE_ROWS, LANES), U32), pltpu.VMEM((bm * TILE_ROWS, LANES), U32),
            pltpu.VMEM((D_MODEL, EXPERT_FF), F32), pltpu.VMEM((D_MODEL, EXPERT_FF), F32),
            pltpu.VMEM((EXPERT_FF, D_MODEL), F32),
            pltpu.VMEM((D_MODEL, FF_PAD), BF16), pltpu.VMEM((D_MODEL, FF_PAD), BF16),
            pltpu.VMEM((FF_PAD, D_MODEL), BF16),
            pltpu.SemaphoreType.DMA((2,)),
            pltpu.SemaphoreType.DMA((1,)),
            pltpu.SemaphoreType.DMA((3,)),
        ],
    )
    return pl.pallas_call(
        functools.partial(_expert_kernel, layer=layer, n_blk=n_blk),
        grid_spec=grid_spec,
        out_shape=jax.ShapeDtypeStruct((n_out_rows, LANES), U32),
        compiler_params=_cparams(("arbitrary",), 48),
        name="experts",
    )(*items, src, src, dst, hx, w_gate, w_up, w_down)


def _combine_kernel(hx_ref, y0, y1, y2, y3, y4, y5, wt_ref, wg_ref, wu_ref, wd_ref, x_ref, gt_ref, g_ref, o_ref):
    tm = x_ref.shape[1]
    shared = jnp.concatenate(_swiglu_tiles(hx_ref, tm, wg_ref, wu_ref, wd_ref), axis=1)
    wt = wt_ref[...]
    acc_lo = [None] * TILE_ROWS
    acc_hi = [None] * TILE_ROWS
    for k, y_ref in enumerate((y0, y1, y2, y3, y4, y5)):
        wk = wt[:, k:k + 1]
        for s, word in enumerate(_load_token_tiles(y_ref, tm)):
            lo, hi = _unpack_halves(word)
            acc_lo[s] = wk * lo if k == 0 else acc_lo[s] + wk * lo
            acc_hi[s] = wk * hi if k == 0 else acc_hi[s] + wk * hi
    ff = jnp.concatenate(acc_lo + acc_hi, axis=-1) + shared
    o_ref[0] = x_ref[0] + gt_ref[0] * _rms(ff, g_ref[...])


def _combine(hx, ys, wts_t, wg, wu, wd, x, gt, g):
    b, s, d = x.shape
    tm = ROW_BLOCK
    nblk = s // tm
    n_tok = b * s
    row = lambda bi, i: (bi, i, 0)
    tiles = (tm * TILE_ROWS, LANES)
    y_specs = [pl.BlockSpec(tiles, functools.partial(lambda bi, i, k: (k * (n_tok // tm) + bi * nblk + i, 0), k=k))
               for k in range(TOP_K)]
    return pl.pallas_call(
        _combine_kernel,
        grid=(b, nblk),
        in_specs=[pl.BlockSpec(tiles, lambda bi, i: (bi * nblk + i, 0))] + y_specs + [
            pl.BlockSpec((tm, 8), lambda bi, i: (bi * nblk + i, 0)),
            _resident(wg.shape), _resident(wu.shape), _resident(wd.shape),
            pl.BlockSpec((1, tm, d), row),
            pl.BlockSpec((1, 1, d), lambda bi, i: (bi, 0, 0)),
            _resident(g.shape),
        ],
        out_specs=pl.BlockSpec((1, tm, d), row),
        out_shape=jax.ShapeDtypeStruct((b, s, d), F32),
        compiler_params=_cparams(("parallel", "parallel"), 48),
        name="combine",
    )(hx, ys, ys, ys, ys, ys, ys, wts_t, wg, wu, wd, x, gt, g)


def _rope_tables(seq):
    t = jnp.arange(seq, dtype=jnp.int32)
    row = (t // GRID_W).astype(F32)
    col = (t % GRID_W).astype(F32)
    per_axis = A_QK_ROPE // 4
    inv = ROPE_BASE ** (-jnp.arange(per_axis, dtype=F32) / per_axis)
    ang = jnp.concatenate([row[:, None] * inv, col[:, None] * inv], axis=-1)
    cos = jnp.cos(ang)
    sin = jnp.sin(ang)
    cos64 = jnp.concatenate([cos, cos], axis=-1)
    sin64 = jnp.concatenate([-sin, sin], axis=-1)
    return jnp.tile(cos64, (1, A_HEADS)), jnp.tile(sin64, (1, A_HEADS))


def _dft_tables(n):
    k = jnp.arange(n, dtype=jnp.int32)
    ang = ((k[:, None] * k[None, :]) % n).astype(F32) * (2.0 * jnp.pi / n)
    scale = n ** -0.5
    return jnp.cos(ang) * scale, -jnp.sin(ang) * scale


def _swap_halves(w):
    half = w.shape[-1] // 2
    return jnp.concatenate([w[..., half:], w[..., :half]], axis=-1)


def _pad_ff(w, axis):
    pad = [(0, 0)] * w.ndim
    pad[axis] = (0, FF_PAD - EXPERT_FF)
    return jnp.pad(w.astype(BF16), pad)


def _router_parts(w_router):
    w = jnp.pad(w_router, ((0, 0), (0, LANES - N_EXPERTS)))
    hi = w.astype(BF16)
    lo = (w - hi.astype(F32)).astype(BF16)
    return hi, lo


def _moe(hx, logits_t, layer, b_router, w_gate, w_up, w_down, sg, su, sd, x, gt, g_post):
    b, s, d = x.shape
    n_tok = b * s
    top_e, wts, counts = _route(logits_t, b_router)
    src, dst, items = _dispatch_tables(top_e, counts, n_tok)
    ys = _experts(hx, src, dst, items, w_gate, w_up, w_down, layer)
    return _combine(hx, ys, wts.T, _pad_ff(sg, 1), _pad_ff(su, 1), _pad_ff(sd, 0), x, gt, g_post)


def kernel(x, c, ctx, c_ctx, mod_w, mod_b, norm_g, a_w_in, a_g_q, a_w_uq, a_g_kv, a_w_ukv, a_w_out, o_w_in, o_g_v, o_w_s, o_b_s, o_conv_w, o_w_out, moe_w_router, moe_b_router, moe_w_gate, moe_w_up, moe_w_down, sh_w_gate, sh_w_up, sh_w_down):
    b, s, d = x.shape
    lc = ctx.shape[1]

    cond = jnp.concatenate([c, c_ctx[None, :], jnp.zeros((COND_ROWS - b - 1, d), F32)], axis=0)
    mod = _adaln(cond, mod_w, mod_b)

    def mod_rows(layer, rows):
        m = mod[layer, rows]
        return [m[:, None, j * d:(j + 1) * d] for j in range(6)]

    def moe(layer, hx, lg, x, gt, g_post):
        return _moe(hx, lg, layer, moe_b_router[layer], moe_w_gate, moe_w_up, moe_w_down,
                    sh_w_gate[layer], sh_w_up[layer], sh_w_down[layer], x, gt, g_post)

    sh1, sc1, gt1, sh2, sc2, gt2 = mod_rows(0, slice(0, b))
    csh1, csc1 = [jnp.broadcast_to(m, (b, 1, d)) for m in mod_rows(0, slice(b, b + 1))[:2]]
    g = norm_g[0]
    w_in = a_w_in[0]
    o = A_Q_RANK + A_KV_RANK
    k_pe_w = w_in[:, o:o + A_QK_ROPE]
    win = jnp.concatenate([w_in[:, :o], w_in[:, o + A_QK_ROPE:], k_pe_w, _swap_halves(k_pe_w)], axis=1).astype(BF16)
    wq = a_w_uq[0].reshape(A_Q_RANK, A_HEADS, A_QK_NOPE + A_QK_ROPE)
    wq_pe = wq[:, :, A_QK_NOPE:]
    wuq = jnp.concatenate([wq[:, :, :A_QK_NOPE].reshape(A_Q_RANK, NOPE_W), wq_pe.reshape(A_Q_RANK, ROPE_W),
                           _swap_halves(wq_pe).reshape(A_Q_RANK, ROPE_W)], axis=1).astype(BF16)
    wkv = a_w_ukv[0].reshape(A_KV_RANK, A_HEADS, A_QK_NOPE + A_V_DIM)
    wukv = jnp.concatenate([wkv[:, :, :A_QK_NOPE].reshape(A_KV_RANK, NOPE_W),
                            wkv[:, :, A_QK_NOPE:].reshape(A_KV_RANK, A_WIDTH)], axis=1).astype(BF16)
    gq = a_g_q[0][None, :]
    gkv = a_g_kv[0][None, :]
    cos, sin = _rope_tables(s)
    cc, cs = _dft_tables(B_GROUP_DIM)
    eye = jnp.eye(B_GROUPS, dtype=F32)
    bdc = jnp.kron(eye, cc).astype(BF16)
    bds = jnp.kron(eye, -cs).astype(BF16)
    a_c, a_s = _dft_tables(s)
    a_c = a_c.astype(BF16)
    a_s = a_s.astype(BF16)

    ones = jnp.ones((lc, ROPE_W), F32)
    _, kc, vc, _, _ = _even_in(ctx, csh1, csc1, g[0][None, :], win, gq, wuq, gkv, wukv, ones, jnp.zeros_like(ones),
                               bdc, bds)
    q, ko, vo, uc, us = _even_in(x, sh1, sc1, g[0][None, :], win, gq, wuq, gkv, wukv, cos, sin, bdc, bds)
    o_a = _attention(q, kc, ko, vc, vo)
    o_b = _fourier(a_c, a_s, uc, us)
    wrh, wrl = _router_parts(moe_w_router[0])
    x, hx, lg = _even_out(o_a, o_b, a_w_out[0].astype(BF16), x, gt1, sh2, sc2, g[1][None, :], g[2][None, :], wrh, wrl)
    x = moe(0, hx, lg, x, gt2, g[3][None, :])

    sh1, sc1, gt1, sh2, sc2, gt2 = mod_rows(1, slice(0, b))
    g = norm_g[1]
    oc, s_, gb = _odd_in(x, sh1, sc1, g[0][None, :], o_w_in[0].astype(BF16), o_g_v[0][None, :],
                         o_w_s[0].astype(BF16), o_b_s[0].T)
    wrh, wrl = _router_parts(moe_w_router[1])
    x, hx, lg = _odd_out(oc, s_, gb, o_conv_w[0], o_w_out[0].astype(BF16), x, gt1, sh2, sc2, g[1][None, :],
                         g[2][None, :], wrh, wrl)
    x = moe(1, hx, lg, x, gt2, g[3][None, :])
    return x
```

```python
import functools

import jax
import jax.numpy as jnp
from jax import lax
from jax.experimental import pallas as pl
from jax.experimental.pallas import tpu as pltpu

F32 = jnp.float32
BF16 = jnp.bfloat16
U32 = jnp.uint32

D_MODEL = 2048
GRID_W = 64
EPS = 1e-6
A_HEADS = 12
A_QK_NOPE = 128
A_QK_ROPE = 64
A_V_DIM = 128
A_Q_RANK = 768
A_KV_RANK = 512
ROPE_BASE = 10000.0
B_GROUPS = 4
B_GROUP_DIM = 128
B_WIDTH = B_GROUPS * B_GROUP_DIM
A_WIDTH = A_HEADS * A_V_DIM
C_HEADS = 8
C_HEAD_DIM = 128
CHUNK = 128
C_WIDTH = C_HEADS * C_HEAD_DIM
D_WIDTH = 1024
N_EXPERTS = 64
TOP_K = 6
N_GROUPS = 8
GROUP_SIZE = N_EXPERTS // N_GROUPS
TOPK_GROUPS = 4
EXPERT_FF = 704
ROUTED_SCALE = 2.5

LANES = 128
VMEM_BYTES_V7X = 64 * 1024 * 1024
FF_PAD = 768
HALF = D_MODEL // 2
TILE_ROWS = 8
QK_PAD = 256
V_PAD = 256
LOG2_E = 1.4426950408889634
COND_ROWS = 32
ROW_BLOCK = 256
EXPERT_BLOCK = 256
ROUTE_BLOCK = 512


def _cparams(semantics, vmem_mib):
    assert vmem_mib * 1024 * 1024 < VMEM_BYTES_V7X
    return pltpu.CompilerParams(dimension_semantics=semantics, vmem_limit_bytes=vmem_mib * 1024 * 1024)


def _resident(shape):
    nd = len(shape)
    return pl.BlockSpec(shape, lambda *_: (0,) * nd, pipeline_mode=pl.Buffered(1))


def _dot(a, b):
    return jnp.dot(a, b, preferred_element_type=F32)


def _dot_nt(a, b):
    return lax.dot_general(a, b, (((1,), (1,)), ((), ())), preferred_element_type=F32)


def _rms(xf, g):
    return xf * lax.rsqrt(jnp.mean(xf * xf, axis=-1, keepdims=True) + EPS) * g


def _split_bf16(a):
    hi = a.astype(BF16)
    lo = (a - hi.astype(F32)).astype(BF16)
    return hi, lo


def _pack_halves(lo_f32, hi_f32):
    lo = lax.bitcast_convert_type(lo_f32.astype(BF16).astype(F32), U32) >> 16
    hi = lax.bitcast_convert_type(hi_f32.astype(BF16).astype(F32), U32) & jnp.uint32(0xFFFF0000)
    return lo | hi


def _unpack_halves(w):
    lo = lax.bitcast_convert_type(w << 16, F32)
    hi = lax.bitcast_convert_type(w & jnp.uint32(0xFFFF0000), F32)
    return lo, hi


def _store_token_tiles(ref, packed):
    m = packed.shape[0]
    for s in range(TILE_ROWS):
        ref[pl.ds(s, m, stride=TILE_ROWS), :] = packed[:, s * LANES:(s + 1) * LANES]


def _load_token_tiles(ref, m):
    return [ref[pl.ds(s, m, stride=TILE_ROWS), :] for s in range(TILE_ROWS)]


def _unpack_token_tiles(ref, m):
    los, his = [], []
    for w in _load_token_tiles(ref, m):
        lo, hi = _unpack_halves(w)
        los.append(lo.astype(BF16))
        his.append(hi.astype(BF16))
    return jnp.concatenate(los, axis=1), jnp.concatenate(his, axis=1)


def _silu(x):
    return x * jax.nn.sigmoid(x)


def _adaln_kernel(c_ref, w_ref, b_ref, o_ref):
    a_hi, a_lo = _split_bf16(_silu(c_ref[...]))
    w_hi, w_lo = _split_bf16(w_ref[0])
    o_ref[0] = _dot(a_hi, w_hi) + _dot(a_lo, w_hi) + _dot(a_hi, w_lo) + b_ref[0]


def _adaln(cond, mod_w, mod_b):
    depth, d, n = mod_w.shape
    tn = 512
    return pl.pallas_call(
        _adaln_kernel,
        grid=(depth, n // tn),
        in_specs=[
            pl.BlockSpec((COND_ROWS, d), lambda l, j: (0, 0)),
            pl.BlockSpec((1, d, tn), lambda l, j: (l, 0, j)),
            pl.BlockSpec((1, 1, tn), lambda l, j: (l, 0, j)),
        ],
        out_specs=pl.BlockSpec((1, COND_ROWS, tn), lambda l, j: (l, 0, j)),
        out_shape=jax.ShapeDtypeStruct((depth, COND_ROWS, n), F32),
        compiler_params=_cparams(("parallel", "parallel"), 32),
        name="adaln",
    )(cond, mod_w, mod_b.reshape(depth, 1, n))


NOPE_W = A_HEADS * A_QK_NOPE
ROPE_W = A_HEADS * A_QK_ROPE
EVEN_IN_PAD = A_Q_RANK + A_KV_RANK + B_WIDTH + 2 * A_QK_ROPE


def _even_in_kernel(x_ref, sh_ref, sc_ref, g_ref, win_ref, gq_ref, wuq_ref, gkv_ref, wukv_ref,
                    cos_ref, sin_ref, bdc_ref, bds_ref, q_ref, k_ref, v_ref, uc_ref, us_ref):
    tm = x_ref.shape[1]
    h = _rms(x_ref[0], g_ref[...]) * (1.0 + sc_ref[0]) + sh_ref[0]
    z = _dot(h.astype(BF16), win_ref[...])
    c_q = z[:, :A_Q_RANK]
    c_kv = z[:, A_Q_RANK:A_Q_RANK + A_KV_RANK]
    o = A_Q_RANK + A_KV_RANK
    u_f = z[:, o:o + B_WIDTH].astype(BF16)
    k_pe = z[:, o + B_WIDTH:o + B_WIDTH + A_QK_ROPE]
    k_pe_sw = z[:, o + B_WIDTH + A_QK_ROPE:]
    cos = cos_ref[...]
    sin = sin_ref[...]
    scale = (A_QK_NOPE + A_QK_ROPE) ** -0.5 * LOG2_E

    q = _dot(_rms(c_q, gq_ref[...]).astype(BF16), wuq_ref[...]) * scale
    q_pe = q[:, NOPE_W:NOPE_W + ROPE_W] * cos + q[:, NOPE_W + ROPE_W:] * sin
    kv = _dot(_rms(c_kv, gkv_ref[...]).astype(BF16), wukv_ref[...])
    k_pe_r = (k_pe * cos[:, :A_QK_ROPE] + k_pe_sw * sin[:, :A_QK_ROPE]).astype(BF16)
    zeros = jnp.zeros((tm, QK_PAD - A_QK_NOPE - A_QK_ROPE), BF16)
    ones_col = (lax.broadcasted_iota(jnp.int32, (tm, V_PAD - A_V_DIM), 1) == 0).astype(BF16)
    for hd in range(A_HEADS):
        n0 = hd * A_QK_NOPE
        r0 = hd * A_QK_ROPE
        q_ref[0, hd, :, :A_QK_NOPE] = q[:, n0:n0 + A_QK_NOPE].astype(BF16)
        q_ref[0, hd, :, A_QK_NOPE:A_QK_NOPE + A_QK_ROPE] = q_pe[:, r0:r0 + A_QK_ROPE].astype(BF16)
        q_ref[0, hd, :, A_QK_NOPE + A_QK_ROPE:] = zeros
        k_ref[0, hd, :, :A_QK_NOPE] = kv[:, n0:n0 + A_QK_NOPE].astype(BF16)
        k_ref[0, hd, :, A_QK_NOPE:A_QK_NOPE + A_QK_ROPE] = k_pe_r
        k_ref[0, hd, :, A_QK_NOPE + A_QK_ROPE:] = zeros
        v_ref[0, hd, :, :A_V_DIM] = kv[:, NOPE_W + n0:NOPE_W + n0 + A_V_DIM].astype(BF16)
        v_ref[0, hd, :, A_V_DIM:] = ones_col
    uc_ref[0] = _dot(u_f, bdc_ref[...]).astype(BF16)
    us_ref[0] = _dot(u_f, bds_ref[...]).astype(BF16)


def _even_in(x, sh, sc, g, win, gq, wuq, gkv, wukv, cos, sin, bdc, bds):
    b, l, d = x.shape
    tm = ROW_BLOCK
    row = lambda bi, i: (bi, i, 0)
    per_b = lambda bi, i: (bi, 0, 0)
    heads = lambda bi, i: (bi, 0, i, 0)
    return pl.pallas_call(
        _even_in_kernel,
        grid=(b, l // tm),
        in_specs=[
            pl.BlockSpec((1, tm, d), row),
            pl.BlockSpec((1, 1, d), per_b),
            pl.BlockSpec((1, 1, d), per_b),
            _resident(g.shape), _resident(win.shape), _resident(gq.shape), _resident(wuq.shape),
            _resident(gkv.shape), _resident(wukv.shape),
            pl.BlockSpec((tm, ROPE_W), lambda bi, i: (i, 0)),
            pl.BlockSpec((tm, ROPE_W), lambda bi, i: (i, 0)),
            _resident(bdc.shape), _resident(bds.shape),
        ],
        out_specs=[
            pl.BlockSpec((1, A_HEADS, tm, QK_PAD), heads),
            pl.BlockSpec((1, A_HEADS, tm, QK_PAD), heads),
            pl.BlockSpec((1, A_HEADS, tm, V_PAD), heads),
            pl.BlockSpec((1, tm, B_WIDTH), row),
            pl.BlockSpec((1, tm, B_WIDTH), row),
        ],
        out_shape=[
            jax.ShapeDtypeStruct((b, A_HEADS, l, QK_PAD), BF16),
            jax.ShapeDtypeStruct((b, A_HEADS, l, QK_PAD), BF16),
            jax.ShapeDtypeStruct((b, A_HEADS, l, V_PAD), BF16),
            jax.ShapeDtypeStruct((b, l, B_WIDTH), BF16),
            jax.ShapeDtypeStruct((b, l, B_WIDTH), BF16),
        ],
        compiler_params=_cparams(("parallel", "parallel"), 56),
        name="even_in",
    )(x, sh, sc, g, win, gq, wuq, gkv, wukv, cos, sin, bdc, bds)


ATTN_Q_BLOCK = 512


def _attn_kernel(q_ref, kc_ref, ko_ref, vc_ref, vo_ref, o_ref):
    kc = kc_ref[0, 0]
    ko = ko_ref[0, 0]
    vc = vc_ref[0, 0]
    vo = vo_ref[0, 0]
    tq = ATTN_Q_BLOCK
    for i in range(q_ref.shape[2] // tq):
        rows = slice(i * tq, (i + 1) * tq)
        q = q_ref[0, 0, rows, :]
        s_c = _dot_nt(q, kc)
        s_o = _dot_nt(q, ko)
        m = jnp.maximum(jnp.max(s_c, axis=-1, keepdims=True), jnp.max(s_o, axis=-1, keepdims=True))
        p_c = jnp.exp2(s_c - m).astype(BF16)
        p_o = jnp.exp2(s_o - m).astype(BF16)
        o = _dot(p_c, vc) + _dot(p_o, vo)
        o_ref[0, rows, :] = (o[:, :A_V_DIM] / o[:, A_V_DIM:A_V_DIM + 1]).astype(BF16)


def _attention(q, kc, ko, vc, vo):
    b, h, s, _ = q.shape
    lc = kc.shape[2]
    bh = lambda bi, hi: (bi, hi, 0, 0)
    return pl.pallas_call(
        _attn_kernel,
        grid=(b, h),
        in_specs=[
            pl.BlockSpec((1, 1, s, QK_PAD), bh),
            pl.BlockSpec((1, 1, lc, QK_PAD), bh),
            pl.BlockSpec((1, 1, s, QK_PAD), bh),
            pl.BlockSpec((1, 1, lc, V_PAD), bh),
            pl.BlockSpec((1, 1, s, V_PAD), bh),
        ],
        out_specs=pl.BlockSpec((1, s, A_V_DIM), lambda bi, hi: (bi, 0, hi)),
        out_shape=jax.ShapeDtypeStruct((b, s, h * A_V_DIM), BF16),
        compiler_params=_cparams(("parallel", "parallel"), 48),
        name="attention",
    )(q, kc, ko, vc, vo)


def _fourier_kernel(ac_ref, as_ref, uc_ref, us_ref, o_ref):
    o_ref[0] = (_dot(ac_ref[...], uc_ref[0]) + _dot(as_ref[...], us_ref[0])).astype(BF16)


def _fourier(a_c, a_s, uc, us):
    b, s, w = uc.shape
    tm = 512
    return pl.pallas_call(
        _fourier_kernel,
        grid=(b, s // tm),
        in_specs=[
            pl.BlockSpec((tm, s), lambda bi, i: (i, 0)),
            pl.BlockSpec((tm, s), lambda bi, i: (i, 0)),
            pl.BlockSpec((1, s, w), lambda bi, i: (bi, 0, 0)),
            pl.BlockSpec((1, s, w), lambda bi, i: (bi, 0, 0)),
        ],
        out_specs=pl.BlockSpec((1, tm, w), lambda bi, i: (bi, i, 0)),
        out_shape=jax.ShapeDtypeStruct((b, s, w), BF16),
        compiler_params=_cparams(("parallel", "parallel"), 32),
        name="fourier",
    )(a_c, a_s, uc, us)


def _mix_epilogue(mix, x_ref, gt_ref, sh_ref, sc_ref, gpost_ref, gffn_ref, wrh_ref, wrl_ref,
                  xo_ref, hx_ref, lg_ref):
    xn = x_ref[0] + gt_ref[0] * _rms(mix, gpost_ref[...])
    xo_ref[0] = xn
    t = _rms(xn, gffn_ref[...]) * (1.0 + sc_ref[0]) + sh_ref[0]
    t_hi, t_lo = _split_bf16(t)
    wrh = wrh_ref[...]
    lg = _dot(t_hi, wrh) + _dot(t_lo, wrh) + _dot(t_hi, wrl_ref[...])
    lg_ref[...] = lg.T[:N_EXPERTS, :]
    tq = t_hi.astype(F32)
    _store_token_tiles(hx_ref, _pack_halves(tq[:, :HALF], tq[:, HALF:]))


def _even_out_kernel(oa_ref, ob_ref, w_ref, *rest):
    mix = _dot(oa_ref[0], w_ref[:A_WIDTH, :]) + _dot(ob_ref[0], w_ref[A_WIDTH:, :])
    _mix_epilogue(mix, *rest)


def _odd_out_kernel(oc_ref, s_ref, sp_ref, sn_ref, gb_ref, cw_ref, w_ref, *rest):
    i = pl.program_id(1)
    tm = s_ref.shape[1]
    s = s_ref[0].astype(F32)
    prev_row = jnp.where(i > 0, sp_ref[0, 15:16, :].astype(F32), 0.0)
    next_row = jnp.where(i < pl.num_programs(1) - 1, sn_ref[0, 0:1, :].astype(F32), 0.0)
    row = lax.broadcasted_iota(jnp.int32, (tm, 1), 0)
    s_dn = jnp.where(row == 0, prev_row, pltpu.roll(s, 1, axis=0))
    s_up = jnp.where(row == tm - 1, next_row, pltpu.roll(s, tm - 1, axis=0))
    cw = cw_ref[...]
    conv = cw[0:1, :] * s_dn + cw[1:2, :] * s + cw[2:3, :] * s_up
    od = (gb_ref[0].astype(F32) * conv).astype(BF16)
    mix = _dot(oc_ref[0], w_ref[:C_WIDTH, :]) + _dot(od, w_ref[C_WIDTH:, :])
    _mix_epilogue(mix, *rest)


def _epilogue_specs(b, s, d, tm):
    per_b = lambda bi, i: (bi, 0, 0)
    row = lambda bi, i: (bi, i, 0)
    nblk = s // tm
    in_specs = [
        pl.BlockSpec((1, tm, d), row),
        pl.BlockSpec((1, 1, d), per_b),
        pl.BlockSpec((1, 1, d), per_b),
        pl.BlockSpec((1, 1, d), per_b),
        _resident((1, d)), _resident((1, d)),
        _resident((d, LANES)), _resident((d, LANES)),
    ]
    out_specs = [
        pl.BlockSpec((1, tm, d), row),
        pl.BlockSpec((tm * TILE_ROWS, LANES), lambda bi, i: (bi * nblk + i, 0)),
        pl.BlockSpec((N_EXPERTS, tm), lambda bi, i: (0, bi * nblk + i)),
    ]
    out_shape = [
        jax.ShapeDtypeStruct((b, s, d), F32),
        jax.ShapeDtypeStruct((b * s * TILE_ROWS, LANES), U32),
        jax.ShapeDtypeStruct((N_EXPERTS, b * s), F32),
    ]
    return in_specs, out_specs, out_shape


def _even_out(oa, ob, w_out, x, gt, sh, sc, gpost, gffn, wrh, wrl):
    b, s, d = x.shape
    tm = ROW_BLOCK
    row = lambda bi, i: (bi, i, 0)
    ep_in, out_specs, out_shape = _epilogue_specs(b, s, d, tm)
    return pl.pallas_call(
        _even_out_kernel,
        grid=(b, s // tm),
        in_specs=[pl.BlockSpec((1, tm, A_WIDTH), row), pl.BlockSpec((1, tm, B_WIDTH), row),
                  _resident(w_out.shape)] + ep_in,
        out_specs=out_specs, out_shape=out_shape,
        compiler_params=_cparams(("parallel", "parallel"), 48),
        name="even_out",
    )(oa, ob, w_out, x, gt, sh, sc, gpost, gffn, wrh, wrl)


def _odd_out(oc, s_, gb, conv_w, w_out, x, gt, sh, sc, gpost, gffn, wrh, wrl):
    b, s, d = x.shape
    tm = ROW_BLOCK
    halo = 16
    nh = s // halo
    row = lambda bi, i: (bi, i, 0)
    ep_in, out_specs, out_shape = _epilogue_specs(b, s, d, tm)
    return pl.pallas_call(
        _odd_out_kernel,
        grid=(b, s // tm),
        in_specs=[
            pl.BlockSpec((1, tm, C_WIDTH), row),
            pl.BlockSpec((1, tm, D_WIDTH), row),
            pl.BlockSpec((1, halo, D_WIDTH), lambda bi, i: (bi, jnp.maximum(i * (tm // halo) - 1, 0), 0)),
            pl.BlockSpec((1, halo, D_WIDTH), lambda bi, i: (bi, jnp.minimum((i + 1) * (tm // halo), nh - 1), 0)),
            pl.BlockSpec((1, tm, D_WIDTH), row),
            _resident(conv_w.shape),
            _resident(w_out.shape),
        ] + ep_in,
        out_specs=out_specs, out_shape=out_shape,
        compiler_params=_cparams(("parallel", "parallel"), 48),
        name="odd_out",
    )(oc, s_, s_, s_, gb, conv_w, w_out, x, gt, sh, sc, gpost, gffn, wrh, wrl)


def _odd_in_kernel(x_ref, sh_ref, sc_ref, g_ref, win_ref, gv_ref, ws_ref, bs_ref, oc_ref, s_ref, gb_ref):
    tm = x_ref.shape[1]
    hb = (_rms(x_ref[0], g_ref[...]) * (1.0 + sc_ref[0]) + sh_ref[0]).astype(BF16)
    u = jax.nn.gelu(_dot(hb, win_ref[:, :C_WIDTH]))
    v = jax.nn.gelu(_dot(hb, win_ref[:, C_WIDTH:2 * C_WIDTH]))
    gv = gv_ref[...]
    bs = bs_ref[...]
    for hd in range(C_HEADS):
        c0 = hd * C_HEAD_DIM
        vh = _rms(v[:, c0:c0 + C_HEAD_DIM], gv[:, c0:c0 + C_HEAD_DIM]).astype(BF16)
        w = ws_ref[hd]
        bias = bs[:, hd:hd + 1]
        for n in range(tm // CHUNK):
            p0 = n * CHUNK
            sv = _dot(w, vh[p0:p0 + CHUNK, :]) + bias
            oc_ref[0, p0:p0 + CHUNK, c0:c0 + C_HEAD_DIM] = (u[p0:p0 + CHUNK, c0:c0 + C_HEAD_DIM] * sv).astype(BF16)
    o = 2 * C_WIDTH
    hd_ = _dot(hb, win_ref[:, o:o + D_WIDTH])
    gb_ref[0] = _dot(hb, win_ref[:, o + D_WIDTH:o + 2 * D_WIDTH]).astype(BF16)
    gc = _dot(hb, win_ref[:, o + 2 * D_WIDTH:])
    s_ref[0] = (gc * hd_).astype(BF16)


def _odd_in(x, sh, sc, g, win, gv, ws, bs_t):
    b, s, d = x.shape
    tm = ROW_BLOCK
    row = lambda bi, i: (bi, i, 0)
    per_b = lambda bi, i: (bi, 0, 0)
    out = jax.ShapeDtypeStruct((b, s, C_WIDTH), BF16)
    return pl.pallas_call(
        _odd_in_kernel,
        grid=(b, s // tm),
        in_specs=[
            pl.BlockSpec((1, tm, d), row),
            pl.BlockSpec((1, 1, d), per_b),
            pl.BlockSpec((1, 1, d), per_b),
            _resident(g.shape), _resident(win.shape), _resident(gv.shape), _resident(ws.shape),
            _resident(bs_t.shape),
        ],
        out_specs=[pl.BlockSpec((1, tm, C_WIDTH), row)] * 3,
        out_shape=[out, out, out],
        compiler_params=_cparams(("parallel", "parallel"), 56),
        name="odd_in",
    )(x, sh, sc, g, win, gv, ws, bs_t)


def _first_index_of_max(vals, idx, sentinel):
    m = jnp.max(vals, axis=0, keepdims=True)
    first = jnp.min(jnp.where(vals == m, idx, sentinel), axis=0, keepdims=True)
    return m, first


def _route_kernel(lg_ref, b_ref, e_ref, w_ref, c_ref):
    tn = lg_ref.shape[1]
    scores = jax.nn.sigmoid(lg_ref[...])
    sel = scores + b_ref[...]
    row8 = lax.broadcasted_iota(jnp.int32, (8, tn), 0).astype(F32)
    neg = -jnp.inf
    gs = jnp.zeros((N_GROUPS, tn), F32)
    for g in range(N_GROUPS):
        blk = sel[g * GROUP_SIZE:(g + 1) * GROUP_SIZE, :]
        m1, i1 = _first_index_of_max(blk, row8, float(GROUP_SIZE))
        m2 = jnp.max(jnp.where(row8 == i1, neg, blk), axis=0, keepdims=True)
        gs = jnp.where(row8 == float(g), m1 + m2, gs)
    keep = jnp.zeros((N_GROUPS, tn), F32)
    cur = gs
    for _ in range(TOPK_GROUPS):
        _, first = _first_index_of_max(cur, row8, float(N_GROUPS))
        pick = row8 == first
        keep = jnp.where(pick, 1.0, keep)
        cur = jnp.where(pick, neg, cur)
    masked = jnp.concatenate(
        [jnp.where(jnp.max(jnp.where(row8 == float(g), keep, 0.0), axis=0, keepdims=True) > 0.5,
                   sel[g * GROUP_SIZE:(g + 1) * GROUP_SIZE, :], neg) for g in range(N_GROUPS)],
        axis=0)
    eidx = lax.broadcasted_iota(jnp.int32, (N_EXPERTS, tn), 0).astype(F32)
    e_out = jnp.zeros((8, tn), F32)
    w_out = jnp.zeros((8, tn), F32)
    total = jnp.zeros((1, tn), F32)
    picked = jnp.zeros((N_EXPERTS, tn), F32)
    for k in range(TOP_K):
        _, first = _first_index_of_max(masked, eidx, float(N_EXPERTS))
        pick = eidx == first
        wk = jnp.sum(jnp.where(pick, scores, 0.0), axis=0, keepdims=True)
        masked = jnp.where(pick, neg, masked)
        picked = jnp.where(pick, 1.0, picked)
        e_out = jnp.where(row8 == float(k), first, e_out)
        w_out = jnp.where(row8 == float(k), wk, w_out)
        total = total + wk
    e_ref[...] = e_out.astype(jnp.int32)
    w_ref[...] = w_out / total * ROUTED_SCALE
    c_ref[...] = jnp.broadcast_to(jnp.sum(picked, axis=1, keepdims=True), c_ref.shape).astype(jnp.int32)


def _route(logits_t, bias):
    e, t = logits_t.shape
    tn = ROUTE_BLOCK
    nblk = t // tn
    top_e, wts, cnt = pl.pallas_call(
        _route_kernel,
        grid=(nblk,),
        in_specs=[pl.BlockSpec((e, tn), lambda i: (0, i)), pl.BlockSpec((e, 1), lambda i: (0, 0))],
        out_specs=[pl.BlockSpec((8, tn), lambda i: (0, i))] * 2 + [pl.BlockSpec((e, LANES), lambda i: (0, i))],
        out_shape=[jax.ShapeDtypeStruct((8, t), jnp.int32), jax.ShapeDtypeStruct((8, t), F32),
                   jax.ShapeDtypeStruct((e, nblk * LANES), jnp.int32)],
        compiler_params=_cparams(("parallel",), 32),
        name="route",
    )(logits_t, bias.reshape(e, 1))
    counts = jnp.sum(cnt.reshape(e, nblk, LANES)[:, :, 0], axis=1)
    return top_e, wts, counts


def _dispatch_tables(top_e, counts, n_tok):
    bm = EXPERT_BLOCK
    n_asg = n_tok * TOP_K
    n_blk = n_asg // bm
    n_steps = n_blk + N_EXPERTS
    i32 = jnp.int32
    flat_e = top_e[:TOP_K].T.reshape(n_asg)
    order = jnp.argsort(flat_e, stable=True).astype(i32)
    tok = order // TOP_K
    k = order - tok * TOP_K
    src = jnp.concatenate([tok * TILE_ROWS, jnp.zeros((bm,), i32)]).reshape(n_blk + 1, 1, bm)
    spare = (n_asg + jnp.arange(bm, dtype=i32)) * TILE_ROWS
    dst = jnp.concatenate([spare, (k * n_tok + tok) * TILE_ROWS]).reshape(n_blk + 1, 1, bm)

    experts = jnp.arange(N_EXPERTS, dtype=i32)
    ends = jnp.cumsum(counts).astype(i32)
    starts = ends - counts
    first_blk = starts // bm
    n_e_blk = jnp.where(counts > 0, (ends - 1) // bm - first_blk + 1, 0)
    item_end = jnp.cumsum(n_e_blk).astype(i32)
    item_start = item_end - n_e_blk
    n_items = item_end[-1]
    e_last = jnp.max(jnp.where(counts > 0, experts, 0))
    w = jnp.arange(n_steps, dtype=i32)
    active = w < n_items
    e_w = jnp.where(active, jnp.sum((item_end[None, :] <= w[:, None]).astype(i32), axis=1), e_last)
    blk_w = jnp.where(active, first_blk[e_w] + w - item_start[e_w], n_blk)
    lo_w = jnp.clip(starts[e_w] - blk_w * bm, 0, bm)
    hi_w = jnp.clip(ends[e_w] - blk_w * bm, 0, bm)
    neg1 = jnp.full((1,), -1, i32)
    first_of_blk = (blk_w != jnp.concatenate([neg1, blk_w[:-1]])).astype(i32)
    first_of_e = ((e_w != jnp.concatenate([neg1, e_w[:-1]])) & active).astype(i32)
    cand = jnp.where(counts > 0, experts, N_EXPERTS)
    suffix_min = lax.cummin(cand[::-1])[::-1]
    nxt = jnp.concatenate([suffix_min[1:], jnp.full((1,), N_EXPERTS, i32)])
    nxt = jnp.where(nxt >= N_EXPERTS, -1, nxt)
    return src, dst, (blk_w, e_w, lo_w, hi_w, first_of_blk, first_of_e, nxt[e_w], n_items.reshape(1))


FF_CHUNK = 256
DOWN_CHUNK = 512


def _swiglu_tiles(x_ref, m, wg_ref, wu_ref, wd_ref, between=lambda: None):
    x_lo, x_hi = _unpack_token_tiles(x_ref, m)
    hs = []
    for c in range(FF_PAD // FF_CHUNK):
        cols = slice(c * FF_CHUNK, (c + 1) * FF_CHUNK)
        g = _dot(x_lo, wg_ref[:HALF, cols]) + _dot(x_hi, wg_ref[HALF:, cols])
        between()
        u = _dot(x_lo, wu_ref[:HALF, cols]) + _dot(x_hi, wu_ref[HALF:, cols])
        between()
        hs.append((_silu(g) * u).astype(BF16))
    h = jnp.concatenate(hs, axis=1)
    ys = []
    for n in range(D_MODEL // DOWN_CHUNK):
        ys.append(_dot(h, wd_ref[:, n * DOWN_CHUNK:(n + 1) * DOWN_CHUNK]))
        between()
    return ys


def _packed_chunks(ys):
    per = DOWN_CHUNK // LANES
    half_chunks = HALF // DOWN_CHUNK
    out = []
    for s in range(TILE_ROWS):
        c0 = (s % per) * LANES
        lo = ys[s // per][:, c0:c0 + LANES]
        hi = ys[half_chunks + s // per][:, c0:c0 + LANES]
        out.append(_pack_halves(lo, hi))
    return out


N_DMA_SLOTS = (FF_PAD // FF_CHUNK) * 2 + D_MODEL // DOWN_CHUNK


def _expert_kernel(blk_ref, exp_ref, lo_ref, hi_ref, fb_ref, fe_ref, nxt_ref, ni_ref,
                   src0_ref, src1_ref, dst_ref, hx_hbm, wg_hbm, wu_hbm, wd_hbm,
                   ys_hbm,
                   xbuf0, xbuf1, ybuf0, ybuf1, sg, su, sd, wg, wu, wd, gsem, ssem, wsem, *, layer, n_blk):
    bm = EXPERT_BLOCK
    w = pl.program_id(0)
    n_items = ni_ref[0]
    blk = blk_ref[w]
    expert = exp_ref[w]
    active = w < n_items
    xbufs = (xbuf0, xbuf1)
    ybufs = (ybuf0, ybuf1)
    tile = lambda r: pl.ds(r * TILE_ROWS, TILE_ROWS)

    def weight_copies(e):
        return (pltpu.make_async_copy(wg_hbm.at[layer, e], sg.at[pl.ds(0, EXPERT_FF), :], wsem.at[0]),
                pltpu.make_async_copy(wu_hbm.at[layer, e], su.at[pl.ds(0, EXPERT_FF), :], wsem.at[1]),
                pltpu.make_async_copy(wd_hbm.at[layer, e], sd, wsem.at[2]))

    def gather_starts(src_ref, p):
        def one(r):
            row = pl.multiple_of(src_ref[0, 0, r], TILE_ROWS)
            pltpu.make_async_copy(hx_hbm.at[pl.ds(row, TILE_ROWS), :], xbufs[p].at[tile(r), :], gsem.at[p]).start()
        return [functools.partial(one, r) for r in range(bm)]

    def scatter_starts(p):
        def one(r):
            row = pl.multiple_of(dst_ref[0, 0, r], TILE_ROWS)
            pltpu.make_async_copy(ybufs[p].at[tile(r), :], ys_hbm.at[pl.ds(row, TILE_ROWS), :], ssem.at[0]).start()
        return [functools.partial(one, r) for r in range(bm)]

    def wait_gather(p):
        pltpu.make_async_copy(hx_hbm.at[pl.ds(0, bm * TILE_ROWS), :], xbufs[p], gsem.at[p]).wait()

    def wait_scatter(p):
        pltpu.make_async_copy(ybufs[p], ys_hbm.at[pl.ds(0, bm * TILE_ROWS), :], ssem.at[0]).wait()

    @pl.when(w == 0)
    def _():
        sg[EXPERT_FF:, :] = jnp.zeros((FF_PAD - EXPERT_FF, D_MODEL), F32)
        su[EXPERT_FF:, :] = jnp.zeros((FF_PAD - EXPERT_FF, D_MODEL), F32)
        wd[EXPERT_FF:, :] = jnp.zeros((FF_PAD - EXPERT_FF, D_MODEL), BF16)
        ybuf1[...] = jnp.zeros(ybuf1.shape, U32)
        for c in weight_copies(expert):
            c.start()
        for start in gather_starts(src0_ref, 0):
            start()

    @pl.when(jnp.logical_and(active, fe_ref[w] == 1))
    def _():
        for c in weight_copies(expert):
            c.wait()
        for c in range(FF_PAD // LANES):
            cols = slice(c * LANES, (c + 1) * LANES)
            wg[:, cols] = sg[cols, :].T.astype(BF16)
            wu[:, cols] = su[cols, :].T.astype(BF16)
        rows_d = 64

        def cast_down(i, carry):
            r = pl.ds(pl.multiple_of(i * rows_d, rows_d), rows_d)
            wd[r, :] = sd[r, :].astype(BF16)
            return carry

        lax.fori_loop(0, EXPERT_FF // rows_d, cast_down, 0)

        @pl.when(nxt_ref[w] >= 0)
        def _():
            for c in weight_copies(nxt_ref[w]):
                c.start()

    def first_item_of_block(p):
        @pl.when(blk > 0)
        def _():
            wait_scatter(p)
        wait_gather(p)
        pending = gather_starts(src1_ref, 1 - p) + scatter_starts(1 - p)
        per_slot = -(-len(pending) // N_DMA_SLOTS)

        def between():
            for start in pending[:per_slot]:
                start()
            del pending[:per_slot]

        ys = _swiglu_tiles(xbufs[p], bm, wg, wu, wd, between)
        assert not pending
        for s, chunk in enumerate(_packed_chunks(ys)):
            ybufs[p][pl.ds(s, bm, stride=TILE_ROWS), :] = chunk

    def later_item_of_block(p):
        ys = _swiglu_tiles(xbufs[p], bm, wg, wu, wd)
        row = lax.broadcasted_iota(jnp.int32, (bm, 1), 0)
        mine = jnp.logical_and(row >= lo_ref[w], row < hi_ref[w])
        for s, chunk in enumerate(_packed_chunks(ys)):
            idx = pl.ds(s, bm, stride=TILE_ROWS)
            ybufs[p][idx, :] = jnp.where(mine, chunk, ybufs[p][idx, :])

    parity = lax.rem(blk, 2)
    first = fb_ref[w] == 1
    for p in range(2):
        pl.when(jnp.logical_and(active, jnp.logical_and(first, parity == p)))(
            functools.partial(first_item_of_block, p))
        pl.when(jnp.logical_and(active, jnp.logical_and(jnp.logical_not(first), parity == p)))(
            functools.partial(later_item_of_block, p))

    @pl.when(w == n_items)
    def _():
        last = (n_blk - 1) % 2
        wait_scatter(1 - last)
        wait_gather(1 - last)
        for start in scatter_starts(last):
            start()
        wait_scatter(last)


def _experts(hx, src, dst, items, w_gate, w_up, w_down, layer):
    bm = EXPERT_BLOCK
    n_blk = src.shape[0] - 1
    n_steps = items[0].shape[0]
    n_out_rows = (n_blk + 1) * bm * TILE_ROWS
    smem = functools.partial(pl.BlockSpec, (1, 1, bm), memory_space=pltpu.SMEM)
    grid_spec = pltpu.PrefetchScalarGridSpec(
        num_scalar_prefetch=len(items),
        grid=(n_steps,),
        in_specs=[
            smem(index_map=lambda w, *_: (0, 0, 0)),
            smem(index_map=lambda w, blk, *_: (jnp.minimum(blk[w] + 1, n_blk), 0, 0)),
            smem(index_map=lambda w, blk, *_: (blk[w], 0, 0)),
            pl.BlockSpec(memory_space=pl.ANY),
            pl.BlockSpec(memory_space=pl.ANY),
            pl.BlockSpec(memory_space=pl.ANY),
            pl.BlockSpec(memory_space=pl.ANY),
        ],
        out_specs=pl.BlockSpec(memory_space=pl.ANY),
        scratch_shapes=[
            pltpu.VMEM((bm * TILE_ROWS, LANES), U32), pltpu.VMEM((bm * TILE_ROWS, LANES), U32),
            pltpu.VMEM((bm * TILE_ROWS, LANES), U32), pltpu.VMEM((bm * TILE_ROWS, LANES), U32),
            pltpu.VMEM((FF_PAD, D_MODEL), F32), pltpu.VMEM((FF_PAD, D_MODEL), F32),
            pltpu.VMEM((EXPERT_FF, D_MODEL), F32),
            pltpu.VMEM((D_MODEL, FF_PAD), BF16), pltpu.VMEM((D_MODEL, FF_PAD), BF16),
            pltpu.VMEM((FF_PAD, D_MODEL), BF16),
            pltpu.SemaphoreType.DMA((2,)),
            pltpu.SemaphoreType.DMA((1,)),
            pltpu.SemaphoreType.DMA((3,)),
        ],
    )
    return pl.pallas_call(
        functools.partial(_expert_kernel, layer=layer, n_blk=n_blk),
        grid_spec=grid_spec,
        out_shape=jax.ShapeDtypeStruct((n_out_rows, LANES), U32),
        compiler_params=_cparams(("arbitrary",), 48),
        name="experts",
    )(*items, src, src, dst, hx, w_gate, w_up, w_down)


def _combine_kernel(hx_ref, y0, y1, y2, y3, y4, y5, wt_ref, wg_ref, wu_ref, wd_ref, x_ref, gt_ref, g_ref, o_ref):
    tm = x_ref.shape[1]
    shared = jnp.concatenate(_swiglu_tiles(hx_ref, tm, wg_ref, wu_ref, wd_ref), axis=1)
    wt = wt_ref[...]
    acc_lo = [None] * TILE_ROWS
    acc_hi = [None] * TILE_ROWS
    for k, y_ref in enumerate((y0, y1, y2, y3, y4, y5)):
        wk = wt[:, k:k + 1]
        for s, word in enumerate(_load_token_tiles(y_ref, tm)):
            lo, hi = _unpack_halves(word)
            acc_lo[s] = wk * lo if k == 0 else acc_lo[s] + wk * lo
            acc_hi[s] = wk * hi if k == 0 else acc_hi[s] + wk * hi
    ff = jnp.concatenate(acc_lo + acc_hi, axis=-1) + shared
    o_ref[0] = x_ref[0] + gt_ref[0] * _rms(ff, g_ref[...])


def _combine(hx, ys, wts_t, wg, wu, wd, x, gt, g):
    b, s, d = x.shape
    tm = ROW_BLOCK
    nblk = s // tm
    n_tok = b * s
    row = lambda bi, i: (bi, i, 0)
    tiles = (tm * TILE_ROWS, LANES)
    y_specs = [pl.BlockSpec(tiles, functools.partial(lambda bi, i, k: (k * (n_tok // tm) + bi * nblk + i, 0), k=k))
               for k in range(TOP_K)]
    return pl.pallas_call(
        _combine_kernel,
        grid=(b, nblk),
        in_specs=[pl.BlockSpec(tiles, lambda bi, i: (bi * nblk + i, 0))] + y_specs + [
            pl.BlockSpec((tm, 8), lambda bi, i: (bi * nblk + i, 0)),
            _resident(wg.shape), _resident(wu.shape), _resident(wd.shape),
            pl.BlockSpec((1, tm, d), row),
            pl.BlockSpec((1, 1, d), lambda bi, i: (bi, 0, 0)),
            _resident(g.shape),
        ],
        out_specs=pl.BlockSpec((1, tm, d), row),
        out_shape=jax.ShapeDtypeStruct((b, s, d), F32),
        compiler_params=_cparams(("parallel", "parallel"), 48),
        name="combine",
    )(hx, ys, ys, ys, ys, ys, ys, wts_t, wg, wu, wd, x, gt, g)


def _rope_tables(seq):
    t = jnp.arange(seq, dtype=jnp.int32)
    row = (t // GRID_W).astype(F32)
    col = (t % GRID_W).astype(F32)
    per_axis = A_QK_ROPE // 4
    inv = ROPE_BASE ** (-jnp.arange(per_axis, dtype=F32) / per_axis)
    ang = jnp.concatenate([row[:, None] * inv, col[:, None] * inv], axis=-1)
    cos = jnp.cos(ang)
    sin = jnp.sin(ang)
    cos64 = jnp.concatenate([cos, cos], axis=-1)
    sin64 = jnp.concatenate([-sin, sin], axis=-1)
    return jnp.tile(cos64, (1, A_HEADS)), jnp.tile(sin64, (1, A_HEADS))


def _dft_tables(n):
    k = jnp.arange(n, dtype=jnp.int32)
    ang = ((k[:, None] * k[None, :]) % n).astype(F32) * (2.0 * jnp.pi / n)
    scale = n ** -0.5
    return jnp.cos(ang) * scale, -jnp.sin(ang) * scale


def _swap_halves(w):
    half = w.shape[-1] // 2
    return jnp.concatenate([w[..., half:], w[..., :half]], axis=-1)


def _pad_ff(w, axis):
    pad = [(0, 0)] * w.ndim
    pad[axis] = (0, FF_PAD - EXPERT_FF)
    return jnp.pad(w.astype(BF16), pad)


def _router_parts(w_router):
    w = jnp.pad(w_router, ((0, 0), (0, LANES - N_EXPERTS)))
    hi = w.astype(BF16)
    lo = (w - hi.astype(F32)).astype(BF16)
    return hi, lo


def _moe(hx, logits_t, layer, b_router, w_gate, w_up, w_down, sg, su, sd, x, gt, g_post):
    b, s, d = x.shape
    n_tok = b * s
    top_e, wts, counts = _route(logits_t, b_router)
    src, dst, items = _dispatch_tables(top_e, counts, n_tok)
    ys = _experts(hx, src, dst, items, jnp.swapaxes(w_gate, 2, 3), jnp.swapaxes(w_up, 2, 3), w_down, layer)
    return _combine(hx, ys, wts.T, _pad_ff(sg, 1), _pad_ff(su, 1), _pad_ff(sd, 0), x, gt, g_post)


def kernel(x, c, ctx, c_ctx, mod_w, mod_b, norm_g, a_w_in, a_g_q, a_w_uq, a_g_kv, a_w_ukv, a_w_out, o_w_in, o_g_v, o_w_s, o_b_s, o_conv_w, o_w_out, moe_w_router, moe_b_router, moe_w_gate, moe_w_up, moe_w_down, sh_w_gate, sh_w_up, sh_w_down):
    b, s, d = x.shape
    lc = ctx.shape[1]

    cond = jnp.concatenate([c, c_ctx[None, :], jnp.zeros((COND_ROWS - b - 1, d), F32)], axis=0)
    mod = _adaln(cond, mod_w, mod_b)

    def mod_rows(layer, rows):
        m = mod[layer, rows]
        return [m[:, None, j * d:(j + 1) * d] for j in range(6)]

    def moe(layer, hx, lg, x, gt, g_post):
        return _moe(hx, lg, layer, moe_b_router[layer], moe_w_gate, moe_w_up, moe_w_down,
                    sh_w_gate[layer], sh_w_up[layer], sh_w_down[layer], x, gt, g_post)

    sh1, sc1, gt1, sh2, sc2, gt2 = mod_rows(0, slice(0, b))
    csh1, csc1 = [jnp.broadcast_to(m, (b, 1, d)) for m in mod_rows(0, slice(b, b + 1))[:2]]
    g = norm_g[0]
    w_in = a_w_in[0]
    o = A_Q_RANK + A_KV_RANK
    k_pe_w = w_in[:, o:o + A_QK_ROPE]
    win = jnp.concatenate([w_in[:, :o], w_in[:, o + A_QK_ROPE:], k_pe_w, _swap_halves(k_pe_w)], axis=1).astype(BF16)
    wq = a_w_uq[0].reshape(A_Q_RANK, A_HEADS, A_QK_NOPE + A_QK_ROPE)
    wq_pe = wq[:, :, A_QK_NOPE:]
    wuq = jnp.concatenate([wq[:, :, :A_QK_NOPE].reshape(A_Q_RANK, NOPE_W), wq_pe.reshape(A_Q_RANK, ROPE_W),
                           _swap_halves(wq_pe).reshape(A_Q_RANK, ROPE_W)], axis=1).astype(BF16)
    wkv = a_w_ukv[0].reshape(A_KV_RANK, A_HEADS, A_QK_NOPE + A_V_DIM)
    wukv = jnp.concatenate([wkv[:, :, :A_QK_NOPE].reshape(A_KV_RANK, NOPE_W),
                            wkv[:, :, A_QK_NOPE:].reshape(A_KV_RANK, A_WIDTH)], axis=1).astype(BF16)
    gq = a_g_q[0][None, :]
    gkv = a_g_kv[0][None, :]
    cos, sin = _rope_tables(s)
    cc, cs = _dft_tables(B_GROUP_DIM)
    eye = jnp.eye(B_GROUPS, dtype=F32)
    bdc = jnp.kron(eye, cc).astype(BF16)
    bds = jnp.kron(eye, -cs).astype(BF16)
    a_c, a_s = _dft_tables(s)
    a_c = a_c.astype(BF16)
    a_s = a_s.astype(BF16)

    ones = jnp.ones((lc, ROPE_W), F32)
    _, kc, vc, _, _ = _even_in(ctx, csh1, csc1, g[0][None, :], win, gq, wuq, gkv, wukv, ones, jnp.zeros_like(ones),
                               bdc, bds)
    q, ko, vo, uc, us = _even_in(x, sh1, sc1, g[0][None, :], win, gq, wuq, gkv, wukv, cos, sin, bdc, bds)
    o_a = _attention(q, kc, ko, vc, vo)
    o_b = _fourier(a_c, a_s, uc, us)
    wrh, wrl = _router_parts(moe_w_router[0])
    x, hx, lg = _even_out(o_a, o_b, a_w_out[0].astype(BF16), x, gt1, sh2, sc2, g[1][None, :], g[2][None, :], wrh, wrl)
    x = moe(0, hx, lg, x, gt2, g[3][None, :])

    sh1, sc1, gt1, sh2, sc2, gt2 = mod_rows(1, slice(0, b))
    g = norm_g[1]
    oc, s_, gb = _odd_in(x, sh1, sc1, g[0][None, :], o_w_in[0].astype(BF16), o_g_v[0][None, :],
                         o_w_s[0].astype(BF16), o_b_s[0].T)
    wrh, wrl = _router_parts(moe_w_router[1])
    x, hx, lg = _odd_out(oc, s_, gb, o_conv_w[0], o_w_out[0].astype(BF16), x, gt1, sh2, sc2, g[1][None, :],
                         g[2][None, :], wrh, wrl)
    x = moe(1, hx, lg, x, gt2, g[3][None, :])
    return x
```

```python
import functools

import jax
import jax.numpy as jnp
from jax import lax
from jax.experimental import pallas as pl
from jax.experimental.pallas import tpu as pltpu

F32 = jnp.float32
BF16 = jnp.bfloat16
U32 = jnp.uint32

D_MODEL = 2048
GRID_W = 64
EPS = 1e-6
A_HEADS = 12
A_QK_NOPE = 128
A_QK_ROPE = 64
A_V_DIM = 128
A_Q_RANK = 768
A_KV_RANK = 512
ROPE_BASE = 10000.0
B_GROUPS = 4
B_GROUP_DIM = 128
B_WIDTH = B_GROUPS * B_GROUP_DIM
A_WIDTH = A_HEADS * A_V_DIM
C_HEADS = 8
C_HEAD_DIM = 128
CHUNK = 128
C_WIDTH = C_HEADS * C_HEAD_DIM
D_WIDTH = 1024
N_EXPERTS = 64
TOP_K = 6
N_GROUPS = 8
GROUP_SIZE = N_EXPERTS // N_GROUPS
TOPK_GROUPS = 4
EXPERT_FF = 704
ROUTED_SCALE = 2.5

LANES = 128
VMEM_BYTES_V7X = 64 * 1024 * 1024
FF_PAD = 768
HALF = D_MODEL // 2
TILE_ROWS = 8
QK_PAD = 256
V_PAD = 256
LOG2_E = 1.4426950408889634
COND_ROWS = 32
ROW_BLOCK = 256
OUT_BLOCK = 512
OUT_SUB_BLOCK = 256
EXPERT_BLOCK = 256
ROUTE_BLOCK = 512


def _cparams(semantics, vmem_mib):
    assert vmem_mib * 1024 * 1024 < VMEM_BYTES_V7X
    return pltpu.CompilerParams(dimension_semantics=semantics, vmem_limit_bytes=vmem_mib * 1024 * 1024)


def _resident(shape):
    nd = len(shape)
    return pl.BlockSpec(shape, lambda *_: (0,) * nd, pipeline_mode=pl.Buffered(1))


def _dot(a, b):
    return jnp.dot(a, b, preferred_element_type=F32)


def _dot_nt(a, b):
    return lax.dot_general(a, b, (((1,), (1,)), ((), ())), preferred_element_type=F32)


def _rms(xf, g):
    return xf * lax.rsqrt(jnp.mean(xf * xf, axis=-1, keepdims=True) + EPS) * g


def _split_bf16(a):
    hi = a.astype(BF16)
    lo = (a - hi.astype(F32)).astype(BF16)
    return hi, lo


def _pack_halves(lo_f32, hi_f32):
    lo = lax.bitcast_convert_type(lo_f32.astype(BF16).astype(F32), U32) >> 16
    hi = lax.bitcast_convert_type(hi_f32.astype(BF16).astype(F32), U32) & jnp.uint32(0xFFFF0000)
    return lo | hi


def _unpack_halves(w):
    lo = lax.bitcast_convert_type(w << 16, F32)
    hi = lax.bitcast_convert_type(w & jnp.uint32(0xFFFF0000), F32)
    return lo, hi


def _store_token_tiles(ref, packed, first_row=0):
    m = packed.shape[0]
    for s in range(TILE_ROWS):
        ref[pl.ds(first_row * TILE_ROWS + s, m, stride=TILE_ROWS), :] = packed[:, s * LANES:(s + 1) * LANES]


def _load_token_tiles(ref, m):
    return [ref[pl.ds(s, m, stride=TILE_ROWS), :] for s in range(TILE_ROWS)]


def _unpack_token_tiles(ref, m):
    los, his = [], []
    for w in _load_token_tiles(ref, m):
        lo, hi = _unpack_halves(w)
        los.append(lo.astype(BF16))
        his.append(hi.astype(BF16))
    return jnp.concatenate(los, axis=1), jnp.concatenate(his, axis=1)


def _silu(x):
    return x * jax.nn.sigmoid(x)


def _adaln_kernel(c_ref, w_ref, b_ref, o_ref):
    a_hi, a_lo = _split_bf16(_silu(c_ref[...]))
    w_hi, w_lo = _split_bf16(w_ref[0])
    o_ref[0] = _dot(a_hi, w_hi) + _dot(a_lo, w_hi) + _dot(a_hi, w_lo) + b_ref[0]


def _adaln(cond, mod_w, mod_b):
    depth, d, n = mod_w.shape
    tn = 512
    return pl.pallas_call(
        _adaln_kernel,
        grid=(depth, n // tn),
        in_specs=[
            pl.BlockSpec((COND_ROWS, d), lambda l, j: (0, 0)),
            pl.BlockSpec((1, d, tn), lambda l, j: (l, 0, j)),
            pl.BlockSpec((1, 1, tn), lambda l, j: (l, 0, j)),
        ],
        out_specs=pl.BlockSpec((1, COND_ROWS, tn), lambda l, j: (l, 0, j)),
        out_shape=jax.ShapeDtypeStruct((depth, COND_ROWS, n), F32),
        compiler_params=_cparams(("parallel", "parallel"), 32),
        name="adaln",
    )(cond, mod_w, mod_b.reshape(depth, 1, n))


NOPE_W = A_HEADS * A_QK_NOPE
ROPE_W = A_HEADS * A_QK_ROPE
EVEN_IN_PAD = A_Q_RANK + A_KV_RANK + B_WIDTH + 2 * A_QK_ROPE


def _even_in_kernel(x_ref, sh_ref, sc_ref, g_ref, win_ref, gq_ref, wuq_ref, gkv_ref, wukv_ref,
                    cos_ref, sin_ref, bdc_ref, bds_ref, q_ref, k_ref, v_ref, uc_ref, us_ref):
    tm = x_ref.shape[1]
    h = _rms(x_ref[0], g_ref[...]) * (1.0 + sc_ref[0]) + sh_ref[0]
    z = _dot(h.astype(BF16), win_ref[...])
    c_q = z[:, :A_Q_RANK]
    c_kv = z[:, A_Q_RANK:A_Q_RANK + A_KV_RANK]
    o = A_Q_RANK + A_KV_RANK
    u_f = z[:, o:o + B_WIDTH].astype(BF16)
    k_pe = z[:, o + B_WIDTH:o + B_WIDTH + A_QK_ROPE]
    k_pe_sw = z[:, o + B_WIDTH + A_QK_ROPE:]
    cos = cos_ref[...]
    sin = sin_ref[...]
    scale = (A_QK_NOPE + A_QK_ROPE) ** -0.5 * LOG2_E

    q = _dot(_rms(c_q, gq_ref[...]).astype(BF16), wuq_ref[...]) * scale
    q_pe = q[:, NOPE_W:NOPE_W + ROPE_W] * cos + q[:, NOPE_W + ROPE_W:] * sin
    kv = _dot(_rms(c_kv, gkv_ref[...]).astype(BF16), wukv_ref[...])
    k_pe_r = (k_pe * cos[:, :A_QK_ROPE] + k_pe_sw * sin[:, :A_QK_ROPE]).astype(BF16)
    zeros = jnp.zeros((tm, QK_PAD - A_QK_NOPE - A_QK_ROPE), BF16)
    ones_col = (lax.broadcasted_iota(jnp.int32, (tm, V_PAD - A_V_DIM), 1) == 0).astype(BF16)
    for hd in range(A_HEADS):
        n0 = hd * A_QK_NOPE
        r0 = hd * A_QK_ROPE
        q_ref[0, hd, :, :A_QK_NOPE] = q[:, n0:n0 + A_QK_NOPE].astype(BF16)
        q_ref[0, hd, :, A_QK_NOPE:A_QK_NOPE + A_QK_ROPE] = q_pe[:, r0:r0 + A_QK_ROPE].astype(BF16)
        q_ref[0, hd, :, A_QK_NOPE + A_QK_ROPE:] = zeros
        k_ref[0, hd, :, :A_QK_NOPE] = kv[:, n0:n0 + A_QK_NOPE].astype(BF16)
        k_ref[0, hd, :, A_QK_NOPE:A_QK_NOPE + A_QK_ROPE] = k_pe_r
        k_ref[0, hd, :, A_QK_NOPE + A_QK_ROPE:] = zeros
        v_ref[0, hd, :, :A_V_DIM] = kv[:, NOPE_W + n0:NOPE_W + n0 + A_V_DIM].astype(BF16)
        v_ref[0, hd, :, A_V_DIM:] = ones_col
    uc_ref[0] = _dot(u_f, bdc_ref[...]).astype(BF16)
    us_ref[0] = _dot(u_f, bds_ref[...]).astype(BF16)


def _even_in(x, sh, sc, g, win, gq, wuq, gkv, wukv, cos, sin, bdc, bds):
    b, l, d = x.shape
    tm = ROW_BLOCK
    row = lambda bi, i: (bi, i, 0)
    per_b = lambda bi, i: (bi, 0, 0)
    heads = lambda bi, i: (bi, 0, i, 0)
    return pl.pallas_call(
        _even_in_kernel,
        grid=(b, l // tm),
        in_specs=[
            pl.BlockSpec((1, tm, d), row),
            pl.BlockSpec((1, 1, d), per_b),
            pl.BlockSpec((1, 1, d), per_b),
            _resident(g.shape), _resident(win.shape), _resident(gq.shape), _resident(wuq.shape),
            _resident(gkv.shape), _resident(wukv.shape),
            pl.BlockSpec((tm, ROPE_W), lambda bi, i: (i, 0)),
            pl.BlockSpec((tm, ROPE_W), lambda bi, i: (i, 0)),
            _resident(bdc.shape), _resident(bds.shape),
        ],
        out_specs=[
            pl.BlockSpec((1, A_HEADS, tm, QK_PAD), heads),
            pl.BlockSpec((1, A_HEADS, tm, QK_PAD), heads),
            pl.BlockSpec((1, A_HEADS, tm, V_PAD), heads),
            pl.BlockSpec((1, tm, B_WIDTH), row),
            pl.BlockSpec((1, tm, B_WIDTH), row),
        ],
        out_shape=[
            jax.ShapeDtypeStruct((b, A_HEADS, l, QK_PAD), BF16),
            jax.ShapeDtypeStruct((b, A_HEADS, l, QK_PAD), BF16),
            jax.ShapeDtypeStruct((b, A_HEADS, l, V_PAD), BF16),
            jax.ShapeDtypeStruct((b, l, B_WIDTH), BF16),
            jax.ShapeDtypeStruct((b, l, B_WIDTH), BF16),
        ],
        compiler_params=_cparams(("parallel", "parallel"), 56),
        name="even_in",
    )(x, sh, sc, g, win, gq, wuq, gkv, wukv, cos, sin, bdc, bds)


ATTN_Q_BLOCK = 512


def _attn_kernel(q_ref, kc_ref, ko_ref, vc_ref, vo_ref, o_ref):
    kc = kc_ref[0, 0]
    ko = ko_ref[0, 0]
    vc = vc_ref[0, 0]
    vo = vo_ref[0, 0]
    tq = ATTN_Q_BLOCK
    for i in range(q_ref.shape[2] // tq):
        rows = slice(i * tq, (i + 1) * tq)
        q = q_ref[0, 0, rows, :]
        s_c = _dot_nt(q, kc)
        s_o = _dot_nt(q, ko)
        m = jnp.maximum(jnp.max(s_c, axis=-1, keepdims=True), jnp.max(s_o, axis=-1, keepdims=True))
        p_c = jnp.exp2(s_c - m).astype(BF16)
        p_o = jnp.exp2(s_o - m).astype(BF16)
        o = _dot(p_c, vc) + _dot(p_o, vo)
        o_ref[0, rows, :] = (o[:, :A_V_DIM] / o[:, A_V_DIM:A_V_DIM + 1]).astype(BF16)


def _attention(q, kc, ko, vc, vo):
    b, h, s, _ = q.shape
    lc = kc.shape[2]
    bh = lambda bi, hi: (bi, hi, 0, 0)
    return pl.pallas_call(
        _attn_kernel,
        grid=(b, h),
        in_specs=[
            pl.BlockSpec((1, 1, s, QK_PAD), bh),
            pl.BlockSpec((1, 1, lc, QK_PAD), bh),
            pl.BlockSpec((1, 1, s, QK_PAD), bh),
            pl.BlockSpec((1, 1, lc, V_PAD), bh),
            pl.BlockSpec((1, 1, s, V_PAD), bh),
        ],
        out_specs=pl.BlockSpec((1, s, A_V_DIM), lambda bi, hi: (bi, 0, hi)),
        out_shape=jax.ShapeDtypeStruct((b, s, h * A_V_DIM), BF16),
        compiler_params=_cparams(("parallel", "parallel"), 48),
        name="attention",
    )(q, kc, ko, vc, vo)


def _fourier_kernel(ac_ref, as_ref, uc_ref, us_ref, o_ref):
    o_ref[0] = (_dot(ac_ref[...], uc_ref[0]) + _dot(as_ref[...], us_ref[0])).astype(BF16)


def _fourier(a_c, a_s, uc, us):
    b, s, w = uc.shape
    tm = 512
    return pl.pallas_call(
        _fourier_kernel,
        grid=(b, s // tm),
        in_specs=[
            pl.BlockSpec((tm, s), lambda bi, i: (i, 0)),
            pl.BlockSpec((tm, s), lambda bi, i: (i, 0)),
            pl.BlockSpec((1, s, w), lambda bi, i: (bi, 0, 0)),
            pl.BlockSpec((1, s, w), lambda bi, i: (bi, 0, 0)),
        ],
        out_specs=pl.BlockSpec((1, tm, w), lambda bi, i: (bi, i, 0)),
        out_shape=jax.ShapeDtypeStruct((b, s, w), BF16),
        compiler_params=_cparams(("parallel", "parallel"), 32),
        name="fourier",
    )(a_c, a_s, uc, us)


def _mix_epilogue(mix, r0, x_ref, gt_ref, sh_ref, sc_ref, gpost_ref, gffn_ref, wrh_ref, wrl_ref,
                  xo_ref, hx_ref, lg_ref):
    m = mix.shape[0]
    rows = slice(r0, r0 + m)
    xn = x_ref[0, rows, :] + gt_ref[0] * _rms(mix, gpost_ref[...])
    xo_ref[0, rows, :] = xn
    t = _rms(xn, gffn_ref[...]) * (1.0 + sc_ref[0]) + sh_ref[0]
    t_hi, t_lo = _split_bf16(t)
    wrh = wrh_ref[...]
    lg = _dot(t_hi, wrh) + _dot(t_lo, wrh) + _dot(t_hi, wrl_ref[...])
    lg_ref[:, rows] = lg.T[:N_EXPERTS, :]
    tq = t_hi.astype(F32)
    _store_token_tiles(hx_ref, _pack_halves(tq[:, :HALF], tq[:, HALF:]), r0)


def _even_out_kernel(oa_ref, ob_ref, w_ref, *rest):
    for r0 in range(0, oa_ref.shape[1], OUT_SUB_BLOCK):
        rows = slice(r0, r0 + OUT_SUB_BLOCK)
        mix = _dot(oa_ref[0, rows, :], w_ref[:A_WIDTH, :]) + _dot(ob_ref[0, rows, :], w_ref[A_WIDTH:, :])
        _mix_epilogue(mix, r0, *rest)


def _odd_out_kernel(oc_ref, s_ref, sp_ref, sn_ref, gb_ref, cw_ref, w_ref, *rest):
    i = pl.program_id(1)
    tm = s_ref.shape[1]
    s = s_ref[0].astype(F32)
    prev_row = jnp.where(i > 0, sp_ref[0, 15:16, :].astype(F32), 0.0)
    next_row = jnp.where(i < pl.num_programs(1) - 1, sn_ref[0, 0:1, :].astype(F32), 0.0)
    row = lax.broadcasted_iota(jnp.int32, (tm, 1), 0)
    s_dn = jnp.where(row == 0, prev_row, pltpu.roll(s, 1, axis=0))
    s_up = jnp.where(row == tm - 1, next_row, pltpu.roll(s, tm - 1, axis=0))
    cw = cw_ref[...]
    conv = cw[0:1, :] * s_dn + cw[1:2, :] * s + cw[2:3, :] * s_up
    od = (gb_ref[0].astype(F32) * conv).astype(BF16)
    for r0 in range(0, tm, OUT_SUB_BLOCK):
        rows = slice(r0, r0 + OUT_SUB_BLOCK)
        mix = _dot(oc_ref[0, rows, :], w_ref[:C_WIDTH, :]) + _dot(od[rows, :], w_ref[C_WIDTH:, :])
        _mix_epilogue(mix, r0, *rest)


def _epilogue_specs(b, s, d, tm):
    per_b = lambda bi, i: (bi, 0, 0)
    row = lambda bi, i: (bi, i, 0)
    nblk = s // tm
    in_specs = [
        pl.BlockSpec((1, tm, d), row),
        pl.BlockSpec((1, 1, d), per_b),
        pl.BlockSpec((1, 1, d), per_b),
        pl.BlockSpec((1, 1, d), per_b),
        _resident((1, d)), _resident((1, d)),
        _resident((d, LANES)), _resident((d, LANES)),
    ]
    out_specs = [
        pl.BlockSpec((1, tm, d), row),
        pl.BlockSpec((tm * TILE_ROWS, LANES), lambda bi, i: (bi * nblk + i, 0)),
        pl.BlockSpec((N_EXPERTS, tm), lambda bi, i: (0, bi * nblk + i)),
    ]
    out_shape = [
        jax.ShapeDtypeStruct((b, s, d), F32),
        jax.ShapeDtypeStruct((b * s * TILE_ROWS, LANES), U32),
        jax.ShapeDtypeStruct((N_EXPERTS, b * s), F32),
    ]
    return in_specs, out_specs, out_shape


def _even_out(oa, ob, w_out, x, gt, sh, sc, gpost, gffn, wrh, wrl):
    b, s, d = x.shape
    tm = OUT_BLOCK
    row = lambda bi, i: (bi, i, 0)
    ep_in, out_specs, out_shape = _epilogue_specs(b, s, d, tm)
    return pl.pallas_call(
        _even_out_kernel,
        grid=(b, s // tm),
        in_specs=[pl.BlockSpec((1, tm, A_WIDTH), row), pl.BlockSpec((1, tm, B_WIDTH), row),
                  _resident(w_out.shape)] + ep_in,
        out_specs=out_specs, out_shape=out_shape,
        compiler_params=_cparams(("parallel", "parallel"), 56),
        name="even_out",
    )(oa, ob, w_out, x, gt, sh, sc, gpost, gffn, wrh, wrl)


def _odd_out(oc, s_, gb, conv_w, w_out, x, gt, sh, sc, gpost, gffn, wrh, wrl):
    b, s, d = x.shape
    tm = OUT_BLOCK
    halo = 16
    nh = s // halo
    row = lambda bi, i: (bi, i, 0)
    ep_in, out_specs, out_shape = _epilogue_specs(b, s, d, tm)
    return pl.pallas_call(
        _odd_out_kernel,
        grid=(b, s // tm),
        in_specs=[
            pl.BlockSpec((1, tm, C_WIDTH), row),
            pl.BlockSpec((1, tm, D_WIDTH), row),
            pl.BlockSpec((1, halo, D_WIDTH), lambda bi, i: (bi, jnp.maximum(i * (tm // halo) - 1, 0), 0)),
            pl.BlockSpec((1, halo, D_WIDTH), lambda bi, i: (bi, jnp.minimum((i + 1) * (tm // halo), nh - 1), 0)),
            pl.BlockSpec((1, tm, D_WIDTH), row),
            _resident(conv_w.shape),
            _resident(w_out.shape),
        ] + ep_in,
        out_specs=out_specs, out_shape=out_shape,
        compiler_params=_cparams(("parallel", "parallel"), 56),
        name="odd_out",
    )(oc, s_, s_, s_, gb, conv_w, w_out, x, gt, sh, sc, gpost, gffn, wrh, wrl)


def _odd_in_kernel(x_ref, sh_ref, sc_ref, g_ref, win_ref, gv_ref, ws_ref, bs_ref, oc_ref, s_ref, gb_ref):
    tm = x_ref.shape[1]
    hb = (_rms(x_ref[0], g_ref[...]) * (1.0 + sc_ref[0]) + sh_ref[0]).astype(BF16)
    u = jax.nn.gelu(_dot(hb, win_ref[:, :C_WIDTH]))
    v = jax.nn.gelu(_dot(hb, win_ref[:, C_WIDTH:2 * C_WIDTH]))
    gv = gv_ref[...]
    bs = bs_ref[...]
    for hd in range(C_HEADS):
        c0 = hd * C_HEAD_DIM
        vh = _rms(v[:, c0:c0 + C_HEAD_DIM], gv[:, c0:c0 + C_HEAD_DIM]).astype(BF16)
        w = ws_ref[hd]
        bias = bs[:, hd:hd + 1]
        for n in range(tm // CHUNK):
            p0 = n * CHUNK
            sv = _dot(w, vh[p0:p0 + CHUNK, :]) + bias
            oc_ref[0, p0:p0 + CHUNK, c0:c0 + C_HEAD_DIM] = (u[p0:p0 + CHUNK, c0:c0 + C_HEAD_DIM] * sv).astype(BF16)
    o = 2 * C_WIDTH
    hd_ = _dot(hb, win_ref[:, o:o + D_WIDTH])
    gb_ref[0] = _dot(hb, win_ref[:, o + D_WIDTH:o + 2 * D_WIDTH]).astype(BF16)
    gc = _dot(hb, win_ref[:, o + 2 * D_WIDTH:])
    s_ref[0] = (gc * hd_).astype(BF16)


def _odd_in(x, sh, sc, g, win, gv, ws, bs_t):
    b, s, d = x.shape
    tm = ROW_BLOCK
    row = lambda bi, i: (bi, i, 0)
    per_b = lambda bi, i: (bi, 0, 0)
    out = jax.ShapeDtypeStruct((b, s, C_WIDTH), BF16)
    return pl.pallas_call(
        _odd_in_kernel,
        grid=(b, s // tm),
        in_specs=[
            pl.BlockSpec((1, tm, d), row),
            pl.BlockSpec((1, 1, d), per_b),
            pl.BlockSpec((1, 1, d), per_b),
            _resident(g.shape), _resident(win.shape), _resident(gv.shape), _resident(ws.shape),
            _resident(bs_t.shape),
        ],
        out_specs=[pl.BlockSpec((1, tm, C_WIDTH), row)] * 3,
        out_shape=[out, out, out],
        compiler_params=_cparams(("parallel", "parallel"), 56),
        name="odd_in",
    )(x, sh, sc, g, win, gv, ws, bs_t)


def _first_index_of_max(vals, idx, sentinel):
    m = jnp.max(vals, axis=0, keepdims=True)
    first = jnp.min(jnp.where(vals == m, idx, sentinel), axis=0, keepdims=True)
    return m, first


def _route_kernel(lg_ref, b_ref, e_ref, w_ref, c_ref):
    tn = lg_ref.shape[1]
    scores = jax.nn.sigmoid(lg_ref[...])
    sel = scores + b_ref[...]
    row8 = lax.broadcasted_iota(jnp.int32, (8, tn), 0).astype(F32)
    neg = -jnp.inf
    gs = jnp.zeros((N_GROUPS, tn), F32)
    for g in range(N_GROUPS):
        blk = sel[g * GROUP_SIZE:(g + 1) * GROUP_SIZE, :]
        m1, i1 = _first_index_of_max(blk, row8, float(GROUP_SIZE))
        m2 = jnp.max(jnp.where(row8 == i1, neg, blk), axis=0, keepdims=True)
        gs = jnp.where(row8 == float(g), m1 + m2, gs)
    keep = jnp.zeros((N_GROUPS, tn), F32)
    cur = gs
    for _ in range(TOPK_GROUPS):
        _, first = _first_index_of_max(cur, row8, float(N_GROUPS))
        pick = row8 == first
        keep = jnp.where(pick, 1.0, keep)
        cur = jnp.where(pick, neg, cur)
    masked = jnp.concatenate(
        [jnp.where(jnp.max(jnp.where(row8 == float(g), keep, 0.0), axis=0, keepdims=True) > 0.5,
                   sel[g * GROUP_SIZE:(g + 1) * GROUP_SIZE, :], neg) for g in range(N_GROUPS)],
        axis=0)
    eidx = lax.broadcasted_iota(jnp.int32, (N_EXPERTS, tn), 0).astype(F32)
    e_out = jnp.zeros((8, tn), F32)
    w_out = jnp.zeros((8, tn), F32)
    total = jnp.zeros((1, tn), F32)
    picked = jnp.zeros((N_EXPERTS, tn), F32)
    for k in range(TOP_K):
        _, first = _first_index_of_max(masked, eidx, float(N_EXPERTS))
        pick = eidx == first
        wk = jnp.sum(jnp.where(pick, scores, 0.0), axis=0, keepdims=True)
        masked = jnp.where(pick, neg, masked)
        picked = jnp.where(pick, 1.0, picked)
        e_out = jnp.where(row8 == float(k), first, e_out)
        w_out = jnp.where(row8 == float(k), wk, w_out)
        total = total + wk
    e_ref[...] = e_out.astype(jnp.int32)
    w_ref[...] = w_out / total * ROUTED_SCALE
    c_ref[...] = jnp.broadcast_to(jnp.sum(picked, axis=1, keepdims=True), c_ref.shape).astype(jnp.int32)


def _route(logits_t, bias):
    e, t = logits_t.shape
    tn = ROUTE_BLOCK
    nblk = t // tn
    top_e, wts, cnt = pl.pallas_call(
        _route_kernel,
        grid=(nblk,),
        in_specs=[pl.BlockSpec((e, tn), lambda i: (0, i)), pl.BlockSpec((e, 1), lambda i: (0, 0))],
        out_specs=[pl.BlockSpec((8, tn), lambda i: (0, i))] * 2 + [pl.BlockSpec((e, LANES), lambda i: (0, i))],
        out_shape=[jax.ShapeDtypeStruct((8, t), jnp.int32), jax.ShapeDtypeStruct((8, t), F32),
                   jax.ShapeDtypeStruct((e, nblk * LANES), jnp.int32)],
        compiler_params=_cparams(("parallel",), 32),
        name="route",
    )(logits_t, bias.reshape(e, 1))
    counts = jnp.sum(cnt.reshape(e, nblk, LANES)[:, :, 0], axis=1)
    return top_e, wts, counts


def _dispatch_tables(top_e, counts, n_tok):
    bm = EXPERT_BLOCK
    n_asg = n_tok * TOP_K
    n_blk = n_asg // bm
    n_steps = n_blk + N_EXPERTS
    i32 = jnp.int32
    flat_e = top_e[:TOP_K].T.reshape(n_asg)
    order = jnp.argsort(flat_e, stable=True).astype(i32)
    tok = order // TOP_K
    k = order - tok * TOP_K
    src = jnp.concatenate([tok * TILE_ROWS, jnp.zeros((bm,), i32)]).reshape(n_blk + 1, 1, bm)
    spare = (n_asg + jnp.arange(bm, dtype=i32)) * TILE_ROWS
    dst = jnp.concatenate([spare, (k * n_tok + tok) * TILE_ROWS]).reshape(n_blk + 1, 1, bm)

    experts = jnp.arange(N_EXPERTS, dtype=i32)
    ends = jnp.cumsum(counts).astype(i32)
    starts = ends - counts
    first_blk = starts // bm
    n_e_blk = jnp.where(counts > 0, (ends - 1) // bm - first_blk + 1, 0)
    item_end = jnp.cumsum(n_e_blk).astype(i32)
    item_start = item_end - n_e_blk
    n_items = item_end[-1]
    e_last = jnp.max(jnp.where(counts > 0, experts, 0))
    w = jnp.arange(n_steps, dtype=i32)
    active = w < n_items
    e_w = jnp.where(active, jnp.sum((item_end[None, :] <= w[:, None]).astype(i32), axis=1), e_last)
    blk_w = jnp.where(active, first_blk[e_w] + w - item_start[e_w], n_blk)
    lo_w = jnp.clip(starts[e_w] - blk_w * bm, 0, bm)
    hi_w = jnp.clip(ends[e_w] - blk_w * bm, 0, bm)
    neg1 = jnp.full((1,), -1, i32)
    first_of_blk = (blk_w != jnp.concatenate([neg1, blk_w[:-1]])).astype(i32)
    first_of_e = ((e_w != jnp.concatenate([neg1, e_w[:-1]])) & active).astype(i32)
    cand = jnp.where(counts > 0, experts, N_EXPERTS)
    suffix_min = lax.cummin(cand[::-1])[::-1]
    nxt = jnp.concatenate([suffix_min[1:], jnp.full((1,), N_EXPERTS, i32)])
    nxt = jnp.where(nxt >= N_EXPERTS, -1, nxt)
    return src, dst, (blk_w, e_w, lo_w, hi_w, first_of_blk, first_of_e, nxt[e_w], n_items.reshape(1))


FF_CHUNK = 256
DOWN_CHUNK = 512


def _swiglu_tiles(x_ref, m, wg_ref, wu_ref, wd_ref, between=lambda: None):
    x_lo, x_hi = _unpack_token_tiles(x_ref, m)
    hs = []
    for c in range(FF_PAD // FF_CHUNK):
        cols = slice(c * FF_CHUNK, (c + 1) * FF_CHUNK)
        g = _dot(x_lo, wg_ref[:HALF, cols]) + _dot(x_hi, wg_ref[HALF:, cols])
        between()
        u = _dot(x_lo, wu_ref[:HALF, cols]) + _dot(x_hi, wu_ref[HALF:, cols])
        between()
        hs.append((_silu(g) * u).astype(BF16))
    h = jnp.concatenate(hs, axis=1)
    ys = []
    for n in range(D_MODEL // DOWN_CHUNK):
        ys.append(_dot(h, wd_ref[:, n * DOWN_CHUNK:(n + 1) * DOWN_CHUNK]))
        between()
    return ys


def _packed_chunks(ys):
    per = DOWN_CHUNK // LANES
    half_chunks = HALF // DOWN_CHUNK
    out = []
    for s in range(TILE_ROWS):
        c0 = (s % per) * LANES
        lo = ys[s // per][:, c0:c0 + LANES]
        hi = ys[half_chunks + s // per][:, c0:c0 + LANES]
        out.append(_pack_halves(lo, hi))
    return out


N_DMA_SLOTS = (FF_PAD // FF_CHUNK) * 2 + D_MODEL // DOWN_CHUNK


def _expert_kernel(blk_ref, exp_ref, lo_ref, hi_ref, fb_ref, fe_ref, nxt_ref, ni_ref,
                   src0_ref, src1_ref, dst_ref, hx_hbm, wg_hbm, wu_hbm, wd_hbm,
                   ys_hbm,
                   xbuf0, xbuf1, ybuf0, ybuf1, sg, su, sd, wg, wu, wd, gsem, ssem, wsem, *, layer, n_blk):
    bm = EXPERT_BLOCK
    w = pl.program_id(0)
    n_items = ni_ref[0]
    blk = blk_ref[w]
    expert = exp_ref[w]
    active = w < n_items
    xbufs = (xbuf0, xbuf1)
    ybufs = (ybuf0, ybuf1)
    tile = lambda r: pl.ds(r * TILE_ROWS, TILE_ROWS)

    def weight_copies(e):
        return (pltpu.make_async_copy(wg_hbm.at[layer, e], sg.at[pl.ds(0, EXPERT_FF), :], wsem.at[0]),
                pltpu.make_async_copy(wu_hbm.at[layer, e], su.at[pl.ds(0, EXPERT_FF), :], wsem.at[1]),
                pltpu.make_async_copy(wd_hbm.at[layer, e], sd, wsem.at[2]))

    def gather_starts(src_ref, p):
        def one(r):
            row = pl.multiple_of(src_ref[0, 0, r], TILE_ROWS)
            pltpu.make_async_copy(hx_hbm.at[pl.ds(row, TILE_ROWS), :], xbufs[p].at[tile(r), :], gsem.at[p]).start()
        return [functools.partial(one, r) for r in range(bm)]

    def scatter_starts(p):
        def one(r):
            row = pl.multiple_of(dst_ref[0, 0, r], TILE_ROWS)
            pltpu.make_async_copy(ybufs[p].at[tile(r), :], ys_hbm.at[pl.ds(row, TILE_ROWS), :], ssem.at[0]).start()
        return [functools.partial(one, r) for r in range(bm)]

    def wait_gather(p):
        pltpu.make_async_copy(hx_hbm.at[pl.ds(0, bm * TILE_ROWS), :], xbufs[p], gsem.at[p]).wait()

    def wait_scatter(p):
        pltpu.make_async_copy(ybufs[p], ys_hbm.at[pl.ds(0, bm * TILE_ROWS), :], ssem.at[0]).wait()

    @pl.when(w == 0)
    def _():
        sg[EXPERT_FF:, :] = jnp.zeros((FF_PAD - EXPERT_FF, D_MODEL), F32)
        su[EXPERT_FF:, :] = jnp.zeros((FF_PAD - EXPERT_FF, D_MODEL), F32)
        wd[EXPERT_FF:, :] = jnp.zeros((FF_PAD - EXPERT_FF, D_MODEL), BF16)
        ybuf1[...] = jnp.zeros(ybuf1.shape, U32)
        for c in weight_copies(expert):
            c.start()
        for start in gather_starts(src0_ref, 0):
            start()

    @pl.when(jnp.logical_and(active, fe_ref[w] == 1))
    def _():
        for c in weight_copies(expert):
            c.wait()
        for c in range(FF_PAD // LANES):
            cols = slice(c * LANES, (c + 1) * LANES)
            wg[:, cols] = sg[cols, :].T.astype(BF16)
            wu[:, cols] = su[cols, :].T.astype(BF16)
        rows_d = 64

        def cast_down(i, carry):
            r = pl.ds(pl.multiple_of(i * rows_d, rows_d), rows_d)
            wd[r, :] = sd[r, :].astype(BF16)
            return carry

        lax.fori_loop(0, EXPERT_FF // rows_d, cast_down, 0)

        @pl.when(nxt_ref[w] >= 0)
        def _():
            for c in weight_copies(nxt_ref[w]):
                c.start()

    def first_item_of_block(p):
        @pl.when(blk > 0)
        def _():
            wait_scatter(p)
        wait_gather(p)
        pending = gather_starts(src1_ref, 1 - p) + scatter_starts(1 - p)
        per_slot = -(-len(pending) // N_DMA_SLOTS)

        def between():
            for start in pending[:per_slot]:
                start()
            del pending[:per_slot]

        ys = _swiglu_tiles(xbufs[p], bm, wg, wu, wd, between)
        assert not pending
        for s, chunk in enumerate(_packed_chunks(ys)):
            ybufs[p][pl.ds(s, bm, stride=TILE_ROWS), :] = chunk

    def later_item_of_block(p):
        ys = _swiglu_tiles(xbufs[p], bm, wg, wu, wd)
        row = lax.broadcasted_iota(jnp.int32, (bm, 1), 0)
        mine = jnp.logical_and(row >= lo_ref[w], row < hi_ref[w])
        for s, chunk in enumerate(_packed_chunks(ys)):
            idx = pl.ds(s, bm, stride=TILE_ROWS)
            ybufs[p][idx, :] = jnp.where(mine, chunk, ybufs[p][idx, :])

    parity = lax.rem(blk, 2)
    first = fb_ref[w] == 1
    for p in range(2):
        pl.when(jnp.logical_and(active, jnp.logical_and(first, parity == p)))(
            functools.partial(first_item_of_block, p))
        pl.when(jnp.logical_and(active, jnp.logical_and(jnp.logical_not(first), parity == p)))(
            functools.partial(later_item_of_block, p))

    @pl.when(w == n_items)
    def _():
        last = (n_blk - 1) % 2
        wait_scatter(1 - last)
        wait_gather(1 - last)
        for start in scatter_starts(last):
            start()
        wait_scatter(last)


def _experts(hx, src, dst, items, w_gate, w_up, w_down, layer):
    bm = EXPERT_BLOCK
    n_blk = src.shape[0] - 1
    n_steps = items[0].shape[0]
    n_out_rows = (n_blk + 1) * bm * TILE_ROWS
    smem = functools.partial(pl.BlockSpec, (1, 1, bm), memory_space=pltpu.SMEM)
    grid_spec = pltpu.PrefetchScalarGridSpec(
        num_scalar_prefetch=len(items),
        grid=(n_steps,),
        in_specs=[
            smem(index_map=lambda w, *_: (0, 0, 0)),
            smem(index_map=lambda w, blk, *_: (jnp.minimum(blk[w] + 1, n_blk), 0, 0)),
            smem(index_map=lambda w, blk, *_: (blk[w], 0, 0)),
            pl.BlockSpec(memory_space=pl.ANY),
            pl.BlockSpec(memory_space=pl.ANY),
            pl.BlockSpec(memory_space=pl.ANY),
            pl.BlockSpec(memory_space=pl.ANY),
        ],
        out_specs=pl.BlockSpec(memory_space=pl.ANY),
        scratch_shapes=[
            pltpu.VMEM((bm * TILE_ROWS, LANES), U32), pltpu.VMEM((bm * TILE_ROWS, LANES), U32),
            pltpu.VMEM((bm * TILE_ROWS, LANES), U32), pltpu.VMEM((bm * TILE_ROWS, LANES), U32),
            pltpu.VMEM((FF_PAD, D_MODEL), F32), pltpu.VMEM((FF_PAD, D_MODEL), F32),
            pltpu.VMEM((EXPERT_FF, D_MODEL), F32),
            pltpu.VMEM((D_MODEL, FF_PAD), BF16), pltpu.VMEM((D_MODEL, FF_PAD), BF16),
            pltpu.VMEM((FF_PAD, D_MODEL), BF16),
            pltpu.SemaphoreType.DMA((2,)),
            pltpu.SemaphoreType.DMA((1,)),
            pltpu.SemaphoreType.DMA((3,)),
        ],
    )
    return pl.pallas_call(
        functools.partial(_expert_kernel, layer=layer, n_blk=n_blk),
        grid_spec=grid_spec,
        out_shape=jax.ShapeDtypeStruct((n_out_rows, LANES), U32),
        compiler_params=_cparams(("arbitrary",), 48),
        name="experts",
    )(*items, src, src, dst, hx, w_gate, w_up, w_down)


def _combine_kernel(hx_ref, y0, y1, y2, y3, y4, y5, wt_ref, wg_ref, wu_ref, wd_ref, x_ref, gt_ref, g_ref, o_ref,
                    acc_ref):
    tm = x_ref.shape[1]
    y_refs = (y0, y1, y2, y3, y4, y5)
    wt = wt_ref[...]
    wk = [jnp.broadcast_to(wt[:, k:k + 1], (tm, LANES)) for k in range(TOP_K)]
    todo = list(range(TILE_ROWS))

    def combine_one_chunk():
        if not todo:
            return
        s = todo.pop(0)
        idx = pl.ds(s, tm, stride=TILE_ROWS)
        acc_lo = acc_hi = None
        for k, y_ref in enumerate(y_refs):
            lo, hi = _unpack_halves(y_ref[idx, :])
            acc_lo = wk[k] * lo if k == 0 else acc_lo + wk[k] * lo
            acc_hi = wk[k] * hi if k == 0 else acc_hi + wk[k] * hi
        acc_ref[:, s * LANES:(s + 1) * LANES] = acc_lo
        acc_ref[:, HALF + s * LANES:HALF + (s + 1) * LANES] = acc_hi

    shared = jnp.concatenate(_swiglu_tiles(hx_ref, tm, wg_ref, wu_ref, wd_ref, combine_one_chunk), axis=1)
    while todo:
        combine_one_chunk()
    ff = acc_ref[...] + shared
    o_ref[0] = x_ref[0] + gt_ref[0] * _rms(ff, g_ref[...])


def _combine(hx, ys, wts_t, wg, wu, wd, x, gt, g):
    b, s, d = x.shape
    tm = ROW_BLOCK
    nblk = s // tm
    n_tok = b * s
    row = lambda bi, i: (bi, i, 0)
    tiles = (tm * TILE_ROWS, LANES)
    y_specs = [pl.BlockSpec(tiles, functools.partial(lambda bi, i, k: (k * (n_tok // tm) + bi * nblk + i, 0), k=k))
               for k in range(TOP_K)]
    return pl.pallas_call(
        _combine_kernel,
        grid=(b, nblk),
        in_specs=[pl.BlockSpec(tiles, lambda bi, i: (bi * nblk + i, 0))] + y_specs + [
            pl.BlockSpec((tm, 8), lambda bi, i: (bi * nblk + i, 0)),
            _resident(wg.shape), _resident(wu.shape), _resident(wd.shape),
            pl.BlockSpec((1, tm, d), row),
            pl.BlockSpec((1, 1, d), lambda bi, i: (bi, 0, 0)),
            _resident(g.shape),
        ],
        out_specs=pl.BlockSpec((1, tm, d), row),
        out_shape=jax.ShapeDtypeStruct((b, s, d), F32),
        scratch_shapes=[pltpu.VMEM((tm, d), F32)],
        compiler_params=_cparams(("parallel", "parallel"), 48),
        name="combine",
    )(hx, ys, ys, ys, ys, ys, ys, wts_t, wg, wu, wd, x, gt, g)


def _rope_tables(seq):
    t = jnp.arange(seq, dtype=jnp.int32)
    row = (t // GRID_W).astype(F32)
    col = (t % GRID_W).astype(F32)
    per_axis = A_QK_ROPE // 4
    inv = ROPE_BASE ** (-jnp.arange(per_axis, dtype=F32) / per_axis)
    ang = jnp.concatenate([row[:, None] * inv, col[:, None] * inv], axis=-1)
    cos = jnp.cos(ang)
    sin = jnp.sin(ang)
    cos64 = jnp.concatenate([cos, cos], axis=-1)
    sin64 = jnp.concatenate([-sin, sin], axis=-1)
    return jnp.tile(cos64, (1, A_HEADS)), jnp.tile(sin64, (1, A_HEADS))


def _dft_tables(n):
    k = jnp.arange(n, dtype=jnp.int32)
    ang = ((k[:, None] * k[None, :]) % n).astype(F32) * (2.0 * jnp.pi / n)
    scale = n ** -0.5
    return jnp.cos(ang) * scale, -jnp.sin(ang) * scale


def _swap_halves(w):
    half = w.shape[-1] // 2
    return jnp.concatenate([w[..., half:], w[..., :half]], axis=-1)


def _pad_ff(w, axis):
    pad = [(0, 0)] * w.ndim
    pad[axis] = (0, FF_PAD - EXPERT_FF)
    return jnp.pad(w.astype(BF16), pad)


def _router_parts(w_router):
    w = jnp.pad(w_router, ((0, 0), (0, LANES - N_EXPERTS)))
    hi = w.astype(BF16)
    lo = (w - hi.astype(F32)).astype(BF16)
    return hi, lo


def _moe(hx, logits_t, layer, b_router, w_gate, w_up, w_down, sg, su, sd, x, gt, g_post):
    b, s, d = x.shape
    n_tok = b * s
    top_e, wts, counts = _route(logits_t, b_router)
    src, dst, items = _dispatch_tables(top_e, counts, n_tok)
    ys = _experts(hx, src, dst, items, jnp.swapaxes(w_gate, 2, 3), jnp.swapaxes(w_up, 2, 3), w_down, layer)
    return _combine(hx, ys, wts.T, _pad_ff(sg, 1), _pad_ff(su, 1), _pad_ff(sd, 0), x, gt, g_post)


def kernel(x, c, ctx, c_ctx, mod_w, mod_b, norm_g, a_w_in, a_g_q, a_w_uq, a_g_kv, a_w_ukv, a_w_out, o_w_in, o_g_v, o_w_s, o_b_s, o_conv_w, o_w_out, moe_w_router, moe_b_router, moe_w_gate, moe_w_up, moe_w_down, sh_w_gate, sh_w_up, sh_w_down):
    b, s, d = x.shape
    lc = ctx.shape[1]

    cond = jnp.concatenate([c, c_ctx[None, :], jnp.zeros((COND_ROWS - b - 1, d), F32)], axis=0)
    mod = _adaln(cond, mod_w, mod_b)

    def mod_rows(layer, rows):
        m = mod[layer, rows]
        return [m[:, None, j * d:(j + 1) * d] for j in range(6)]

    def moe(layer, hx, lg, x, gt, g_post):
        return _moe(hx, lg, layer, moe_b_router[layer], moe_w_gate, moe_w_up, moe_w_down,
                    sh_w_gate[layer], sh_w_up[layer], sh_w_down[layer], x, gt, g_post)

    sh1, sc1, gt1, sh2, sc2, gt2 = mod_rows(0, slice(0, b))
    csh1, csc1 = [jnp.broadcast_to(m, (b, 1, d)) for m in mod_rows(0, slice(b, b + 1))[:2]]
    g = norm_g[0]
    w_in = a_w_in[0]
    o = A_Q_RANK + A_KV_RANK
    k_pe_w = w_in[:, o:o + A_QK_ROPE]
    win = jnp.concatenate([w_in[:, :o], w_in[:, o + A_QK_ROPE:], k_pe_w, _swap_halves(k_pe_w)], axis=1).astype(BF16)
    wq = a_w_uq[0].reshape(A_Q_RANK, A_HEADS, A_QK_NOPE + A_QK_ROPE)
    wq_pe = wq[:, :, A_QK_NOPE:]
    wuq = jnp.concatenate([wq[:, :, :A_QK_NOPE].reshape(A_Q_RANK, NOPE_W), wq_pe.reshape(A_Q_RANK, ROPE_W),
                           _swap_halves(wq_pe).reshape(A_Q_RANK, ROPE_W)], axis=1).astype(BF16)
    wkv = a_w_ukv[0].reshape(A_KV_RANK, A_HEADS, A_QK_NOPE + A_V_DIM)
    wukv = jnp.concatenate([wkv[:, :, :A_QK_NOPE].reshape(A_KV_RANK, NOPE_W),
                            wkv[:, :, A_QK_NOPE:].reshape(A_KV_RANK, A_WIDTH)], axis=1).astype(BF16)
    gq = a_g_q[0][None, :]
    gkv = a_g_kv[0][None, :]
    cos, sin = _rope_tables(s)
    cc, cs = _dft_tables(B_GROUP_DIM)
    eye = jnp.eye(B_GROUPS, dtype=F32)
    bdc = jnp.kron(eye, cc).astype(BF16)
    bds = jnp.kron(eye, -cs).astype(BF16)
    a_c, a_s = _dft_tables(s)
    a_c = a_c.astype(BF16)
    a_s = a_s.astype(BF16)

    ones = jnp.ones((lc, ROPE_W), F32)
    _, kc, vc, _, _ = _even_in(ctx, csh1, csc1, g[0][None, :], win, gq, wuq, gkv, wukv, ones, jnp.zeros_like(ones),
                               bdc, bds)
    q, ko, vo, uc, us = _even_in(x, sh1, sc1, g[0][None, :], win, gq, wuq, gkv, wukv, cos, sin, bdc, bds)
    o_a = _attention(q, kc, ko, vc, vo)
    o_b = _fourier(a_c, a_s, uc, us)
    wrh, wrl = _router_parts(moe_w_router[0])
    x, hx, lg = _even_out(o_a, o_b, a_w_out[0].astype(BF16), x, gt1, sh2, sc2, g[1][None, :], g[2][None, :], wrh, wrl)
    x = moe(0, hx, lg, x, gt2, g[3][None, :])

    sh1, sc1, gt1, sh2, sc2, gt2 = mod_rows(1, slice(0, b))
    g = norm_g[1]
    oc, s_, gb = _odd_in(x, sh1, sc1, g[0][None, :], o_w_in[0].astype(BF16), o_g_v[0][None, :],
                         o_w_s[0].astype(BF16), o_b_s[0].T)
    wrh, wrl = _router_parts(moe_w_router[1])
    x, hx, lg = _odd_out(oc, s_, gb, o_conv_w[0], o_w_out[0].astype(BF16), x, gt1, sh2, sc2, g[1][None, :],
                         g[2][None, :], wrh, wrl)
    x = moe(1, hx, lg, x, gt2, g[3][None, :])
    return x
```

```python
import functools

import jax
import jax.numpy as jnp
from jax import lax
from jax.experimental import pallas as pl
from jax.experimental.pallas import tpu as pltpu

F32 = jnp.float32
BF16 = jnp.bfloat16
U32 = jnp.uint32

D_MODEL = 2048
GRID_W = 64
EPS = 1e-6
A_HEADS = 12
A_QK_NOPE = 128
A_QK_ROPE = 64
A_V_DIM = 128
A_Q_RANK = 768
A_KV_RANK = 512
ROPE_BASE = 10000.0
B_GROUPS = 4
B_GROUP_DIM = 128
B_WIDTH = B_GROUPS * B_GROUP_DIM
A_WIDTH = A_HEADS * A_V_DIM
C_HEADS = 8
C_HEAD_DIM = 128
CHUNK = 128
C_WIDTH = C_HEADS * C_HEAD_DIM
D_WIDTH = 1024
N_EXPERTS = 64
TOP_K = 6
N_GROUPS = 8
GROUP_SIZE = N_EXPERTS // N_GROUPS
TOPK_GROUPS = 4
EXPERT_FF = 704
ROUTED_SCALE = 2.5

LANES = 128
VMEM_BYTES_V7X = 64 * 1024 * 1024
FF_PAD = 768
HALF = D_MODEL // 2
TILE_ROWS = 8
QK_PAD = 256
V_PAD = 256
LOG2_E = 1.4426950408889634
COND_ROWS = 32
ROW_BLOCK = 256
OUT_BLOCK = 512
OUT_SUB_BLOCK = 256
EXPERT_BLOCK = 256
ROUTE_BLOCK = 512


def _cparams(semantics, vmem_mib):
    assert vmem_mib * 1024 * 1024 < VMEM_BYTES_V7X
    return pltpu.CompilerParams(dimension_semantics=semantics, vmem_limit_bytes=vmem_mib * 1024 * 1024)


def _resident(shape):
    nd = len(shape)
    return pl.BlockSpec(shape, lambda *_: (0,) * nd, pipeline_mode=pl.Buffered(1))


def _dot(a, b):
    return jnp.dot(a, b, preferred_element_type=F32)


def _dot_nt(a, b):
    return lax.dot_general(a, b, (((1,), (1,)), ((), ())), preferred_element_type=F32)


def _rms(xf, g):
    return xf * lax.rsqrt(jnp.mean(xf * xf, axis=-1, keepdims=True) + EPS) * g


def _split_bf16(a):
    hi = a.astype(BF16)
    lo = (a - hi.astype(F32)).astype(BF16)
    return hi, lo


def _pack_halves(lo_f32, hi_f32):
    lo = lax.bitcast_convert_type(lo_f32.astype(BF16).astype(F32), U32) >> 16
    hi = lax.bitcast_convert_type(hi_f32.astype(BF16).astype(F32), U32) & jnp.uint32(0xFFFF0000)
    return lo | hi


def _unpack_halves(w):
    lo = lax.bitcast_convert_type(w << 16, F32)
    hi = lax.bitcast_convert_type(w & jnp.uint32(0xFFFF0000), F32)
    return lo, hi


def _store_token_tiles(ref, packed, first_row=0):
    m = packed.shape[0]
    for s in range(TILE_ROWS):
        ref[pl.ds(first_row * TILE_ROWS + s, m, stride=TILE_ROWS), :] = packed[:, s * LANES:(s + 1) * LANES]


def _load_token_tiles(ref, m):
    return [ref[pl.ds(s, m, stride=TILE_ROWS), :] for s in range(TILE_ROWS)]


def _unpack_token_tiles(ref, m):
    los, his = [], []
    for w in _load_token_tiles(ref, m):
        lo, hi = _unpack_halves(w)
        los.append(lo.astype(BF16))
        his.append(hi.astype(BF16))
    return jnp.concatenate(los, axis=1), jnp.concatenate(his, axis=1)


def _silu(x):
    return x * jax.nn.sigmoid(x)


def _adaln_kernel(c_ref, w_ref, b_ref, o_ref):
    a_hi, a_lo = _split_bf16(_silu(c_ref[...]))
    w_hi, w_lo = _split_bf16(w_ref[0])
    o_ref[0] = _dot(a_hi, w_hi) + _dot(a_lo, w_hi) + _dot(a_hi, w_lo) + b_ref[0]


def _adaln(cond, mod_w, mod_b):
    depth, d, n = mod_w.shape
    tn = 512
    return pl.pallas_call(
        _adaln_kernel,
        grid=(depth, n // tn),
        in_specs=[
            pl.BlockSpec((COND_ROWS, d), lambda l, j: (0, 0)),
            pl.BlockSpec((1, d, tn), lambda l, j: (l, 0, j)),
            pl.BlockSpec((1, 1, tn), lambda l, j: (l, 0, j)),
        ],
        out_specs=pl.BlockSpec((1, COND_ROWS, tn), lambda l, j: (l, 0, j)),
        out_shape=jax.ShapeDtypeStruct((depth, COND_ROWS, n), F32),
        compiler_params=_cparams(("parallel", "parallel"), 32),
        name="adaln",
    )(cond, mod_w, mod_b.reshape(depth, 1, n))


NOPE_W = A_HEADS * A_QK_NOPE
ROPE_W = A_HEADS * A_QK_ROPE
EVEN_IN_PAD = A_Q_RANK + A_KV_RANK + B_WIDTH + 2 * A_QK_ROPE


def _even_in_kernel(x_ref, sh_ref, sc_ref, g_ref, win_ref, gq_ref, wuq_ref, gkv_ref, wukv_ref,
                    cos_ref, sin_ref, bdc_ref, bds_ref, q_ref, k_ref, v_ref, uc_ref, us_ref):
    tm = x_ref.shape[1]
    h = _rms(x_ref[0], g_ref[...]) * (1.0 + sc_ref[0]) + sh_ref[0]
    z = _dot(h.astype(BF16), win_ref[...])
    c_q = z[:, :A_Q_RANK]
    c_kv = z[:, A_Q_RANK:A_Q_RANK + A_KV_RANK]
    o = A_Q_RANK + A_KV_RANK
    u_f = z[:, o:o + B_WIDTH].astype(BF16)
    k_pe = z[:, o + B_WIDTH:o + B_WIDTH + A_QK_ROPE]
    k_pe_sw = z[:, o + B_WIDTH + A_QK_ROPE:]
    cos = cos_ref[...]
    sin = sin_ref[...]
    scale = (A_QK_NOPE + A_QK_ROPE) ** -0.5 * LOG2_E

    q = _dot(_rms(c_q, gq_ref[...]).astype(BF16), wuq_ref[...]) * scale
    q_pe = q[:, NOPE_W:NOPE_W + ROPE_W] * cos + q[:, NOPE_W + ROPE_W:] * sin
    kv = _dot(_rms(c_kv, gkv_ref[...]).astype(BF16), wukv_ref[...])
    k_pe_r = (k_pe * cos[:, :A_QK_ROPE] + k_pe_sw * sin[:, :A_QK_ROPE]).astype(BF16)
    zeros = jnp.zeros((tm, QK_PAD - A_QK_NOPE - A_QK_ROPE), BF16)
    ones_col = (lax.broadcasted_iota(jnp.int32, (tm, V_PAD - A_V_DIM), 1) == 0).astype(BF16)
    for hd in range(A_HEADS):
        n0 = hd * A_QK_NOPE
        r0 = hd * A_QK_ROPE
        q_ref[0, hd, :, :A_QK_NOPE] = q[:, n0:n0 + A_QK_NOPE].astype(BF16)
        q_ref[0, hd, :, A_QK_NOPE:A_QK_NOPE + A_QK_ROPE] = q_pe[:, r0:r0 + A_QK_ROPE].astype(BF16)
        q_ref[0, hd, :, A_QK_NOPE + A_QK_ROPE:] = zeros
        k_ref[0, hd, :, :A_QK_NOPE] = kv[:, n0:n0 + A_QK_NOPE].astype(BF16)
        k_ref[0, hd, :, A_QK_NOPE:A_QK_NOPE + A_QK_ROPE] = k_pe_r
        k_ref[0, hd, :, A_QK_NOPE + A_QK_ROPE:] = zeros
        v_ref[0, hd, :, :A_V_DIM] = kv[:, NOPE_W + n0:NOPE_W + n0 + A_V_DIM].astype(BF16)
        v_ref[0, hd, :, A_V_DIM:] = ones_col
    uc_ref[0] = _dot(u_f, bdc_ref[...]).astype(BF16)
    us_ref[0] = _dot(u_f, bds_ref[...]).astype(BF16)


def _even_in(x, sh, sc, g, win, gq, wuq, gkv, wukv, cos, sin, bdc, bds):
    b, l, d = x.shape
    tm = ROW_BLOCK
    row = lambda bi, i: (bi, i, 0)
    per_b = lambda bi, i: (bi, 0, 0)
    heads = lambda bi, i: (bi, 0, i, 0)
    return pl.pallas_call(
        _even_in_kernel,
        grid=(b, l // tm),
        in_specs=[
            pl.BlockSpec((1, tm, d), row),
            pl.BlockSpec((1, 1, d), per_b),
            pl.BlockSpec((1, 1, d), per_b),
            _resident(g.shape), _resident(win.shape), _resident(gq.shape), _resident(wuq.shape),
            _resident(gkv.shape), _resident(wukv.shape),
            pl.BlockSpec((tm, ROPE_W), lambda bi, i: (i, 0)),
            pl.BlockSpec((tm, ROPE_W), lambda bi, i: (i, 0)),
            _resident(bdc.shape), _resident(bds.shape),
        ],
        out_specs=[
            pl.BlockSpec((1, A_HEADS, tm, QK_PAD), heads),
            pl.BlockSpec((1, A_HEADS, tm, QK_PAD), heads),
            pl.BlockSpec((1, A_HEADS, tm, V_PAD), heads),
            pl.BlockSpec((1, tm, B_WIDTH), row),
            pl.BlockSpec((1, tm, B_WIDTH), row),
        ],
        out_shape=[
            jax.ShapeDtypeStruct((b, A_HEADS, l, QK_PAD), BF16),
            jax.ShapeDtypeStruct((b, A_HEADS, l, QK_PAD), BF16),
            jax.ShapeDtypeStruct((b, A_HEADS, l, V_PAD), BF16),
            jax.ShapeDtypeStruct((b, l, B_WIDTH), BF16),
            jax.ShapeDtypeStruct((b, l, B_WIDTH), BF16),
        ],
        compiler_params=_cparams(("parallel", "parallel"), 56),
        name="even_in",
    )(x, sh, sc, g, win, gq, wuq, gkv, wukv, cos, sin, bdc, bds)


ATTN_Q_BLOCK = 512


def _attn_kernel(q_ref, kc_ref, ko_ref, vc_ref, vo_ref, o_ref):
    kc = kc_ref[0, 0]
    ko = ko_ref[0, 0]
    vc = vc_ref[0, 0]
    vo = vo_ref[0, 0]
    tq = ATTN_Q_BLOCK
    for i in range(q_ref.shape[2] // tq):
        rows = slice(i * tq, (i + 1) * tq)
        q = q_ref[0, 0, rows, :]
        s_c = _dot_nt(q, kc)
        s_o = _dot_nt(q, ko)
        m = jnp.maximum(jnp.max(s_c, axis=-1, keepdims=True), jnp.max(s_o, axis=-1, keepdims=True))
        p_c = jnp.exp2(s_c - m).astype(BF16)
        p_o = jnp.exp2(s_o - m).astype(BF16)
        o = _dot(p_c, vc) + _dot(p_o, vo)
        o_ref[0, rows, :] = (o[:, :A_V_DIM] / o[:, A_V_DIM:A_V_DIM + 1]).astype(BF16)


def _attention(q, kc, ko, vc, vo):
    b, h, s, _ = q.shape
    lc = kc.shape[2]
    bh = lambda bi, hi: (bi, hi, 0, 0)
    return pl.pallas_call(
        _attn_kernel,
        grid=(b, h),
        in_specs=[
            pl.BlockSpec((1, 1, s, QK_PAD), bh),
            pl.BlockSpec((1, 1, lc, QK_PAD), bh),
            pl.BlockSpec((1, 1, s, QK_PAD), bh),
            pl.BlockSpec((1, 1, lc, V_PAD), bh),
            pl.BlockSpec((1, 1, s, V_PAD), bh),
        ],
        out_specs=pl.BlockSpec((1, s, A_V_DIM), lambda bi, hi: (bi, 0, hi)),
        out_shape=jax.ShapeDtypeStruct((b, s, h * A_V_DIM), BF16),
        compiler_params=_cparams(("parallel", "parallel"), 48),
        name="attention",
    )(q, kc, ko, vc, vo)


def _fourier_kernel(ac_ref, as_ref, uc_ref, us_ref, o_ref):
    o_ref[0] = (_dot(ac_ref[...], uc_ref[0]) + _dot(as_ref[...], us_ref[0])).astype(BF16)


def _fourier(a_c, a_s, uc, us):
    b, s, w = uc.shape
    tm = 512
    return pl.pallas_call(
        _fourier_kernel,
        grid=(b, s // tm),
        in_specs=[
            pl.BlockSpec((tm, s), lambda bi, i: (i, 0)),
            pl.BlockSpec((tm, s), lambda bi, i: (i, 0)),
            pl.BlockSpec((1, s, w), lambda bi, i: (bi, 0, 0)),
            pl.BlockSpec((1, s, w), lambda bi, i: (bi, 0, 0)),
        ],
        out_specs=pl.BlockSpec((1, tm, w), lambda bi, i: (bi, i, 0)),
        out_shape=jax.ShapeDtypeStruct((b, s, w), BF16),
        compiler_params=_cparams(("parallel", "parallel"), 32),
        name="fourier",
    )(a_c, a_s, uc, us)


def _mix_epilogue(mix, r0, x_ref, gt_ref, sh_ref, sc_ref, gpost_ref, gffn_ref, wrh_ref, wrl_ref,
                  xo_ref, hx_ref, lg_ref):
    m = mix.shape[0]
    rows = slice(r0, r0 + m)
    xn = x_ref[0, rows, :] + gt_ref[0] * _rms(mix, gpost_ref[...])
    xo_ref[0, rows, :] = xn
    t = _rms(xn, gffn_ref[...]) * (1.0 + sc_ref[0]) + sh_ref[0]
    t_hi, t_lo = _split_bf16(t)
    wrh = wrh_ref[...]
    lg = _dot(t_hi, wrh) + _dot(t_lo, wrh) + _dot(t_hi, wrl_ref[...])
    lg_ref[:, rows] = lg.T[:N_EXPERTS, :]
    tq = t_hi.astype(F32)
    _store_token_tiles(hx_ref, _pack_halves(tq[:, :HALF], tq[:, HALF:]), r0)


def _even_out_kernel(oa_ref, ob_ref, w_ref, *rest):
    for r0 in range(0, oa_ref.shape[1], OUT_SUB_BLOCK):
        rows = slice(r0, r0 + OUT_SUB_BLOCK)
        mix = _dot(oa_ref[0, rows, :], w_ref[:A_WIDTH, :]) + _dot(ob_ref[0, rows, :], w_ref[A_WIDTH:, :])
        _mix_epilogue(mix, r0, *rest)


def _odd_out_kernel(oc_ref, s_ref, sp_ref, sn_ref, gb_ref, cw_ref, w_ref, *rest):
    i = pl.program_id(1)
    tm = s_ref.shape[1]
    s = s_ref[0].astype(F32)
    prev_row = jnp.where(i > 0, sp_ref[0, 15:16, :].astype(F32), 0.0)
    next_row = jnp.where(i < pl.num_programs(1) - 1, sn_ref[0, 0:1, :].astype(F32), 0.0)
    row = lax.broadcasted_iota(jnp.int32, (tm, 1), 0)
    s_dn = jnp.where(row == 0, prev_row, pltpu.roll(s, 1, axis=0))
    s_up = jnp.where(row == tm - 1, next_row, pltpu.roll(s, tm - 1, axis=0))
    cw = cw_ref[...]
    conv = cw[0:1, :] * s_dn + cw[1:2, :] * s + cw[2:3, :] * s_up
    od = (gb_ref[0].astype(F32) * conv).astype(BF16)
    for r0 in range(0, tm, OUT_SUB_BLOCK):
        rows = slice(r0, r0 + OUT_SUB_BLOCK)
        mix = _dot(oc_ref[0, rows, :], w_ref[:C_WIDTH, :]) + _dot(od[rows, :], w_ref[C_WIDTH:, :])
        _mix_epilogue(mix, r0, *rest)


def _epilogue_specs(b, s, d, tm):
    per_b = lambda bi, i: (bi, 0, 0)
    row = lambda bi, i: (bi, i, 0)
    nblk = s // tm
    in_specs = [
        pl.BlockSpec((1, tm, d), row),
        pl.BlockSpec((1, 1, d), per_b),
        pl.BlockSpec((1, 1, d), per_b),
        pl.BlockSpec((1, 1, d), per_b),
        _resident((1, d)), _resident((1, d)),
        _resident((d, LANES)), _resident((d, LANES)),
    ]
    out_specs = [
        pl.BlockSpec((1, tm, d), row),
        pl.BlockSpec((tm * TILE_ROWS, LANES), lambda bi, i: (bi * nblk + i, 0)),
        pl.BlockSpec((N_EXPERTS, tm), lambda bi, i: (0, bi * nblk + i)),
    ]
    out_shape = [
        jax.ShapeDtypeStruct((b, s, d), F32),
        jax.ShapeDtypeStruct((b * s * TILE_ROWS, LANES), U32),
        jax.ShapeDtypeStruct((N_EXPERTS, b * s), F32),
    ]
    return in_specs, out_specs, out_shape


def _even_out(oa, ob, w_out, x, gt, sh, sc, gpost, gffn, wrh, wrl):
    b, s, d = x.shape
    tm = OUT_BLOCK
    row = lambda bi, i: (bi, i, 0)
    ep_in, out_specs, out_shape = _epilogue_specs(b, s, d, tm)
    return pl.pallas_call(
        _even_out_kernel,
        grid=(b, s // tm),
        in_specs=[pl.BlockSpec((1, tm, A_WIDTH), row), pl.BlockSpec((1, tm, B_WIDTH), row),
                  _resident(w_out.shape)] + ep_in,
        out_specs=out_specs, out_shape=out_shape,
        compiler_params=_cparams(("parallel", "parallel"), 56),
        name="even_out",
    )(oa, ob, w_out, x, gt, sh, sc, gpost, gffn, wrh, wrl)


def _odd_out(oc, s_, gb, conv_w, w_out, x, gt, sh, sc, gpost, gffn, wrh, wrl):
    b, s, d = x.shape
    tm = OUT_BLOCK
    halo = 16
    nh = s // halo
    row = lambda bi, i: (bi, i, 0)
    ep_in, out_specs, out_shape = _epilogue_specs(b, s, d, tm)
    return pl.pallas_call(
        _odd_out_kernel,
        grid=(b, s // tm),
        in_specs=[
            pl.BlockSpec((1, tm, C_WIDTH), row),
            pl.BlockSpec((1, tm, D_WIDTH), row),
            pl.BlockSpec((1, halo, D_WIDTH), lambda bi, i: (bi, jnp.maximum(i * (tm // halo) - 1, 0), 0)),
            pl.BlockSpec((1, halo, D_WIDTH), lambda bi, i: (bi, jnp.minimum((i + 1) * (tm // halo), nh - 1), 0)),
            pl.BlockSpec((1, tm, D_WIDTH), row),
            _resident(conv_w.shape),
            _resident(w_out.shape),
        ] + ep_in,
        out_specs=out_specs, out_shape=out_shape,
        compiler_params=_cparams(("parallel", "parallel"), 56),
        name="odd_out",
    )(oc, s_, s_, s_, gb, conv_w, w_out, x, gt, sh, sc, gpost, gffn, wrh, wrl)


def _odd_in_kernel(x_ref, sh_ref, sc_ref, g_ref, win_ref, gv_ref, ws_ref, bs_ref, oc_ref, s_ref, gb_ref):
    tm = x_ref.shape[1]
    hb = (_rms(x_ref[0], g_ref[...]) * (1.0 + sc_ref[0]) + sh_ref[0]).astype(BF16)
    u = jax.nn.gelu(_dot(hb, win_ref[:, :C_WIDTH]))
    v = jax.nn.gelu(_dot(hb, win_ref[:, C_WIDTH:2 * C_WIDTH]))
    gv = gv_ref[...]
    bs = bs_ref[...]
    for hd in range(C_HEADS):
        c0 = hd * C_HEAD_DIM
        vh = _rms(v[:, c0:c0 + C_HEAD_DIM], gv[:, c0:c0 + C_HEAD_DIM]).astype(BF16)
        w = ws_ref[hd]
        bias = bs[:, hd:hd + 1]
        for n in range(tm // CHUNK):
            p0 = n * CHUNK
            sv = _dot(w, vh[p0:p0 + CHUNK, :]) + bias
            oc_ref[0, p0:p0 + CHUNK, c0:c0 + C_HEAD_DIM] = (u[p0:p0 + CHUNK, c0:c0 + C_HEAD_DIM] * sv).astype(BF16)
    o = 2 * C_WIDTH
    hd_ = _dot(hb, win_ref[:, o:o + D_WIDTH])
    gb_ref[0] = _dot(hb, win_ref[:, o + D_WIDTH:o + 2 * D_WIDTH]).astype(BF16)
    gc = _dot(hb, win_ref[:, o + 2 * D_WIDTH:])
    s_ref[0] = (gc * hd_).astype(BF16)


def _odd_in(x, sh, sc, g, win, gv, ws, bs_t):
    b, s, d = x.shape
    tm = ROW_BLOCK
    row = lambda bi, i: (bi, i, 0)
    per_b = lambda bi, i: (bi, 0, 0)
    out = jax.ShapeDtypeStruct((b, s, C_WIDTH), BF16)
    return pl.pallas_call(
        _odd_in_kernel,
        grid=(b, s // tm),
        in_specs=[
            pl.BlockSpec((1, tm, d), row),
            pl.BlockSpec((1, 1, d), per_b),
            pl.BlockSpec((1, 1, d), per_b),
            _resident(g.shape), _resident(win.shape), _resident(gv.shape), _resident(ws.shape),
            _resident(bs_t.shape),
        ],
        out_specs=[pl.BlockSpec((1, tm, C_WIDTH), row)] * 3,
        out_shape=[out, out, out],
        compiler_params=_cparams(("parallel", "parallel"), 56),
        name="odd_in",
    )(x, sh, sc, g, win, gv, ws, bs_t)


def _first_index_of_max(vals, idx, sentinel):
    m = jnp.max(vals, axis=0, keepdims=True)
    first = jnp.min(jnp.where(vals == m, idx, sentinel), axis=0, keepdims=True)
    return m, first


def _route_kernel(lg_ref, b_ref, e_ref, w_ref, c_ref):
    tn = lg_ref.shape[1]
    scores = jax.nn.sigmoid(lg_ref[...])
    sel = scores + b_ref[...]
    row8 = lax.broadcasted_iota(jnp.int32, (8, tn), 0).astype(F32)
    neg = -jnp.inf
    gs = jnp.zeros((N_GROUPS, tn), F32)
    for g in range(N_GROUPS):
        blk = sel[g * GROUP_SIZE:(g + 1) * GROUP_SIZE, :]
        m1, i1 = _first_index_of_max(blk, row8, float(GROUP_SIZE))
        m2 = jnp.max(jnp.where(row8 == i1, neg, blk), axis=0, keepdims=True)
        gs = jnp.where(row8 == float(g), m1 + m2, gs)
    keep = jnp.zeros((N_GROUPS, tn), F32)
    cur = gs
    for _ in range(TOPK_GROUPS):
        _, first = _first_index_of_max(cur, row8, float(N_GROUPS))
        pick = row8 == first
        keep = jnp.where(pick, 1.0, keep)
        cur = jnp.where(pick, neg, cur)
    masked = jnp.concatenate(
        [jnp.where(jnp.max(jnp.where(row8 == float(g), keep, 0.0), axis=0, keepdims=True) > 0.5,
                   sel[g * GROUP_SIZE:(g + 1) * GROUP_SIZE, :], neg) for g in range(N_GROUPS)],
        axis=0)
    eidx = lax.broadcasted_iota(jnp.int32, (N_EXPERTS, tn), 0).astype(F32)
    e_out = jnp.zeros((8, tn), F32)
    w_out = jnp.zeros((8, tn), F32)
    total = jnp.zeros((1, tn), F32)
    picked = jnp.zeros((N_EXPERTS, tn), F32)
    for k in range(TOP_K):
        _, first = _first_index_of_max(masked, eidx, float(N_EXPERTS))
        pick = eidx == first
        wk = jnp.sum(jnp.where(pick, scores, 0.0), axis=0, keepdims=True)
        masked = jnp.where(pick, neg, masked)
        picked = jnp.where(pick, 1.0, picked)
        e_out = jnp.where(row8 == float(k), first, e_out)
        w_out = jnp.where(row8 == float(k), wk, w_out)
        total = total + wk
    e_ref[...] = e_out.astype(jnp.int32)
    w_ref[...] = w_out / total * ROUTED_SCALE
    c_ref[...] = jnp.broadcast_to(jnp.sum(picked, axis=1, keepdims=True), c_ref.shape).astype(jnp.int32)


def _route(logits_t, bias):
    e, t = logits_t.shape
    tn = ROUTE_BLOCK
    nblk = t // tn
    top_e, wts, cnt = pl.pallas_call(
        _route_kernel,
        grid=(nblk,),
        in_specs=[pl.BlockSpec((e, tn), lambda i: (0, i)), pl.BlockSpec((e, 1), lambda i: (0, 0))],
        out_specs=[pl.BlockSpec((8, tn), lambda i: (0, i))] * 2 + [pl.BlockSpec((e, LANES), lambda i: (0, i))],
        out_shape=[jax.ShapeDtypeStruct((8, t), jnp.int32), jax.ShapeDtypeStruct((8, t), F32),
                   jax.ShapeDtypeStruct((e, nblk * LANES), jnp.int32)],
        compiler_params=_cparams(("parallel",), 32),
        name="route",
    )(logits_t, bias.reshape(e, 1))
    counts = jnp.sum(cnt.reshape(e, nblk, LANES)[:, :, 0], axis=1)
    return top_e, wts, counts


def _dispatch_tables(top_e, counts, n_tok):
    bm = EXPERT_BLOCK
    n_asg = n_tok * TOP_K
    n_blk = n_asg // bm
    n_steps = n_blk + N_EXPERTS
    i32 = jnp.int32
    flat_e = top_e[:TOP_K].T.reshape(n_asg)
    order = jnp.argsort(flat_e, stable=True).astype(i32)
    tok = order // TOP_K
    k = order - tok * TOP_K
    src = jnp.concatenate([tok * TILE_ROWS, jnp.zeros((bm,), i32)]).reshape(n_blk + 1, 1, bm)
    spare = (n_asg + jnp.arange(bm, dtype=i32)) * TILE_ROWS
    dst = jnp.concatenate([spare, (k * n_tok + tok) * TILE_ROWS]).reshape(n_blk + 1, 1, bm)

    experts = jnp.arange(N_EXPERTS, dtype=i32)
    ends = jnp.cumsum(counts).astype(i32)
    starts = ends - counts
    first_blk = starts // bm
    n_e_blk = jnp.where(counts > 0, (ends - 1) // bm - first_blk + 1, 0)
    item_end = jnp.cumsum(n_e_blk).astype(i32)
    item_start = item_end - n_e_blk
    n_items = item_end[-1]
    e_last = jnp.max(jnp.where(counts > 0, experts, 0))
    w = jnp.arange(n_steps, dtype=i32)
    active = w < n_items
    e_w = jnp.where(active, jnp.sum((item_end[None, :] <= w[:, None]).astype(i32), axis=1), e_last)
    onehot = e_w[:, None] == experts[None, :]
    of_item = lambda per_expert: jnp.sum(jnp.where(onehot, per_expert[None, :], 0), axis=1)
    blk_w = jnp.where(active, of_item(first_blk) + w - of_item(item_start), n_blk)
    lo_w = jnp.clip(of_item(starts) - blk_w * bm, 0, bm)
    hi_w = jnp.clip(of_item(ends) - blk_w * bm, 0, bm)
    neg1 = jnp.full((1,), -1, i32)
    first_of_blk = (blk_w != jnp.concatenate([neg1, blk_w[:-1]])).astype(i32)
    first_of_e = ((e_w != jnp.concatenate([neg1, e_w[:-1]])) & active).astype(i32)
    cand = jnp.where(counts > 0, experts, N_EXPERTS)
    suffix_min = lax.cummin(cand[::-1])[::-1]
    nxt = jnp.concatenate([suffix_min[1:], jnp.full((1,), N_EXPERTS, i32)])
    nxt = jnp.where(nxt >= N_EXPERTS, -1, nxt)
    return src, dst, (blk_w, e_w, lo_w, hi_w, first_of_blk, first_of_e, of_item(nxt), n_items.reshape(1))


FF_CHUNK = 256
DOWN_CHUNK = 512


def _swiglu_tiles(x_ref, m, wg_ref, wu_ref, wd_ref, between=lambda: None):
    x_lo, x_hi = _unpack_token_tiles(x_ref, m)
    hs = []
    for c in range(FF_PAD // FF_CHUNK):
        cols = slice(c * FF_CHUNK, (c + 1) * FF_CHUNK)
        g = _dot(x_lo, wg_ref[:HALF, cols]) + _dot(x_hi, wg_ref[HALF:, cols])
        between()
        u = _dot(x_lo, wu_ref[:HALF, cols]) + _dot(x_hi, wu_ref[HALF:, cols])
        between()
        hs.append((_silu(g) * u).astype(BF16))
    h = jnp.concatenate(hs, axis=1)
    ys = []
    for n in range(D_MODEL // DOWN_CHUNK):
        ys.append(_dot(h, wd_ref[:, n * DOWN_CHUNK:(n + 1) * DOWN_CHUNK]))
        between()
    return ys


def _packed_chunks(ys):
    per = DOWN_CHUNK // LANES
    half_chunks = HALF // DOWN_CHUNK
    out = []
    for s in range(TILE_ROWS):
        c0 = (s % per) * LANES
        lo = ys[s // per][:, c0:c0 + LANES]
        hi = ys[half_chunks + s // per][:, c0:c0 + LANES]
        out.append(_pack_halves(lo, hi))
    return out


N_DMA_SLOTS = (FF_PAD // FF_CHUNK) * 2 + D_MODEL // DOWN_CHUNK
GATHER_DMA_PRIORITY = 0
OTHER_DMA_PRIORITY = 1


def _expert_kernel(blk_ref, exp_ref, lo_ref, hi_ref, fb_ref, fe_ref, nxt_ref, ni_ref,
                   src0_ref, src1_ref, dst_ref, hx_hbm, wg_hbm, wu_hbm, wd_hbm,
                   ys_hbm,
                   xbuf0, xbuf1, ybuf0, ybuf1, sg, su, sd, wg, wu, wd, gsem, ssem, wsem, *, layer, n_blk):
    bm = EXPERT_BLOCK
    w = pl.program_id(0)
    n_items = ni_ref[0]
    blk = blk_ref[w]
    expert = exp_ref[w]
    active = w < n_items
    xbufs = (xbuf0, xbuf1)
    ybufs = (ybuf0, ybuf1)
    tile = lambda r: pl.ds(r * TILE_ROWS, TILE_ROWS)

    def weight_copies(e):
        return (pltpu.make_async_copy(wg_hbm.at[layer, e], sg.at[pl.ds(0, EXPERT_FF), :], wsem.at[0]),
                pltpu.make_async_copy(wu_hbm.at[layer, e], su.at[pl.ds(0, EXPERT_FF), :], wsem.at[1]),
                pltpu.make_async_copy(wd_hbm.at[layer, e], sd, wsem.at[2]))

    def gather_starts(src_ref, p):
        def one(r):
            row = pl.multiple_of(src_ref[0, 0, r], TILE_ROWS)
            pltpu.make_async_copy(hx_hbm.at[pl.ds(row, TILE_ROWS), :], xbufs[p].at[tile(r), :],
                                  gsem.at[p]).start(priority=GATHER_DMA_PRIORITY)
        return [functools.partial(one, r) for r in range(bm)]

    def scatter_starts(p):
        def one(r):
            row = pl.multiple_of(dst_ref[0, 0, r], TILE_ROWS)
            pltpu.make_async_copy(ybufs[p].at[tile(r), :], ys_hbm.at[pl.ds(row, TILE_ROWS), :],
                                  ssem.at[0]).start(priority=OTHER_DMA_PRIORITY)
        return [functools.partial(one, r) for r in range(bm)]

    def wait_gather(p):
        pltpu.make_async_copy(hx_hbm.at[pl.ds(0, bm * TILE_ROWS), :], xbufs[p], gsem.at[p]).wait()

    def wait_scatter(p):
        pltpu.make_async_copy(ybufs[p], ys_hbm.at[pl.ds(0, bm * TILE_ROWS), :], ssem.at[0]).wait()

    @pl.when(w == 0)
    def _():
        sg[EXPERT_FF:, :] = jnp.zeros((FF_PAD - EXPERT_FF, D_MODEL), F32)
        su[EXPERT_FF:, :] = jnp.zeros((FF_PAD - EXPERT_FF, D_MODEL), F32)
        wd[EXPERT_FF:, :] = jnp.zeros((FF_PAD - EXPERT_FF, D_MODEL), BF16)
        ybuf1[...] = jnp.zeros(ybuf1.shape, U32)
        for c in weight_copies(expert):
            c.start(priority=OTHER_DMA_PRIORITY)
        for start in gather_starts(src0_ref, 0):
            start()

    @pl.when(jnp.logical_and(active, fe_ref[w] == 1))
    def _():
        for c in weight_copies(expert):
            c.wait()
        for c in range(FF_PAD // LANES):
            cols = slice(c * LANES, (c + 1) * LANES)
            wg[:, cols] = sg[cols, :].T.astype(BF16)
            wu[:, cols] = su[cols, :].T.astype(BF16)
        rows_d = 64

        def cast_down(i, carry):
            r = pl.ds(pl.multiple_of(i * rows_d, rows_d), rows_d)
            wd[r, :] = sd[r, :].astype(BF16)
            return carry

        lax.fori_loop(0, EXPERT_FF // rows_d, cast_down, 0)

        @pl.when(nxt_ref[w] >= 0)
        def _():
            for c in weight_copies(nxt_ref[w]):
                c.start(priority=OTHER_DMA_PRIORITY)

    def first_item_of_block(p):
        @pl.when(blk > 0)
        def _():
            wait_scatter(p)
        wait_gather(p)
        pending = gather_starts(src1_ref, 1 - p) + scatter_starts(1 - p)
        per_slot = -(-len(pending) // N_DMA_SLOTS)

        def between():
            for start in pending[:per_slot]:
                start()
            del pending[:per_slot]

        ys = _swiglu_tiles(xbufs[p], bm, wg, wu, wd, between)
        assert not pending
        for s, chunk in enumerate(_packed_chunks(ys)):
            ybufs[p][pl.ds(s, bm, stride=TILE_ROWS), :] = chunk

    def later_item_of_block(p):
        ys = _swiglu_tiles(xbufs[p], bm, wg, wu, wd)
        row = lax.broadcasted_iota(jnp.int32, (bm, 1), 0)
        mine = jnp.logical_and(row >= lo_ref[w], row < hi_ref[w])
        for s, chunk in enumerate(_packed_chunks(ys)):
            idx = pl.ds(s, bm, stride=TILE_ROWS)
            ybufs[p][idx, :] = jnp.where(mine, chunk, ybufs[p][idx, :])

    parity = lax.rem(blk, 2)
    first = fb_ref[w] == 1
    for p in range(2):
        pl.when(jnp.logical_and(active, jnp.logical_and(first, parity == p)))(
            functools.partial(first_item_of_block, p))
        pl.when(jnp.logical_and(active, jnp.logical_and(jnp.logical_not(first), parity == p)))(
            functools.partial(later_item_of_block, p))

    @pl.when(w == n_items)
    def _():
        last = (n_blk - 1) % 2
        wait_scatter(1 - last)
        wait_gather(1 - last)
        for start in scatter_starts(last):
            start()
        wait_scatter(last)


def _experts(hx, src, dst, items, w_gate, w_up, w_down, layer):
    bm = EXPERT_BLOCK
    n_blk = src.shape[0] - 1
    n_steps = items[0].shape[0]
    n_out_rows = (n_blk + 1) * bm * TILE_ROWS
    smem = functools.partial(pl.BlockSpec, (1, 1, bm), memory_space=pltpu.SMEM)
    grid_spec = pltpu.PrefetchScalarGridSpec(
        num_scalar_prefetch=len(items),
        grid=(n_steps,),
        in_specs=[
            smem(index_map=lambda w, *_: (0, 0, 0)),
            smem(index_map=lambda w, blk, *_: (jnp.minimum(blk[w] + 1, n_blk), 0, 0)),
            smem(index_map=lambda w, blk, *_: (blk[w], 0, 0)),
            pl.BlockSpec(memory_space=pl.ANY),
            pl.BlockSpec(memory_space=pl.ANY),
            pl.BlockSpec(memory_space=pl.ANY),
            pl.BlockSpec(memory_space=pl.ANY),
        ],
        out_specs=pl.BlockSpec(memory_space=pl.ANY),
        scratch_shapes=[
            pltpu.VMEM((bm * TILE_ROWS, LANES), U32), pltpu.VMEM((bm * TILE_ROWS, LANES), U32),
            pltpu.VMEM((bm * TILE_ROWS, LANES), U32), pltpu.VMEM((bm * TILE_ROWS, LANES), U32),
            pltpu.VMEM((FF_PAD, D_MODEL), F32), pltpu.VMEM((FF_PAD, D_MODEL), F32),
            pltpu.VMEM((EXPERT_FF, D_MODEL), F32),
            pltpu.VMEM((D_MODEL, FF_PAD), BF16), pltpu.VMEM((D_MODEL, FF_PAD), BF16),
            pltpu.VMEM((FF_PAD, D_MODEL), BF16),
            pltpu.SemaphoreType.DMA((2,)),
            pltpu.SemaphoreType.DMA((1,)),
            pltpu.SemaphoreType.DMA((3,)),
        ],
    )
    return pl.pallas_call(
        functools.partial(_expert_kernel, layer=layer, n_blk=n_blk),
        grid_spec=grid_spec,
        out_shape=jax.ShapeDtypeStruct((n_out_rows, LANES), U32),
        compiler_params=_cparams(("arbitrary",), 48),
        name="experts",
    )(*items, src, src, dst, hx, w_gate, w_up, w_down)


def _combine_kernel(hx_ref, y0, y1, y2, y3, y4, y5, wt_ref, wg_ref, wu_ref, wd_ref, x_ref, gt_ref, g_ref, o_ref,
                    acc_ref):
    tm = x_ref.shape[1]
    y_refs = (y0, y1, y2, y3, y4, y5)
    wt = wt_ref[...]
    wk = [jnp.broadcast_to(wt[:, k:k + 1], (tm, LANES)) for k in range(TOP_K)]
    todo = list(range(TILE_ROWS))

    def combine_one_chunk():
        if not todo:
            return
        s = todo.pop(0)
        idx = pl.ds(s, tm, stride=TILE_ROWS)
        acc_lo = acc_hi = None
        for k, y_ref in enumerate(y_refs):
            lo, hi = _unpack_halves(y_ref[idx, :])
            acc_lo = wk[k] * lo if k == 0 else acc_lo + wk[k] * lo
            acc_hi = wk[k] * hi if k == 0 else acc_hi + wk[k] * hi
        acc_ref[:, s * LANES:(s + 1) * LANES] = acc_lo
        acc_ref[:, HALF + s * LANES:HALF + (s + 1) * LANES] = acc_hi

    shared = jnp.concatenate(_swiglu_tiles(hx_ref, tm, wg_ref, wu_ref, wd_ref, combine_one_chunk), axis=1)
    while todo:
        combine_one_chunk()
    ff = acc_ref[...] + shared
    o_ref[0] = x_ref[0] + gt_ref[0] * _rms(ff, g_ref[...])


def _combine(hx, ys, wts_t, wg, wu, wd, x, gt, g):
    b, s, d = x.shape
    tm = ROW_BLOCK
    nblk = s // tm
    n_tok = b * s
    row = lambda bi, i: (bi, i, 0)
    tiles = (tm * TILE_ROWS, LANES)
    y_specs = [pl.BlockSpec(tiles, functools.partial(lambda bi, i, k: (k * (n_tok // tm) + bi * nblk + i, 0), k=k))
               for k in range(TOP_K)]
    return pl.pallas_call(
        _combine_kernel,
        grid=(b, nblk),
        in_specs=[pl.BlockSpec(tiles, lambda bi, i: (bi * nblk + i, 0))] + y_specs + [
            pl.BlockSpec((tm, 8), lambda bi, i: (bi * nblk + i, 0)),
            _resident(wg.shape), _resident(wu.shape), _resident(wd.shape),
            pl.BlockSpec((1, tm, d), row),
            pl.BlockSpec((1, 1, d), lambda bi, i: (bi, 0, 0)),
            _resident(g.shape),
        ],
        out_specs=pl.BlockSpec((1, tm, d), row),
        out_shape=jax.ShapeDtypeStruct((b, s, d), F32),
        scratch_shapes=[pltpu.VMEM((tm, d), F32)],
        compiler_params=_cparams(("parallel", "parallel"), 48),
        name="combine",
    )(hx, ys, ys, ys, ys, ys, ys, wts_t, wg, wu, wd, x, gt, g)


def _rope_tables(seq):
    t = jnp.arange(seq, dtype=jnp.int32)
    row = (t // GRID_W).astype(F32)
    col = (t % GRID_W).astype(F32)
    per_axis = A_QK_ROPE // 4
    inv = ROPE_BASE ** (-jnp.arange(per_axis, dtype=F32) / per_axis)
    ang = jnp.concatenate([row[:, None] * inv, col[:, None] * inv], axis=-1)
    cos = jnp.cos(ang)
    sin = jnp.sin(ang)
    cos64 = jnp.concatenate([cos, cos], axis=-1)
    sin64 = jnp.concatenate([-sin, sin], axis=-1)
    return jnp.tile(cos64, (1, A_HEADS)), jnp.tile(sin64, (1, A_HEADS))


def _dft_tables(n):
    k = jnp.arange(n, dtype=jnp.int32)
    ang = ((k[:, None] * k[None, :]) % n).astype(F32) * (2.0 * jnp.pi / n)
    scale = n ** -0.5
    return jnp.cos(ang) * scale, -jnp.sin(ang) * scale


def _swap_halves(w):
    half = w.shape[-1] // 2
    return jnp.concatenate([w[..., half:], w[..., :half]], axis=-1)


def _pad_ff(w, axis):
    pad = [(0, 0)] * w.ndim
    pad[axis] = (0, FF_PAD - EXPERT_FF)
    return jnp.pad(w.astype(BF16), pad)


def _router_parts(w_router):
    w = jnp.pad(w_router, ((0, 0), (0, LANES - N_EXPERTS)))
    hi = w.astype(BF16)
    lo = (w - hi.astype(F32)).astype(BF16)
    return hi, lo


def _moe(hx, logits_t, layer, b_router, w_gate, w_up, w_down, sg, su, sd, x, gt, g_post):
    b, s, d = x.shape
    n_tok = b * s
    top_e, wts, counts = _route(logits_t, b_router)
    src, dst, items = _dispatch_tables(top_e, counts, n_tok)
    ys = _experts(hx, src, dst, items, jnp.swapaxes(w_gate, 2, 3), jnp.swapaxes(w_up, 2, 3), w_down, layer)
    return _combine(hx, ys, wts.T, _pad_ff(sg, 1), _pad_ff(su, 1), _pad_ff(sd, 0), x, gt, g_post)


def kernel(x, c, ctx, c_ctx, mod_w, mod_b, norm_g, a_w_in, a_g_q, a_w_uq, a_g_kv, a_w_ukv, a_w_out, o_w_in, o_g_v, o_w_s, o_b_s, o_conv_w, o_w_out, moe_w_router, moe_b_router, moe_w_gate, moe_w_up, moe_w_down, sh_w_gate, sh_w_up, sh_w_down):
    b, s, d = x.shape
    lc = ctx.shape[1]

    cond = jnp.concatenate([c, c_ctx[None, :], jnp.zeros((COND_ROWS - b - 1, d), F32)], axis=0)
    mod = _adaln(cond, mod_w, mod_b)

    def mod_rows(layer, rows):
        m = mod[layer, rows]
        return [m[:, None, j * d:(j + 1) * d] for j in range(6)]

    def moe(layer, hx, lg, x, gt, g_post):
        return _moe(hx, lg, layer, moe_b_router[layer], moe_w_gate, moe_w_up, moe_w_down,
                    sh_w_gate[layer], sh_w_up[layer], sh_w_down[layer], x, gt, g_post)

    sh1, sc1, gt1, sh2, sc2, gt2 = mod_rows(0, slice(0, b))
    csh1, csc1 = [jnp.broadcast_to(m, (b, 1, d)) for m in mod_rows(0, slice(b, b + 1))[:2]]
    g = norm_g[0]
    w_in = a_w_in[0]
    o = A_Q_RANK + A_KV_RANK
    k_pe_w = w_in[:, o:o + A_QK_ROPE]
    win = jnp.concatenate([w_in[:, :o], w_in[:, o + A_QK_ROPE:], k_pe_w, _swap_halves(k_pe_w)], axis=1).astype(BF16)
    wq = a_w_uq[0].reshape(A_Q_RANK, A_HEADS, A_QK_NOPE + A_QK_ROPE)
    wq_pe = wq[:, :, A_QK_NOPE:]
    wuq = jnp.concatenate([wq[:, :, :A_QK_NOPE].reshape(A_Q_RANK, NOPE_W), wq_pe.reshape(A_Q_RANK, ROPE_W),
                           _swap_halves(wq_pe).reshape(A_Q_RANK, ROPE_W)], axis=1).astype(BF16)
    wkv = a_w_ukv[0].reshape(A_KV_RANK, A_HEADS, A_QK_NOPE + A_V_DIM)
    wukv = jnp.concatenate([wkv[:, :, :A_QK_NOPE].reshape(A_KV_RANK, NOPE_W),
                            wkv[:, :, A_QK_NOPE:].reshape(A_KV_RANK, A_WIDTH)], axis=1).astype(BF16)
    gq = a_g_q[0][None, :]
    gkv = a_g_kv[0][None, :]
    cos, sin = _rope_tables(s)
    cc, cs = _dft_tables(B_GROUP_DIM)
    eye = jnp.eye(B_GROUPS, dtype=F32)
    bdc = jnp.kron(eye, cc).astype(BF16)
    bds = jnp.kron(eye, -cs).astype(BF16)
    a_c, a_s = _dft_tables(s)
    a_c = a_c.astype(BF16)
    a_s = a_s.astype(BF16)

    ones = jnp.ones((lc, ROPE_W), F32)
    _, kc, vc, _, _ = _even_in(ctx, csh1, csc1, g[0][None, :], win, gq, wuq, gkv, wukv, ones, jnp.zeros_like(ones),
                               bdc, bds)
    q, ko, vo, uc, us = _even_in(x, sh1, sc1, g[0][None, :], win, gq, wuq, gkv, wukv, cos, sin, bdc, bds)
    o_a = _attention(q, kc, ko, vc, vo)
    o_b = _fourier(a_c, a_s, uc, us)
    wrh, wrl = _router_parts(moe_w_router[0])
    x, hx, lg = _even_out(o_a, o_b, a_w_out[0].astype(BF16), x, gt1, sh2, sc2, g[1][None, :], g[2][None, :], wrh, wrl)
    x = moe(0, hx, lg, x, gt2, g[3][None, :])

    sh1, sc1, gt1, sh2, sc2, gt2 = mod_rows(1, slice(0, b))
    g = norm_g[1]
    oc, s_, gb = _odd_in(x, sh1, sc1, g[0][None, :], o_w_in[0].astype(BF16), o_g_v[0][None, :],
                         o_w_s[0].astype(BF16), o_b_s[0].T)
    wrh, wrl = _router_parts(moe_w_router[1])
    x, hx, lg = _odd_out(oc, s_, gb, o_conv_w[0], o_w_out[0].astype(BF16), x, gt1, sh2, sc2, g[1][None, :],
                         g[2][None, :], wrh, wrl)
    x = moe(1, hx, lg, x, gt2, g[3][None, :])
    return x
```

```python
import functools

import jax
import jax.numpy as jnp
from jax import lax
from jax.experimental import pallas as pl
from jax.experimental.pallas import tpu as pltpu

F32 = jnp.float32
BF16 = jnp.bfloat16
U32 = jnp.uint32

D_MODEL = 2048
GRID_W = 64
EPS = 1e-6
A_HEADS = 12
A_QK_NOPE = 128
A_QK_ROPE = 64
A_V_DIM = 128
A_Q_RANK = 768
A_KV_RANK = 512
ROPE_BASE = 10000.0
B_GROUPS = 4
B_GROUP_DIM = 128
B_WIDTH = B_GROUPS * B_GROUP_DIM
A_WIDTH = A_HEADS * A_V_DIM
C_HEADS = 8
C_HEAD_DIM = 128
CHUNK = 128
C_WIDTH = C_HEADS * C_HEAD_DIM
D_WIDTH = 1024
N_EXPERTS = 64
TOP_K = 6
N_GROUPS = 8
GROUP_SIZE = N_EXPERTS // N_GROUPS
TOPK_GROUPS = 4
EXPERT_FF = 704
ROUTED_SCALE = 2.5

LANES = 128
VMEM_BYTES_V7X = 64 * 1024 * 1024
FF_PAD = 768
HALF = D_MODEL // 2
TILE_ROWS = 8
QK_PAD = 256
V_PAD = 256
LOG2_E = 1.4426950408889634
COND_ROWS = 32
ROW_BLOCK = 256
OUT_BLOCK = 512
OUT_SUB_BLOCK = 256
EXPERT_BLOCK = 256
ROUTE_BLOCK = 512


def _cparams(semantics, vmem_mib):
    assert vmem_mib * 1024 * 1024 < VMEM_BYTES_V7X
    return pltpu.CompilerParams(dimension_semantics=semantics, vmem_limit_bytes=vmem_mib * 1024 * 1024)


def _resident(shape):
    nd = len(shape)
    return pl.BlockSpec(shape, lambda *_: (0,) * nd, pipeline_mode=pl.Buffered(1))


def _dot(a, b):
    return jnp.dot(a, b, preferred_element_type=F32)


def _dot_nt(a, b):
    return lax.dot_general(a, b, (((1,), (1,)), ((), ())), preferred_element_type=F32)


def _rms(xf, g):
    return xf * lax.rsqrt(jnp.mean(xf * xf, axis=-1, keepdims=True) + EPS) * g


def _split_bf16(a):
    hi = a.astype(BF16)
    lo = (a - hi.astype(F32)).astype(BF16)
    return hi, lo


def _pack_halves(lo_f32, hi_f32):
    lo = lax.bitcast_convert_type(lo_f32.astype(BF16).astype(F32), U32) >> 16
    hi = lax.bitcast_convert_type(hi_f32.astype(BF16).astype(F32), U32) & jnp.uint32(0xFFFF0000)
    return lo | hi


def _unpack_halves(w):
    lo = lax.bitcast_convert_type(w << 16, F32)
    hi = lax.bitcast_convert_type(w & jnp.uint32(0xFFFF0000), F32)
    return lo, hi


def _store_token_tiles(ref, packed, first_row=0):
    m = packed.shape[0]
    for s in range(TILE_ROWS):
        ref[pl.ds(first_row * TILE_ROWS + s, m, stride=TILE_ROWS), :] = packed[:, s * LANES:(s + 1) * LANES]


def _load_token_tiles(ref, m):
    return [ref[pl.ds(s, m, stride=TILE_ROWS), :] for s in range(TILE_ROWS)]


def _unpack_token_tiles(ref, m):
    los, his = [], []
    for w in _load_token_tiles(ref, m):
        lo, hi = _unpack_halves(w)
        los.append(lo.astype(BF16))
        his.append(hi.astype(BF16))
    return jnp.concatenate(los, axis=1), jnp.concatenate(his, axis=1)


def _silu(x):
    return x * jax.nn.sigmoid(x)


def _adaln_kernel(c_ref, w_ref, b_ref, o_ref):
    a_hi, a_lo = _split_bf16(_silu(c_ref[...]))
    w_hi, w_lo = _split_bf16(w_ref[0])
    o_ref[0] = _dot(a_hi, w_hi) + _dot(a_lo, w_hi) + _dot(a_hi, w_lo) + b_ref[0]


def _adaln(cond, mod_w, mod_b):
    depth, d, n = mod_w.shape
    tn = 512
    return pl.pallas_call(
        _adaln_kernel,
        grid=(depth, n // tn),
        in_specs=[
            pl.BlockSpec((COND_ROWS, d), lambda l, j: (0, 0)),
            pl.BlockSpec((1, d, tn), lambda l, j: (l, 0, j)),
            pl.BlockSpec((1, 1, tn), lambda l, j: (l, 0, j)),
        ],
        out_specs=pl.BlockSpec((1, COND_ROWS, tn), lambda l, j: (l, 0, j)),
        out_shape=jax.ShapeDtypeStruct((depth, COND_ROWS, n), F32),
        compiler_params=_cparams(("parallel", "parallel"), 32),
        name="adaln",
    )(cond, mod_w, mod_b.reshape(depth, 1, n))


NOPE_W = A_HEADS * A_QK_NOPE
ROPE_W = A_HEADS * A_QK_ROPE
EVEN_IN_PAD = A_Q_RANK + A_KV_RANK + B_WIDTH + 2 * A_QK_ROPE


def _even_in_kernel(x_ref, sh_ref, sc_ref, g_ref, win_ref, gq_ref, wuq_ref, gkv_ref, wukv_ref,
                    cos_ref, sin_ref, bdc_ref, bds_ref, q_ref, k_ref, v_ref, uc_ref, us_ref):
    tm = x_ref.shape[1]
    h = _rms(x_ref[0], g_ref[...]) * (1.0 + sc_ref[0]) + sh_ref[0]
    z = _dot(h.astype(BF16), win_ref[...])
    c_q = z[:, :A_Q_RANK]
    c_kv = z[:, A_Q_RANK:A_Q_RANK + A_KV_RANK]
    o = A_Q_RANK + A_KV_RANK
    u_f = z[:, o:o + B_WIDTH].astype(BF16)
    k_pe = z[:, o + B_WIDTH:o + B_WIDTH + A_QK_ROPE]
    k_pe_sw = z[:, o + B_WIDTH + A_QK_ROPE:]
    uc_ref[0] = _dot(u_f, bdc_ref[...]).astype(BF16)
    us_ref[0] = _dot(u_f, bds_ref[...]).astype(BF16)
    cos = cos_ref[...]
    sin = sin_ref[...]
    scale = (A_QK_NOPE + A_QK_ROPE) ** -0.5 * LOG2_E

    q = _dot(_rms(c_q, gq_ref[...]).astype(BF16), wuq_ref[...]) * scale
    q_pe = q[:, NOPE_W:NOPE_W + ROPE_W] * cos + q[:, NOPE_W + ROPE_W:] * sin
    kv = _dot(_rms(c_kv, gkv_ref[...]).astype(BF16), wukv_ref[...])
    k_pe_r = (k_pe * cos[:, :A_QK_ROPE] + k_pe_sw * sin[:, :A_QK_ROPE]).astype(BF16)
    zeros = jnp.zeros((tm, QK_PAD - A_QK_NOPE - A_QK_ROPE), BF16)
    ones_col = (lax.broadcasted_iota(jnp.int32, (tm, V_PAD - A_V_DIM), 1) == 0).astype(BF16)
    for hd in range(A_HEADS):
        n0 = hd * A_QK_NOPE
        r0 = hd * A_QK_ROPE
        q_ref[0, hd, :, :A_QK_NOPE] = q[:, n0:n0 + A_QK_NOPE].astype(BF16)
        q_ref[0, hd, :, A_QK_NOPE:A_QK_NOPE + A_QK_ROPE] = q_pe[:, r0:r0 + A_QK_ROPE].astype(BF16)
        q_ref[0, hd, :, A_QK_NOPE + A_QK_ROPE:] = zeros
        k_ref[0, hd, :, :A_QK_NOPE] = kv[:, n0:n0 + A_QK_NOPE].astype(BF16)
        k_ref[0, hd, :, A_QK_NOPE:A_QK_NOPE + A_QK_ROPE] = k_pe_r
        k_ref[0, hd, :, A_QK_NOPE + A_QK_ROPE:] = zeros
        v_ref[0, hd, :, :A_V_DIM] = kv[:, NOPE_W + n0:NOPE_W + n0 + A_V_DIM].astype(BF16)
        v_ref[0, hd, :, A_V_DIM:] = ones_col


def _even_in(x, sh, sc, g, win, gq, wuq, gkv, wukv, cos, sin, bdc, bds):
    b, l, d = x.shape
    tm = ROW_BLOCK
    row = lambda bi, i: (bi, i, 0)
    per_b = lambda bi, i: (bi, 0, 0)
    heads = lambda bi, i: (bi, 0, i, 0)
    return pl.pallas_call(
        _even_in_kernel,
        grid=(b, l // tm),
        in_specs=[
            pl.BlockSpec((1, tm, d), row),
            pl.BlockSpec((1, 1, d), per_b),
            pl.BlockSpec((1, 1, d), per_b),
            _resident(g.shape), _resident(win.shape), _resident(gq.shape), _resident(wuq.shape),
            _resident(gkv.shape), _resident(wukv.shape),
            pl.BlockSpec((tm, ROPE_W), lambda bi, i: (i, 0)),
            pl.BlockSpec((tm, ROPE_W), lambda bi, i: (i, 0)),
            _resident(bdc.shape), _resident(bds.shape),
        ],
        out_specs=[
            pl.BlockSpec((1, A_HEADS, tm, QK_PAD), heads),
            pl.BlockSpec((1, A_HEADS, tm, QK_PAD), heads),
            pl.BlockSpec((1, A_HEADS, tm, V_PAD), heads),
            pl.BlockSpec((1, tm, B_WIDTH), row),
            pl.BlockSpec((1, tm, B_WIDTH), row),
        ],
        out_shape=[
            jax.ShapeDtypeStruct((b, A_HEADS, l, QK_PAD), BF16),
            jax.ShapeDtypeStruct((b, A_HEADS, l, QK_PAD), BF16),
            jax.ShapeDtypeStruct((b, A_HEADS, l, V_PAD), BF16),
            jax.ShapeDtypeStruct((b, l, B_WIDTH), BF16),
            jax.ShapeDtypeStruct((b, l, B_WIDTH), BF16),
        ],
        compiler_params=_cparams(("parallel", "parallel"), 56),
        name="even_in",
    )(x, sh, sc, g, win, gq, wuq, gkv, wukv, cos, sin, bdc, bds)


ATTN_Q_BLOCK = 512


def _attn_kernel(q_ref, kc_ref, ko_ref, vc_ref, vo_ref, o_ref):
    kc = kc_ref[0, 0]
    ko = ko_ref[0, 0]
    vc = vc_ref[0, 0]
    vo = vo_ref[0, 0]
    tq = ATTN_Q_BLOCK
    n_chunks = q_ref.shape[2] // tq

    def scores(i):
        q = q_ref[0, 0, i * tq:(i + 1) * tq, :]
        return _dot_nt(q, kc), _dot_nt(q, ko)

    nxt = scores(0)
    for i in range(n_chunks):
        rows = slice(i * tq, (i + 1) * tq)
        s_c, s_o = nxt
        if i + 1 < n_chunks:
            nxt = scores(i + 1)
        m = jnp.maximum(jnp.max(s_c, axis=-1, keepdims=True), jnp.max(s_o, axis=-1, keepdims=True))
        p_c = jnp.exp2(s_c - m).astype(BF16)
        p_o = jnp.exp2(s_o - m).astype(BF16)
        o = _dot(p_c, vc) + _dot(p_o, vo)
        o_ref[0, rows, :] = (o[:, :A_V_DIM] / o[:, A_V_DIM:A_V_DIM + 1]).astype(BF16)


def _attention(q, kc, ko, vc, vo):
    b, h, s, _ = q.shape
    lc = kc.shape[2]
    bh = lambda bi, hi: (bi, hi, 0, 0)
    return pl.pallas_call(
        _attn_kernel,
        grid=(b, h),
        in_specs=[
            pl.BlockSpec((1, 1, s, QK_PAD), bh),
            pl.BlockSpec((1, 1, lc, QK_PAD), bh),
            pl.BlockSpec((1, 1, s, QK_PAD), bh),
            pl.BlockSpec((1, 1, lc, V_PAD), bh),
            pl.BlockSpec((1, 1, s, V_PAD), bh),
        ],
        out_specs=pl.BlockSpec((1, s, A_V_DIM), lambda bi, hi: (bi, 0, hi)),
        out_shape=jax.ShapeDtypeStruct((b, s, h * A_V_DIM), BF16),
        compiler_params=_cparams(("parallel", "parallel"), 48),
        name="attention",
    )(q, kc, ko, vc, vo)


def _fourier_kernel(ac_ref, as_ref, uc_ref, us_ref, o_ref):
    o_ref[0] = (_dot(ac_ref[...], uc_ref[0]) + _dot(as_ref[...], us_ref[0])).astype(BF16)


def _fourier(a_c, a_s, uc, us):
    b, s, w = uc.shape
    tm = 512
    return pl.pallas_call(
        _fourier_kernel,
        grid=(b, s // tm),
        in_specs=[
            pl.BlockSpec((tm, s), lambda bi, i: (i, 0)),
            pl.BlockSpec((tm, s), lambda bi, i: (i, 0)),
            pl.BlockSpec((1, s, w), lambda bi, i: (bi, 0, 0)),
            pl.BlockSpec((1, s, w), lambda bi, i: (bi, 0, 0)),
        ],
        out_specs=pl.BlockSpec((1, tm, w), lambda bi, i: (bi, i, 0)),
        out_shape=jax.ShapeDtypeStruct((b, s, w), BF16),
        compiler_params=_cparams(("parallel", "parallel"), 32),
        name="fourier",
    )(a_c, a_s, uc, us)


def _mix_epilogue(mix, r0, x_ref, gt_ref, sh_ref, sc_ref, gpost_ref, gffn_ref, wrh_ref, wrl_ref,
                  xo_ref, hx_ref, lg_ref):
    m = mix.shape[0]
    rows = slice(r0, r0 + m)
    xn = x_ref[0, rows, :] + gt_ref[0] * _rms(mix, gpost_ref[...])
    xo_ref[0, rows, :] = xn
    t = _rms(xn, gffn_ref[...]) * (1.0 + sc_ref[0]) + sh_ref[0]
    t_hi, t_lo = _split_bf16(t)
    wrh = wrh_ref[...]
    lg = _dot(t_hi, wrh) + _dot(t_lo, wrh) + _dot(t_hi, wrl_ref[...])
    lg_ref[:, rows] = lg.T[:N_EXPERTS, :]
    tq = t_hi.astype(F32)
    _store_token_tiles(hx_ref, _pack_halves(tq[:, :HALF], tq[:, HALF:]), r0)


def _pipelined_sub_blocks(mix, tm, epilogue_refs):
    starts = list(range(0, tm, OUT_SUB_BLOCK))
    nxt = mix(starts[0])
    for j, r0 in enumerate(starts):
        cur = nxt
        if j + 1 < len(starts):
            nxt = mix(starts[j + 1])
        _mix_epilogue(cur, r0, *epilogue_refs)


def _even_out_kernel(oa_ref, ob_ref, w_ref, *rest):
    def mix(r0):
        rows = slice(r0, r0 + OUT_SUB_BLOCK)
        return _dot(oa_ref[0, rows, :], w_ref[:A_WIDTH, :]) + _dot(ob_ref[0, rows, :], w_ref[A_WIDTH:, :])

    _pipelined_sub_blocks(mix, oa_ref.shape[1], rest)


def _odd_out_kernel(oc_ref, s_ref, sp_ref, sn_ref, gb_ref, cw_ref, w_ref, *rest):
    i = pl.program_id(1)
    tm = s_ref.shape[1]
    s = s_ref[0].astype(F32)
    prev_row = jnp.where(i > 0, sp_ref[0, 15:16, :].astype(F32), 0.0)
    next_row = jnp.where(i < pl.num_programs(1) - 1, sn_ref[0, 0:1, :].astype(F32), 0.0)
    row = lax.broadcasted_iota(jnp.int32, (tm, 1), 0)
    s_dn = jnp.where(row == 0, prev_row, pltpu.roll(s, 1, axis=0))
    s_up = jnp.where(row == tm - 1, next_row, pltpu.roll(s, tm - 1, axis=0))
    cw = cw_ref[...]
    conv = cw[0:1, :] * s_dn + cw[1:2, :] * s + cw[2:3, :] * s_up
    od = (gb_ref[0].astype(F32) * conv).astype(BF16)
    def mix(r0):
        rows = slice(r0, r0 + OUT_SUB_BLOCK)
        return _dot(oc_ref[0, rows, :], w_ref[:C_WIDTH, :]) + _dot(od[rows, :], w_ref[C_WIDTH:, :])

    _pipelined_sub_blocks(mix, tm, rest)


def _epilogue_specs(b, s, d, tm):
    per_b = lambda bi, i: (bi, 0, 0)
    row = lambda bi, i: (bi, i, 0)
    nblk = s // tm
    in_specs = [
        pl.BlockSpec((1, tm, d), row),
        pl.BlockSpec((1, 1, d), per_b),
        pl.BlockSpec((1, 1, d), per_b),
        pl.BlockSpec((1, 1, d), per_b),
        _resident((1, d)), _resident((1, d)),
        _resident((d, LANES)), _resident((d, LANES)),
    ]
    out_specs = [
        pl.BlockSpec((1, tm, d), row),
        pl.BlockSpec((tm * TILE_ROWS, LANES), lambda bi, i: (bi * nblk + i, 0)),
        pl.BlockSpec((N_EXPERTS, tm), lambda bi, i: (0, bi * nblk + i)),
    ]
    out_shape = [
        jax.ShapeDtypeStruct((b, s, d), F32),
        jax.ShapeDtypeStruct((b * s * TILE_ROWS, LANES), U32),
        jax.ShapeDtypeStruct((N_EXPERTS, b * s), F32),
    ]
    return in_specs, out_specs, out_shape


def _even_out(oa, ob, w_out, x, gt, sh, sc, gpost, gffn, wrh, wrl):
    b, s, d = x.shape
    tm = OUT_BLOCK
    row = lambda bi, i: (bi, i, 0)
    ep_in, out_specs, out_shape = _epilogue_specs(b, s, d, tm)
    return pl.pallas_call(
        _even_out_kernel,
        grid=(b, s // tm),
        in_specs=[pl.BlockSpec((1, tm, A_WIDTH), row), pl.BlockSpec((1, tm, B_WIDTH), row),
                  _resident(w_out.shape)] + ep_in,
        out_specs=out_specs, out_shape=out_shape,
        compiler_params=_cparams(("parallel", "parallel"), 56),
        name="even_out",
    )(oa, ob, w_out, x, gt, sh, sc, gpost, gffn, wrh, wrl)


def _odd_out(oc, s_, gb, conv_w, w_out, x, gt, sh, sc, gpost, gffn, wrh, wrl):
    b, s, d = x.shape
    tm = OUT_BLOCK
    halo = 16
    nh = s // halo
    row = lambda bi, i: (bi, i, 0)
    ep_in, out_specs, out_shape = _epilogue_specs(b, s, d, tm)
    return pl.pallas_call(
        _odd_out_kernel,
        grid=(b, s // tm),
        in_specs=[
            pl.BlockSpec((1, tm, C_WIDTH), row),
            pl.BlockSpec((1, tm, D_WIDTH), row),
            pl.BlockSpec((1, halo, D_WIDTH), lambda bi, i: (bi, jnp.maximum(i * (tm // halo) - 1, 0), 0)),
            pl.BlockSpec((1, halo, D_WIDTH), lambda bi, i: (bi, jnp.minimum((i + 1) * (tm // halo), nh - 1), 0)),
            pl.BlockSpec((1, tm, D_WIDTH), row),
            _resident(conv_w.shape),
            _resident(w_out.shape),
        ] + ep_in,
        out_specs=out_specs, out_shape=out_shape,
        compiler_params=_cparams(("parallel", "parallel"), 56),
        name="odd_out",
    )(oc, s_, s_, s_, gb, conv_w, w_out, x, gt, sh, sc, gpost, gffn, wrh, wrl)


def _odd_in_kernel(x_ref, sh_ref, sc_ref, g_ref, win_ref, gv_ref, ws_ref, bs_ref, oc_ref, s_ref, gb_ref):
    tm = x_ref.shape[1]
    hb = (_rms(x_ref[0], g_ref[...]) * (1.0 + sc_ref[0]) + sh_ref[0]).astype(BF16)
    o = 2 * C_WIDTH
    u = _dot(hb, win_ref[:, :C_WIDTH])
    v = _dot(hb, win_ref[:, C_WIDTH:o])
    hd_ = _dot(hb, win_ref[:, o:o + D_WIDTH])
    gb_ref[0] = _dot(hb, win_ref[:, o + D_WIDTH:o + 2 * D_WIDTH]).astype(BF16)
    gc = _dot(hb, win_ref[:, o + 2 * D_WIDTH:])
    s_ref[0] = (gc * hd_).astype(BF16)
    u = jax.nn.gelu(u)
    v = jax.nn.gelu(v)
    gv = gv_ref[...]
    bs = bs_ref[...]
    for hd in range(C_HEADS):
        c0 = hd * C_HEAD_DIM
        vh = _rms(v[:, c0:c0 + C_HEAD_DIM], gv[:, c0:c0 + C_HEAD_DIM]).astype(BF16)
        w = ws_ref[hd]
        bias = bs[:, hd:hd + 1]
        for n in range(tm // CHUNK):
            p0 = n * CHUNK
            sv = _dot(w, vh[p0:p0 + CHUNK, :]) + bias
            oc_ref[0, p0:p0 + CHUNK, c0:c0 + C_HEAD_DIM] = (u[p0:p0 + CHUNK, c0:c0 + C_HEAD_DIM] * sv).astype(BF16)


def _odd_in(x, sh, sc, g, win, gv, ws, bs_t):
    b, s, d = x.shape
    tm = ROW_BLOCK
    row = lambda bi, i: (bi, i, 0)
    per_b = lambda bi, i: (bi, 0, 0)
    out = jax.ShapeDtypeStruct((b, s, C_WIDTH), BF16)
    return pl.pallas_call(
        _odd_in_kernel,
        grid=(b, s // tm),
        in_specs=[
            pl.BlockSpec((1, tm, d), row),
            pl.BlockSpec((1, 1, d), per_b),
            pl.BlockSpec((1, 1, d), per_b),
            _resident(g.shape), _resident(win.shape), _resident(gv.shape), _resident(ws.shape),
            _resident(bs_t.shape),
        ],
        out_specs=[pl.BlockSpec((1, tm, C_WIDTH), row)] * 3,
        out_shape=[out, out, out],
        compiler_params=_cparams(("parallel", "parallel"), 56),
        name="odd_in",
    )(x, sh, sc, g, win, gv, ws, bs_t)


def _first_index_of_max(vals, idx, sentinel):
    m = jnp.max(vals, axis=0, keepdims=True)
    first = jnp.min(jnp.where(vals == m, idx, sentinel), axis=0, keepdims=True)
    return m, first


def _route_kernel(lg_ref, b_ref, e_ref, w_ref, c_ref):
    tn = lg_ref.shape[1]
    scores = jax.nn.sigmoid(lg_ref[...])
    sel = scores + b_ref[...]
    row8 = lax.broadcasted_iota(jnp.int32, (8, tn), 0).astype(F32)
    neg = -jnp.inf
    gs = jnp.zeros((N_GROUPS, tn), F32)
    for g in range(N_GROUPS):
        blk = sel[g * GROUP_SIZE:(g + 1) * GROUP_SIZE, :]
        m1, i1 = _first_index_of_max(blk, row8, float(GROUP_SIZE))
        m2 = jnp.max(jnp.where(row8 == i1, neg, blk), axis=0, keepdims=True)
        gs = jnp.where(row8 == float(g), m1 + m2, gs)
    keep = jnp.zeros((N_GROUPS, tn), F32)
    cur = gs
    for _ in range(TOPK_GROUPS):
        _, first = _first_index_of_max(cur, row8, float(N_GROUPS))
        pick = row8 == first
        keep = jnp.where(pick, 1.0, keep)
        cur = jnp.where(pick, neg, cur)
    masked = jnp.concatenate(
        [jnp.where(jnp.max(jnp.where(row8 == float(g), keep, 0.0), axis=0, keepdims=True) > 0.5,
                   sel[g * GROUP_SIZE:(g + 1) * GROUP_SIZE, :], neg) for g in range(N_GROUPS)],
        axis=0)
    eidx = lax.broadcasted_iota(jnp.int32, (N_EXPERTS, tn), 0).astype(F32)
    e_out = jnp.zeros((8, tn), F32)
    w_out = jnp.zeros((8, tn), F32)
    total = jnp.zeros((1, tn), F32)
    picked = jnp.zeros((N_EXPERTS, tn), F32)
    for k in range(TOP_K):
        _, first = _first_index_of_max(masked, eidx, float(N_EXPERTS))
        pick = eidx == first
        wk = jnp.sum(jnp.where(pick, scores, 0.0), axis=0, keepdims=True)
        masked = jnp.where(pick, neg, masked)
        picked = jnp.where(pick, 1.0, picked)
        e_out = jnp.where(row8 == float(k), first, e_out)
        w_out = jnp.where(row8 == float(k), wk, w_out)
        total = total + wk
    e_ref[...] = e_out.astype(jnp.int32)
    w_ref[...] = w_out / total * ROUTED_SCALE
    c_ref[...] = jnp.broadcast_to(jnp.sum(picked, axis=1, keepdims=True), c_ref.shape).astype(jnp.int32)


def _route(logits_t, bias):
    e, t = logits_t.shape
    tn = ROUTE_BLOCK
    nblk = t // tn
    top_e, wts, cnt = pl.pallas_call(
        _route_kernel,
        grid=(nblk,),
        in_specs=[pl.BlockSpec((e, tn), lambda i: (0, i)), pl.BlockSpec((e, 1), lambda i: (0, 0))],
        out_specs=[pl.BlockSpec((8, tn), lambda i: (0, i))] * 2 + [pl.BlockSpec((e, LANES), lambda i: (0, i))],
        out_shape=[jax.ShapeDtypeStruct((8, t), jnp.int32), jax.ShapeDtypeStruct((8, t), F32),
                   jax.ShapeDtypeStruct((e, nblk * LANES), jnp.int32)],
        compiler_params=_cparams(("parallel",), 32),
        name="route",
    )(logits_t, bias.reshape(e, 1))
    counts = jnp.sum(cnt.reshape(e, nblk, LANES)[:, :, 0], axis=1)
    return top_e, wts, counts


def _dispatch_tables(top_e, counts, n_tok):
    bm = EXPERT_BLOCK
    n_asg = n_tok * TOP_K
    n_blk = n_asg // bm
    n_steps = n_blk + N_EXPERTS
    i32 = jnp.int32
    flat_e = top_e[:TOP_K].T.reshape(n_asg)
    order = jnp.argsort(flat_e, stable=True).astype(i32)
    tok = order // TOP_K
    k = order - tok * TOP_K
    src = jnp.concatenate([tok * TILE_ROWS, jnp.zeros((bm,), i32)]).reshape(n_blk + 1, 1, bm)
    spare = (n_asg + jnp.arange(bm, dtype=i32)) * TILE_ROWS
    dst = jnp.concatenate([spare, (k * n_tok + tok) * TILE_ROWS]).reshape(n_blk + 1, 1, bm)

    experts = jnp.arange(N_EXPERTS, dtype=i32)
    ends = jnp.cumsum(counts).astype(i32)
    starts = ends - counts
    first_blk = starts // bm
    n_e_blk = jnp.where(counts > 0, (ends - 1) // bm - first_blk + 1, 0)
    item_end = jnp.cumsum(n_e_blk).astype(i32)
    item_start = item_end - n_e_blk
    n_items = item_end[-1]
    e_last = jnp.max(jnp.where(counts > 0, experts, 0))
    w = jnp.arange(n_steps, dtype=i32)
    active = w < n_items
    e_w = jnp.where(active, jnp.sum((item_end[None, :] <= w[:, None]).astype(i32), axis=1), e_last)
    onehot = e_w[:, None] == experts[None, :]
    of_item = lambda per_expert: jnp.sum(jnp.where(onehot, per_expert[None, :], 0), axis=1)
    blk_w = jnp.where(active, of_item(first_blk) + w - of_item(item_start), n_blk)
    lo_w = jnp.clip(of_item(starts) - blk_w * bm, 0, bm)
    hi_w = jnp.clip(of_item(ends) - blk_w * bm, 0, bm)
    neg1 = jnp.full((1,), -1, i32)
    first_of_blk = (blk_w != jnp.concatenate([neg1, blk_w[:-1]])).astype(i32)
    first_of_e = ((e_w != jnp.concatenate([neg1, e_w[:-1]])) & active).astype(i32)
    cand = jnp.where(counts > 0, experts, N_EXPERTS)
    suffix_min = lax.cummin(cand[::-1])[::-1]
    nxt = jnp.concatenate([suffix_min[1:], jnp.full((1,), N_EXPERTS, i32)])
    nxt = jnp.where(nxt >= N_EXPERTS, -1, nxt)
    return src, dst, (blk_w, e_w, lo_w, hi_w, first_of_blk, first_of_e, of_item(nxt), n_items.reshape(1))


FF_CHUNK = 256
DOWN_CHUNK = 512


def _swiglu_tiles(x_ref, m, wg_ref, wu_ref, wd_ref, between=lambda: None):
    x_lo, x_hi = _unpack_token_tiles(x_ref, m)
    hs = []
    for c in range(FF_PAD // FF_CHUNK):
        cols = slice(c * FF_CHUNK, (c + 1) * FF_CHUNK)
        g = _dot(x_lo, wg_ref[:HALF, cols]) + _dot(x_hi, wg_ref[HALF:, cols])
        between()
        u = _dot(x_lo, wu_ref[:HALF, cols]) + _dot(x_hi, wu_ref[HALF:, cols])
        between()
        hs.append((_silu(g) * u).astype(BF16))
    h = jnp.concatenate(hs, axis=1)
    ys = []
    for n in range(D_MODEL // DOWN_CHUNK):
        ys.append(_dot(h, wd_ref[:, n * DOWN_CHUNK:(n + 1) * DOWN_CHUNK]))
        between()
    return ys


def _packed_chunks(ys):
    per = DOWN_CHUNK // LANES
    half_chunks = HALF // DOWN_CHUNK
    out = []
    for s in range(TILE_ROWS):
        c0 = (s % per) * LANES
        lo = ys[s // per][:, c0:c0 + LANES]
        hi = ys[half_chunks + s // per][:, c0:c0 + LANES]
        out.append(_pack_halves(lo, hi))
    return out


N_DMA_SLOTS = (FF_PAD // FF_CHUNK) * 2 + D_MODEL // DOWN_CHUNK
GATHER_DMA_PRIORITY = 0
OTHER_DMA_PRIORITY = 1


def _expert_kernel(blk_ref, exp_ref, lo_ref, hi_ref, fb_ref, fe_ref, nxt_ref, ni_ref,
                   src0_ref, src1_ref, dst_ref, hx_hbm, wg_hbm, wu_hbm, wd_hbm,
                   ys_hbm,
                   xbuf0, xbuf1, ybuf0, ybuf1, sg, su, sd, wg, wu, wd, gsem, ssem, wsem, *, layer, n_blk):
    bm = EXPERT_BLOCK
    w = pl.program_id(0)
    n_items = ni_ref[0]
    blk = blk_ref[w]
    expert = exp_ref[w]
    active = w < n_items
    xbufs = (xbuf0, xbuf1)
    ybufs = (ybuf0, ybuf1)
    tile = lambda r: pl.ds(r * TILE_ROWS, TILE_ROWS)

    def weight_copies(e):
        return (pltpu.make_async_copy(wg_hbm.at[layer, e], sg.at[pl.ds(0, EXPERT_FF), :], wsem.at[0]),
                pltpu.make_async_copy(wu_hbm.at[layer, e], su.at[pl.ds(0, EXPERT_FF), :], wsem.at[1]),
                pltpu.make_async_copy(wd_hbm.at[layer, e], sd, wsem.at[2]))

    def gather_starts(src_ref, p):
        def one(r):
            row = pl.multiple_of(src_ref[0, 0, r], TILE_ROWS)
            pltpu.make_async_copy(hx_hbm.at[pl.ds(row, TILE_ROWS), :], xbufs[p].at[tile(r), :],
                                  gsem.at[p]).start(priority=GATHER_DMA_PRIORITY)
        return [functools.partial(one, r) for r in range(bm)]

    def scatter_starts(p):
        def one(r):
            row = pl.multiple_of(dst_ref[0, 0, r], TILE_ROWS)
            pltpu.make_async_copy(ybufs[p].at[tile(r), :], ys_hbm.at[pl.ds(row, TILE_ROWS), :],
                                  ssem.at[0]).start(priority=OTHER_DMA_PRIORITY)
        return [functools.partial(one, r) for r in range(bm)]

    def wait_gather(p):
        pltpu.make_async_copy(hx_hbm.at[pl.ds(0, bm * TILE_ROWS), :], xbufs[p], gsem.at[p]).wait()

    def wait_scatter(p):
        pltpu.make_async_copy(ybufs[p], ys_hbm.at[pl.ds(0, bm * TILE_ROWS), :], ssem.at[0]).wait()

    @pl.when(w == 0)
    def _():
        sg[EXPERT_FF:, :] = jnp.zeros((FF_PAD - EXPERT_FF, D_MODEL), F32)
        su[EXPERT_FF:, :] = jnp.zeros((FF_PAD - EXPERT_FF, D_MODEL), F32)
        wd[EXPERT_FF:, :] = jnp.zeros((FF_PAD - EXPERT_FF, D_MODEL), BF16)
        ybuf1[...] = jnp.zeros(ybuf1.shape, U32)
        for c in weight_copies(expert):
            c.start(priority=OTHER_DMA_PRIORITY)
        for start in gather_starts(src0_ref, 0):
            start()

    @pl.when(jnp.logical_and(active, fe_ref[w] == 1))
    def _():
        for c in weight_copies(expert):
            c.wait()
        for c in range(FF_PAD // LANES):
            cols = slice(c * LANES, (c + 1) * LANES)
            wg[:, cols] = sg[cols, :].T.astype(BF16)
            wu[:, cols] = su[cols, :].T.astype(BF16)
        rows_d = 64

        def cast_down(i, carry):
            r = pl.ds(pl.multiple_of(i * rows_d, rows_d), rows_d)
            wd[r, :] = sd[r, :].astype(BF16)
            return carry

        lax.fori_loop(0, EXPERT_FF // rows_d, cast_down, 0)

        @pl.when(nxt_ref[w] >= 0)
        def _():
            for c in weight_copies(nxt_ref[w]):
                c.start(priority=OTHER_DMA_PRIORITY)

    def first_item_of_block(p):
        @pl.when(blk > 0)
        def _():
            wait_scatter(p)
        wait_gather(p)
        pending = gather_starts(src1_ref, 1 - p) + scatter_starts(1 - p)
        per_slot = -(-len(pending) // N_DMA_SLOTS)

        def between():
            for start in pending[:per_slot]:
                start()
            del pending[:per_slot]

        ys = _swiglu_tiles(xbufs[p], bm, wg, wu, wd, between)
        assert not pending
        for s, chunk in enumerate(_packed_chunks(ys)):
            ybufs[p][pl.ds(s, bm, stride=TILE_ROWS), :] = chunk

    def later_item_of_block(p):
        ys = _swiglu_tiles(xbufs[p], bm, wg, wu, wd)
        row = lax.broadcasted_iota(jnp.int32, (bm, 1), 0)
        mine = jnp.logical_and(row >= lo_ref[w], row < hi_ref[w])
        for s, chunk in enumerate(_packed_chunks(ys)):
            idx = pl.ds(s, bm, stride=TILE_ROWS)
            ybufs[p][idx, :] = jnp.where(mine, chunk, ybufs[p][idx, :])

    parity = lax.rem(blk, 2)
    first = fb_ref[w] == 1
    for p in range(2):
        pl.when(jnp.logical_and(active, jnp.logical_and(first, parity == p)))(
            functools.partial(first_item_of_block, p))
        pl.when(jnp.logical_and(active, jnp.logical_and(jnp.logical_not(first), parity == p)))(
            functools.partial(later_item_of_block, p))

    @pl.when(w == n_items)
    def _():
        last = (n_blk - 1) % 2
        wait_scatter(1 - last)
        wait_gather(1 - last)
        for start in scatter_starts(last):
            start()
        wait_scatter(last)


def _experts(hx, src, dst, items, w_gate, w_up, w_down, layer):
    bm = EXPERT_BLOCK
    n_blk = src.shape[0] - 1
    n_steps = items[0].shape[0]
    n_out_rows = (n_blk + 1) * bm * TILE_ROWS
    smem = functools.partial(pl.BlockSpec, (1, 1, bm), memory_space=pltpu.SMEM)
    grid_spec = pltpu.PrefetchScalarGridSpec(
        num_scalar_prefetch=len(items),
        grid=(n_steps,),
        in_specs=[
            smem(index_map=lambda w, *_: (0, 0, 0)),
            smem(index_map=lambda w, blk, *_: (jnp.minimum(blk[w] + 1, n_blk), 0, 0)),
            smem(index_map=lambda w, blk, *_: (blk[w], 0, 0)),
            pl.BlockSpec(memory_space=pl.ANY),
            pl.BlockSpec(memory_space=pl.ANY),
            pl.BlockSpec(memory_space=pl.ANY),
            pl.BlockSpec(memory_space=pl.ANY),
        ],
        out_specs=pl.BlockSpec(memory_space=pl.ANY),
        scratch_shapes=[
            pltpu.VMEM((bm * TILE_ROWS, LANES), U32), pltpu.VMEM((bm * TILE_ROWS, LANES), U32),
            pltpu.VMEM((bm * TILE_ROWS, LANES), U32), pltpu.VMEM((bm * TILE_ROWS, LANES), U32),
            pltpu.VMEM((FF_PAD, D_MODEL), F32), pltpu.VMEM((FF_PAD, D_MODEL), F32),
            pltpu.VMEM((EXPERT_FF, D_MODEL), F32),
            pltpu.VMEM((D_MODEL, FF_PAD), BF16), pltpu.VMEM((D_MODEL, FF_PAD), BF16),
            pltpu.VMEM((FF_PAD, D_MODEL), BF16),
            pltpu.SemaphoreType.DMA((2,)),
            pltpu.SemaphoreType.DMA((1,)),
            pltpu.SemaphoreType.DMA((3,)),
        ],
    )
    return pl.pallas_call(
        functools.partial(_expert_kernel, layer=layer, n_blk=n_blk),
        grid_spec=grid_spec,
        out_shape=jax.ShapeDtypeStruct((n_out_rows, LANES), U32),
        compiler_params=_cparams(("arbitrary",), 48),
        name="experts",
    )(*items, src, src, dst, hx, w_gate, w_up, w_down)


def _combine_kernel(hx_ref, y0, y1, y2, y3, y4, y5, wt_ref, wg_ref, wu_ref, wd_ref, x_ref, gt_ref, g_ref, o_ref,
                    acc_ref):
    tm = x_ref.shape[1]
    y_refs = (y0, y1, y2, y3, y4, y5)
    wt = wt_ref[...]
    wk = [jnp.broadcast_to(wt[:, k:k + 1], (tm, LANES)) for k in range(TOP_K)]
    todo = list(range(TILE_ROWS))

    def combine_one_chunk():
        if not todo:
            return
        s = todo.pop(0)
        idx = pl.ds(s, tm, stride=TILE_ROWS)
        acc_lo = acc_hi = None
        for k, y_ref in enumerate(y_refs):
            lo, hi = _unpack_halves(y_ref[idx, :])
            acc_lo = wk[k] * lo if k == 0 else acc_lo + wk[k] * lo
            acc_hi = wk[k] * hi if k == 0 else acc_hi + wk[k] * hi
        acc_ref[:, s * LANES:(s + 1) * LANES] = acc_lo
        acc_ref[:, HALF + s * LANES:HALF + (s + 1) * LANES] = acc_hi

    shared = jnp.concatenate(_swiglu_tiles(hx_ref, tm, wg_ref, wu_ref, wd_ref, combine_one_chunk), axis=1)
    while todo:
        combine_one_chunk()
    ff = acc_ref[...] + shared
    o_ref[0] = x_ref[0] + gt_ref[0] * _rms(ff, g_ref[...])


def _combine(hx, ys, wts_t, wg, wu, wd, x, gt, g):
    b, s, d = x.shape
    tm = ROW_BLOCK
    nblk = s // tm
    n_tok = b * s
    row = lambda bi, i: (bi, i, 0)
    tiles = (tm * TILE_ROWS, LANES)
    y_specs = [pl.BlockSpec(tiles, functools.partial(lambda bi, i, k: (k * (n_tok // tm) + bi * nblk + i, 0), k=k))
               for k in range(TOP_K)]
    return pl.pallas_call(
        _combine_kernel,
        grid=(b, nblk),
        in_specs=[pl.BlockSpec(tiles, lambda bi, i: (bi * nblk + i, 0))] + y_specs + [
            pl.BlockSpec((tm, 8), lambda bi, i: (bi * nblk + i, 0)),
            _resident(wg.shape), _resident(wu.shape), _resident(wd.shape),
            pl.BlockSpec((1, tm, d), row),
            pl.BlockSpec((1, 1, d), lambda bi, i: (bi, 0, 0)),
            _resident(g.shape),
        ],
        out_specs=pl.BlockSpec((1, tm, d), row),
        out_shape=jax.ShapeDtypeStruct((b, s, d), F32),
        scratch_shapes=[pltpu.VMEM((tm, d), F32)],
        compiler_params=_cparams(("parallel", "parallel"), 48),
        name="combine",
    )(hx, ys, ys, ys, ys, ys, ys, wts_t, wg, wu, wd, x, gt, g)


def _rope_tables(seq):
    t = jnp.arange(seq, dtype=jnp.int32)
    row = (t // GRID_W).astype(F32)
    col = (t % GRID_W).astype(F32)
    per_axis = A_QK_ROPE // 4
    inv = ROPE_BASE ** (-jnp.arange(per_axis, dtype=F32) / per_axis)
    ang = jnp.concatenate([row[:, None] * inv, col[:, None] * inv], axis=-1)
    cos = jnp.cos(ang)
    sin = jnp.sin(ang)
    cos64 = jnp.concatenate([cos, cos], axis=-1)
    sin64 = jnp.concatenate([-sin, sin], axis=-1)
    return jnp.tile(cos64, (1, A_HEADS)), jnp.tile(sin64, (1, A_HEADS))


def _dft_tables(n):
    k = jnp.arange(n, dtype=jnp.int32)
    ang = ((k[:, None] * k[None, :]) % n).astype(F32) * (2.0 * jnp.pi / n)
    scale = n ** -0.5
    return jnp.cos(ang) * scale, -jnp.sin(ang) * scale


def _swap_halves(w):
    half = w.shape[-1] // 2
    return jnp.concatenate([w[..., half:], w[..., :half]], axis=-1)


def _pad_ff(w, axis):
    pad = [(0, 0)] * w.ndim
    pad[axis] = (0, FF_PAD - EXPERT_FF)
    return jnp.pad(w.astype(BF16), pad)


def _router_parts(w_router):
    w = jnp.pad(w_router, ((0, 0), (0, LANES - N_EXPERTS)))
    hi = w.astype(BF16)
    lo = (w - hi.astype(F32)).astype(BF16)
    return hi, lo


def _moe(hx, logits_t, layer, b_router, w_gate, w_up, w_down, sg, su, sd, x, gt, g_post):
    b, s, d = x.shape
    n_tok = b * s
    top_e, wts, counts = _route(logits_t, b_router)
    src, dst, items = _dispatch_tables(top_e, counts, n_tok)
    ys = _experts(hx, src, dst, items, jnp.swapaxes(w_gate, 2, 3), jnp.swapaxes(w_up, 2, 3), w_down, layer)
    return _combine(hx, ys, wts.T, _pad_ff(sg, 1), _pad_ff(su, 1), _pad_ff(sd, 0), x, gt, g_post)


def kernel(x, c, ctx, c_ctx, mod_w, mod_b, norm_g, a_w_in, a_g_q, a_w_uq, a_g_kv, a_w_ukv, a_w_out, o_w_in, o_g_v, o_w_s, o_b_s, o_conv_w, o_w_out, moe_w_router, moe_b_router, moe_w_gate, moe_w_up, moe_w_down, sh_w_gate, sh_w_up, sh_w_down):
    b, s, d = x.shape
    lc = ctx.shape[1]

    cond = jnp.concatenate([c, c_ctx[None, :], jnp.zeros((COND_ROWS - b - 1, d), F32)], axis=0)
    mod = _adaln(cond, mod_w, mod_b)

    def mod_rows(layer, rows):
        m = mod[layer, rows]
        return [m[:, None, j * d:(j + 1) * d] for j in range(6)]

    def moe(layer, hx, lg, x, gt, g_post):
        return _moe(hx, lg, layer, moe_b_router[layer], moe_w_gate, moe_w_up, moe_w_down,
                    sh_w_gate[layer], sh_w_up[layer], sh_w_down[layer], x, gt, g_post)

    sh1, sc1, gt1, sh2, sc2, gt2 = mod_rows(0, slice(0, b))
    csh1, csc1 = [jnp.broadcast_to(m, (b, 1, d)) for m in mod_rows(0, slice(b, b + 1))[:2]]
    g = norm_g[0]
    w_in = a_w_in[0]
    o = A_Q_RANK + A_KV_RANK
    k_pe_w = w_in[:, o:o + A_QK_ROPE]
    win = jnp.concatenate([w_in[:, :o], w_in[:, o + A_QK_ROPE:], k_pe_w, _swap_halves(k_pe_w)], axis=1).astype(BF16)
    wq = a_w_uq[0].reshape(A_Q_RANK, A_HEADS, A_QK_NOPE + A_QK_ROPE)
    wq_pe = wq[:, :, A_QK_NOPE:]
    wuq = jnp.concatenate([wq[:, :, :A_QK_NOPE].reshape(A_Q_RANK, NOPE_W), wq_pe.reshape(A_Q_RANK, ROPE_W),
                           _swap_halves(wq_pe).reshape(A_Q_RANK, ROPE_W)], axis=1).astype(BF16)
    wkv = a_w_ukv[0].reshape(A_KV_RANK, A_HEADS, A_QK_NOPE + A_V_DIM)
    wukv = jnp.concatenate([wkv[:, :, :A_QK_NOPE].reshape(A_KV_RANK, NOPE_W),
                            wkv[:, :, A_QK_NOPE:].reshape(A_KV_RANK, A_WIDTH)], axis=1).astype(BF16)
    gq = a_g_q[0][None, :]
    gkv = a_g_kv[0][None, :]
    cos, sin = _rope_tables(s)
    cc, cs = _dft_tables(B_GROUP_DIM)
    eye = jnp.eye(B_GROUPS, dtype=F32)
    bdc = jnp.kron(eye, cc).astype(BF16)
    bds = jnp.kron(eye, -cs).astype(BF16)
    a_c, a_s = _dft_tables(s)
    a_c = a_c.astype(BF16)
    a_s = a_s.astype(BF16)

    ones = jnp.ones((lc, ROPE_W), F32)
    _, kc, vc, _, _ = _even_in(ctx, csh1, csc1, g[0][None, :], win, gq, wuq, gkv, wukv, ones, jnp.zeros_like(ones),
                               bdc, bds)
    q, ko, vo, uc, us = _even_in(x, sh1, sc1, g[0][None, :], win, gq, wuq, gkv, wukv, cos, sin, bdc, bds)
    o_a = _attention(q, kc, ko, vc, vo)
    o_b = _fourier(a_c, a_s, uc, us)
    wrh, wrl = _router_parts(moe_w_router[0])
    x, hx, lg = _even_out(o_a, o_b, a_w_out[0].astype(BF16), x, gt1, sh2, sc2, g[1][None, :], g[2][None, :], wrh, wrl)
    x = moe(0, hx, lg, x, gt2, g[3][None, :])

    sh1, sc1, gt1, sh2, sc2, gt2 = mod_rows(1, slice(0, b))
    g = norm_g[1]
    oc, s_, gb = _odd_in(x, sh1, sc1, g[0][None, :], o_w_in[0].astype(BF16), o_g_v[0][None, :],
                         o_w_s[0].astype(BF16), o_b_s[0].T)
    wrh, wrl = _router_parts(moe_w_router[1])
    x, hx, lg = _odd_out(oc, s_, gb, o_conv_w[0], o_w_out[0].astype(BF16), x, gt1, sh2, sc2, g[1][None, :],
                         g[2][None, :], wrh, wrl)
    x = moe(1, hx, lg, x, gt2, g[3][None, :])
    return x
```

```python
import functools

import jax
import jax.numpy as jnp
from jax import lax
from jax.experimental import pallas as pl
from jax.experimental.pallas import tpu as pltpu

F32 = jnp.float32
BF16 = jnp.bfloat16
U32 = jnp.uint32

D_MODEL = 2048
GRID_W = 64
EPS = 1e-6
A_HEADS = 12
A_QK_NOPE = 128
A_QK_ROPE = 64
A_V_DIM = 128
A_Q_RANK = 768
A_KV_RANK = 512
ROPE_BASE = 10000.0
B_GROUPS = 4
B_GROUP_DIM = 128
B_WIDTH = B_GROUPS * B_GROUP_DIM
A_WIDTH = A_HEADS * A_V_DIM
C_HEADS = 8
C_HEAD_DIM = 128
CHUNK = 128
C_WIDTH = C_HEADS * C_HEAD_DIM
D_WIDTH = 1024
N_EXPERTS = 64
TOP_K = 6
N_GROUPS = 8
GROUP_SIZE = N_EXPERTS // N_GROUPS
TOPK_GROUPS = 4
EXPERT_FF = 704
ROUTED_SCALE = 2.5

LANES = 128
VMEM_BYTES_V7X = 64 * 1024 * 1024
FF_PAD = 768
HALF = D_MODEL // 2
TILE_ROWS = 8
QK_PAD = 256
V_PAD = 256
LOG2_E = 1.4426950408889634
COND_ROWS = 32
ROW_BLOCK = 256
OUT_BLOCK = 512
OUT_SUB_BLOCK = 256
EXPERT_BLOCK = 256
ROUTE_BLOCK = 512


def _cparams(semantics, vmem_mib):
    assert vmem_mib * 1024 * 1024 < VMEM_BYTES_V7X
    return pltpu.CompilerParams(dimension_semantics=semantics, vmem_limit_bytes=vmem_mib * 1024 * 1024)


def _resident(shape):
    nd = len(shape)
    return pl.BlockSpec(shape, lambda *_: (0,) * nd, pipeline_mode=pl.Buffered(1))


def _dot(a, b):
    return jnp.dot(a, b, preferred_element_type=F32)


def _dot_nt(a, b):
    return lax.dot_general(a, b, (((1,), (1,)), ((), ())), preferred_element_type=F32)


def _rms(xf, g):
    return xf * lax.rsqrt(jnp.mean(xf * xf, axis=-1, keepdims=True) + EPS) * g


def _split_bf16(a):
    hi = a.astype(BF16)
    lo = (a - hi.astype(F32)).astype(BF16)
    return hi, lo


def _pack_halves(lo_f32, hi_f32):
    lo = lax.bitcast_convert_type(lo_f32.astype(BF16).astype(F32), U32) >> 16
    hi = lax.bitcast_convert_type(hi_f32.astype(BF16).astype(F32), U32) & jnp.uint32(0xFFFF0000)
    return lo | hi


def _unpack_halves(w):
    lo = lax.bitcast_convert_type(w << 16, F32)
    hi = lax.bitcast_convert_type(w & jnp.uint32(0xFFFF0000), F32)
    return lo, hi


def _store_token_tiles(ref, packed, first_row=0):
    m = packed.shape[0]
    for s in range(TILE_ROWS):
        ref[pl.ds(first_row * TILE_ROWS + s, m, stride=TILE_ROWS), :] = packed[:, s * LANES:(s + 1) * LANES]


def _load_token_tiles(ref, m):
    return [ref[pl.ds(s, m, stride=TILE_ROWS), :] for s in range(TILE_ROWS)]


def _silu(x):
    return x * jax.nn.sigmoid(x)


def _adaln_kernel(c_ref, w_ref, b_ref, o_ref):
    a_hi, a_lo = _split_bf16(_silu(c_ref[...]))
    w_hi, w_lo = _split_bf16(w_ref[0])
    o_ref[0] = _dot(a_hi, w_hi) + _dot(a_lo, w_hi) + _dot(a_hi, w_lo) + b_ref[0]


def _adaln(cond, mod_w, mod_b):
    depth, d, n = mod_w.shape
    tn = 512
    return pl.pallas_call(
        _adaln_kernel,
        grid=(depth, n // tn),
        in_specs=[
            pl.BlockSpec((COND_ROWS, d), lambda l, j: (0, 0)),
            pl.BlockSpec((1, d, tn), lambda l, j: (l, 0, j)),
            pl.BlockSpec((1, 1, tn), lambda l, j: (l, 0, j)),
        ],
        out_specs=pl.BlockSpec((1, COND_ROWS, tn), lambda l, j: (l, 0, j)),
        out_shape=jax.ShapeDtypeStruct((depth, COND_ROWS, n), F32),
        compiler_params=_cparams(("parallel", "parallel"), 32),
        name="adaln",
    )(cond, mod_w, mod_b.reshape(depth, 1, n))


NOPE_W = A_HEADS * A_QK_NOPE
ROPE_W = A_HEADS * A_QK_ROPE
EVEN_IN_PAD = A_Q_RANK + A_KV_RANK + B_WIDTH + 2 * A_QK_ROPE


def _even_in_kernel(x_ref, sh_ref, sc_ref, g_ref, win_ref, gq_ref, wuq_ref, gkv_ref, wukv_ref,
                    cos_ref, sin_ref, bdc_ref, bds_ref, q_ref, k_ref, v_ref, uc_ref, us_ref):
    tm = x_ref.shape[1]
    h = _rms(x_ref[0], g_ref[...]) * (1.0 + sc_ref[0]) + sh_ref[0]
    z = _dot(h.astype(BF16), win_ref[...])
    c_q = z[:, :A_Q_RANK]
    c_kv = z[:, A_Q_RANK:A_Q_RANK + A_KV_RANK]
    o = A_Q_RANK + A_KV_RANK
    u_f = z[:, o:o + B_WIDTH].astype(BF16)
    k_pe = z[:, o + B_WIDTH:o + B_WIDTH + A_QK_ROPE]
    k_pe_sw = z[:, o + B_WIDTH + A_QK_ROPE:]
    uc_ref[0] = _dot(u_f, bdc_ref[...]).astype(BF16)
    us_ref[0] = _dot(u_f, bds_ref[...]).astype(BF16)
    cos = cos_ref[...]
    sin = sin_ref[...]
    scale = (A_QK_NOPE + A_QK_ROPE) ** -0.5 * LOG2_E

    q = _dot(_rms(c_q, gq_ref[...]).astype(BF16), wuq_ref[...]) * scale
    q_pe = q[:, NOPE_W:NOPE_W + ROPE_W] * cos + q[:, NOPE_W + ROPE_W:] * sin
    kv = _dot(_rms(c_kv, gkv_ref[...]).astype(BF16), wukv_ref[...])
    k_pe_r = (k_pe * cos[:, :A_QK_ROPE] + k_pe_sw * sin[:, :A_QK_ROPE]).astype(BF16)
    zeros = jnp.zeros((tm, QK_PAD - A_QK_NOPE - A_QK_ROPE), BF16)
    ones_col = (lax.broadcasted_iota(jnp.int32, (tm, V_PAD - A_V_DIM), 1) == 0).astype(BF16)
    for hd in range(A_HEADS):
        n0 = hd * A_QK_NOPE
        r0 = hd * A_QK_ROPE
        q_ref[0, hd, :, :A_QK_NOPE] = q[:, n0:n0 + A_QK_NOPE].astype(BF16)
        q_ref[0, hd, :, A_QK_NOPE:A_QK_NOPE + A_QK_ROPE] = q_pe[:, r0:r0 + A_QK_ROPE].astype(BF16)
        q_ref[0, hd, :, A_QK_NOPE + A_QK_ROPE:] = zeros
        k_ref[0, hd, :, :A_QK_NOPE] = kv[:, n0:n0 + A_QK_NOPE].astype(BF16)
        k_ref[0, hd, :, A_QK_NOPE:A_QK_NOPE + A_QK_ROPE] = k_pe_r
        k_ref[0, hd, :, A_QK_NOPE + A_QK_ROPE:] = zeros
        v_ref[0, hd, :, :A_V_DIM] = kv[:, NOPE_W + n0:NOPE_W + n0 + A_V_DIM].astype(BF16)
        v_ref[0, hd, :, A_V_DIM:] = ones_col


def _even_in(x, sh, sc, g, win, gq, wuq, gkv, wukv, cos, sin, bdc, bds):
    b, l, d = x.shape
    tm = ROW_BLOCK
    row = lambda bi, i: (bi, i, 0)
    per_b = lambda bi, i: (bi, 0, 0)
    heads = lambda bi, i: (bi, 0, i, 0)
    return pl.pallas_call(
        _even_in_kernel,
        grid=(b, l // tm),
        in_specs=[
            pl.BlockSpec((1, tm, d), row),
            pl.BlockSpec((1, 1, d), per_b),
            pl.BlockSpec((1, 1, d), per_b),
            _resident(g.shape), _resident(win.shape), _resident(gq.shape), _resident(wuq.shape),
            _resident(gkv.shape), _resident(wukv.shape),
            pl.BlockSpec((tm, ROPE_W), lambda bi, i: (i, 0)),
            pl.BlockSpec((tm, ROPE_W), lambda bi, i: (i, 0)),
            _resident(bdc.shape), _resident(bds.shape),
        ],
        out_specs=[
            pl.BlockSpec((1, A_HEADS, tm, QK_PAD), heads),
            pl.BlockSpec((1, A_HEADS, tm, QK_PAD), heads),
            pl.BlockSpec((1, A_HEADS, tm, V_PAD), heads),
            pl.BlockSpec((1, tm, B_WIDTH), row),
            pl.BlockSpec((1, tm, B_WIDTH), row),
        ],
        out_shape=[
            jax.ShapeDtypeStruct((b, A_HEADS, l, QK_PAD), BF16),
            jax.ShapeDtypeStruct((b, A_HEADS, l, QK_PAD), BF16),
            jax.ShapeDtypeStruct((b, A_HEADS, l, V_PAD), BF16),
            jax.ShapeDtypeStruct((b, l, B_WIDTH), BF16),
            jax.ShapeDtypeStruct((b, l, B_WIDTH), BF16),
        ],
        compiler_params=_cparams(("parallel", "parallel"), 56),
        name="even_in",
    )(x, sh, sc, g, win, gq, wuq, gkv, wukv, cos, sin, bdc, bds)


ATTN_Q_BLOCK = 512


def _attn_kernel(q_ref, kc_ref, ko_ref, vc_ref, vo_ref, o_ref):
    kc = kc_ref[0, 0]
    ko = ko_ref[0, 0]
    vc = vc_ref[0, 0]
    vo = vo_ref[0, 0]
    tq = ATTN_Q_BLOCK
    n_chunks = q_ref.shape[2] // tq

    def scores(i):
        q = q_ref[0, 0, i * tq:(i + 1) * tq, :]
        return _dot_nt(q, kc), _dot_nt(q, ko)

    nxt = scores(0)
    for i in range(n_chunks):
        rows = slice(i * tq, (i + 1) * tq)
        s_c, s_o = nxt
        if i + 1 < n_chunks:
            nxt = scores(i + 1)
        m = jnp.maximum(jnp.max(s_c, axis=-1, keepdims=True), jnp.max(s_o, axis=-1, keepdims=True))
        p_c = jnp.exp2(s_c - m).astype(BF16)
        p_o = jnp.exp2(s_o - m).astype(BF16)
        o = _dot(p_c, vc) + _dot(p_o, vo)
        o_ref[0, rows, :] = (o[:, :A_V_DIM] / o[:, A_V_DIM:A_V_DIM + 1]).astype(BF16)


def _attention(q, kc, ko, vc, vo):
    b, h, s, _ = q.shape
    lc = kc.shape[2]
    bh = lambda bi, hi: (bi, hi, 0, 0)
    return pl.pallas_call(
        _attn_kernel,
        grid=(b, h),
        in_specs=[
            pl.BlockSpec((1, 1, s, QK_PAD), bh),
            pl.BlockSpec((1, 1, lc, QK_PAD), bh),
            pl.BlockSpec((1, 1, s, QK_PAD), bh),
            pl.BlockSpec((1, 1, lc, V_PAD), bh),
            pl.BlockSpec((1, 1, s, V_PAD), bh),
        ],
        out_specs=pl.BlockSpec((1, s, A_V_DIM), lambda bi, hi: (bi, 0, hi)),
        out_shape=jax.ShapeDtypeStruct((b, s, h * A_V_DIM), BF16),
        compiler_params=_cparams(("parallel", "parallel"), 48),
        name="attention",
    )(q, kc, ko, vc, vo)


def _fourier_kernel(ac_ref, as_ref, uc_ref, us_ref, o_ref):
    o_ref[0] = (_dot(ac_ref[...], uc_ref[0]) + _dot(as_ref[...], us_ref[0])).astype(BF16)


def _fourier(a_c, a_s, uc, us):
    b, s, w = uc.shape
    tm = 512
    return pl.pallas_call(
        _fourier_kernel,
        grid=(b, s // tm),
        in_specs=[
            pl.BlockSpec((tm, s), lambda bi, i: (i, 0)),
            pl.BlockSpec((tm, s), lambda bi, i: (i, 0)),
            pl.BlockSpec((1, s, w), lambda bi, i: (bi, 0, 0)),
            pl.BlockSpec((1, s, w), lambda bi, i: (bi, 0, 0)),
        ],
        out_specs=pl.BlockSpec((1, tm, w), lambda bi, i: (bi, i, 0)),
        out_shape=jax.ShapeDtypeStruct((b, s, w), BF16),
        compiler_params=_cparams(("parallel", "parallel"), 32),
        name="fourier",
    )(a_c, a_s, uc, us)


def _mix_epilogue(mix, r0, x_ref, gt_ref, sh_ref, sc_ref, gpost_ref, gffn_ref, wrh_ref, wrl_ref,
                  xo_ref, hx_ref, lg_ref):
    m = mix.shape[0]
    rows = slice(r0, r0 + m)
    xn = x_ref[0, rows, :] + gt_ref[0] * _rms(mix, gpost_ref[...])
    xo_ref[0, rows, :] = xn
    t = _rms(xn, gffn_ref[...]) * (1.0 + sc_ref[0]) + sh_ref[0]
    t_hi, t_lo = _split_bf16(t)
    wrh = wrh_ref[...]
    lg = _dot(t_hi, wrh) + _dot(t_lo, wrh) + _dot(t_hi, wrl_ref[...])
    lg_ref[:, rows] = lg.T[:N_EXPERTS, :]
    tq = t_hi.astype(F32)
    _store_token_tiles(hx_ref, _pack_halves(tq[:, :HALF], tq[:, HALF:]), r0)


def _pipelined_sub_blocks(mix, tm, epilogue_refs):
    starts = list(range(0, tm, OUT_SUB_BLOCK))
    nxt = mix(starts[0])
    for j, r0 in enumerate(starts):
        cur = nxt
        if j + 1 < len(starts):
            nxt = mix(starts[j + 1])
        _mix_epilogue(cur, r0, *epilogue_refs)


def _even_out_kernel(oa_ref, ob_ref, w_ref, *rest):
    def mix(r0):
        rows = slice(r0, r0 + OUT_SUB_BLOCK)
        return _dot(oa_ref[0, rows, :], w_ref[:A_WIDTH, :]) + _dot(ob_ref[0, rows, :], w_ref[A_WIDTH:, :])

    _pipelined_sub_blocks(mix, oa_ref.shape[1], rest)


def _odd_out_kernel(oc_ref, s_ref, sp_ref, sn_ref, gb_ref, cw_ref, w_ref, *rest):
    i = pl.program_id(1)
    tm = s_ref.shape[1]
    s = s_ref[0].astype(F32)
    prev_row = jnp.where(i > 0, sp_ref[0, 15:16, :].astype(F32), 0.0)
    next_row = jnp.where(i < pl.num_programs(1) - 1, sn_ref[0, 0:1, :].astype(F32), 0.0)
    row = lax.broadcasted_iota(jnp.int32, (tm, 1), 0)
    s_dn = jnp.where(row == 0, prev_row, pltpu.roll(s, 1, axis=0))
    s_up = jnp.where(row == tm - 1, next_row, pltpu.roll(s, tm - 1, axis=0))
    cw = cw_ref[...]
    conv = cw[0:1, :] * s_dn + cw[1:2, :] * s + cw[2:3, :] * s_up
    od = (gb_ref[0].astype(F32) * conv).astype(BF16)
    def mix(r0):
        rows = slice(r0, r0 + OUT_SUB_BLOCK)
        return _dot(oc_ref[0, rows, :], w_ref[:C_WIDTH, :]) + _dot(od[rows, :], w_ref[C_WIDTH:, :])

    _pipelined_sub_blocks(mix, tm, rest)


def _epilogue_specs(b, s, d, tm):
    per_b = lambda bi, i: (bi, 0, 0)
    row = lambda bi, i: (bi, i, 0)
    nblk = s // tm
    in_specs = [
        pl.BlockSpec((1, tm, d), row),
        pl.BlockSpec((1, 1, d), per_b),
        pl.BlockSpec((1, 1, d), per_b),
        pl.BlockSpec((1, 1, d), per_b),
        _resident((1, d)), _resident((1, d)),
        _resident((d, LANES)), _resident((d, LANES)),
    ]
    out_specs = [
        pl.BlockSpec((1, tm, d), row),
        pl.BlockSpec((tm * TILE_ROWS, LANES), lambda bi, i: (bi * nblk + i, 0)),
        pl.BlockSpec((N_EXPERTS, tm), lambda bi, i: (0, bi * nblk + i)),
    ]
    out_shape = [
        jax.ShapeDtypeStruct((b, s, d), F32),
        jax.ShapeDtypeStruct((b * s * TILE_ROWS, LANES), U32),
        jax.ShapeDtypeStruct((N_EXPERTS, b * s), F32),
    ]
    return in_specs, out_specs, out_shape


def _even_out(oa, ob, w_out, x, gt, sh, sc, gpost, gffn, wrh, wrl):
    b, s, d = x.shape
    tm = OUT_BLOCK
    row = lambda bi, i: (bi, i, 0)
    ep_in, out_specs, out_shape = _epilogue_specs(b, s, d, tm)
    return pl.pallas_call(
        _even_out_kernel,
        grid=(b, s // tm),
        in_specs=[pl.BlockSpec((1, tm, A_WIDTH), row), pl.BlockSpec((1, tm, B_WIDTH), row),
                  _resident(w_out.shape)] + ep_in,
        out_specs=out_specs, out_shape=out_shape,
        compiler_params=_cparams(("parallel", "parallel"), 56),
        name="even_out",
    )(oa, ob, w_out, x, gt, sh, sc, gpost, gffn, wrh, wrl)


def _odd_out(oc, s_, gb, conv_w, w_out, x, gt, sh, sc, gpost, gffn, wrh, wrl):
    b, s, d = x.shape
    tm = OUT_BLOCK
    halo = 16
    nh = s // halo
    row = lambda bi, i: (bi, i, 0)
    ep_in, out_specs, out_shape = _epilogue_specs(b, s, d, tm)
    return pl.pallas_call(
        _odd_out_kernel,
        grid=(b, s // tm),
        in_specs=[
            pl.BlockSpec((1, tm, C_WIDTH), row),
            pl.BlockSpec((1, tm, D_WIDTH), row),
            pl.BlockSpec((1, halo, D_WIDTH), lambda bi, i: (bi, jnp.maximum(i * (tm // halo) - 1, 0), 0)),
            pl.BlockSpec((1, halo, D_WIDTH), lambda bi, i: (bi, jnp.minimum((i + 1) * (tm // halo), nh - 1), 0)),
            pl.BlockSpec((1, tm, D_WIDTH), row),
            _resident(conv_w.shape),
            _resident(w_out.shape),
        ] + ep_in,
        out_specs=out_specs, out_shape=out_shape,
        compiler_params=_cparams(("parallel", "parallel"), 56),
        name="odd_out",
    )(oc, s_, s_, s_, gb, conv_w, w_out, x, gt, sh, sc, gpost, gffn, wrh, wrl)


def _odd_in_kernel(x_ref, sh_ref, sc_ref, g_ref, win_ref, gv_ref, ws_ref, bs_ref, oc_ref, s_ref, gb_ref):
    tm = x_ref.shape[1]
    hb = (_rms(x_ref[0], g_ref[...]) * (1.0 + sc_ref[0]) + sh_ref[0]).astype(BF16)
    o = 2 * C_WIDTH
    u = _dot(hb, win_ref[:, :C_WIDTH])
    v = _dot(hb, win_ref[:, C_WIDTH:o])
    hd_ = _dot(hb, win_ref[:, o:o + D_WIDTH])
    gb_ref[0] = _dot(hb, win_ref[:, o + D_WIDTH:o + 2 * D_WIDTH]).astype(BF16)
    gc = _dot(hb, win_ref[:, o + 2 * D_WIDTH:])
    s_ref[0] = (gc * hd_).astype(BF16)
    u = jax.nn.gelu(u)
    v = jax.nn.gelu(v)
    gv = gv_ref[...]
    bs = bs_ref[...]
    for hd in range(C_HEADS):
        c0 = hd * C_HEAD_DIM
        vh = _rms(v[:, c0:c0 + C_HEAD_DIM], gv[:, c0:c0 + C_HEAD_DIM]).astype(BF16)
        w = ws_ref[hd]
        bias = bs[:, hd:hd + 1]
        for n in range(tm // CHUNK):
            p0 = n * CHUNK
            sv = _dot(w, vh[p0:p0 + CHUNK, :]) + bias
            oc_ref[0, p0:p0 + CHUNK, c0:c0 + C_HEAD_DIM] = (u[p0:p0 + CHUNK, c0:c0 + C_HEAD_DIM] * sv).astype(BF16)


def _odd_in(x, sh, sc, g, win, gv, ws, bs_t):
    b, s, d = x.shape
    tm = ROW_BLOCK
    row = lambda bi, i: (bi, i, 0)
    per_b = lambda bi, i: (bi, 0, 0)
    out = jax.ShapeDtypeStruct((b, s, C_WIDTH), BF16)
    return pl.pallas_call(
        _odd_in_kernel,
        grid=(b, s // tm),
        in_specs=[
            pl.BlockSpec((1, tm, d), row),
            pl.BlockSpec((1, 1, d), per_b),
            pl.BlockSpec((1, 1, d), per_b),
            _resident(g.shape), _resident(win.shape), _resident(gv.shape), _resident(ws.shape),
            _resident(bs_t.shape),
        ],
        out_specs=[pl.BlockSpec((1, tm, C_WIDTH), row)] * 3,
        out_shape=[out, out, out],
        compiler_params=_cparams(("parallel", "parallel"), 56),
        name="odd_in",
    )(x, sh, sc, g, win, gv, ws, bs_t)


def _first_index_of_max(vals, idx, sentinel):
    m = jnp.max(vals, axis=0, keepdims=True)
    first = jnp.min(jnp.where(vals == m, idx, sentinel), axis=0, keepdims=True)
    return m, first


def _route_kernel(lg_ref, b_ref, e_ref, w_ref, c_ref):
    tn = lg_ref.shape[1]
    scores = jax.nn.sigmoid(lg_ref[...])
    sel = scores + b_ref[...]
    row8 = lax.broadcasted_iota(jnp.int32, (8, tn), 0).astype(F32)
    neg = -jnp.inf
    gs = jnp.zeros((N_GROUPS, tn), F32)
    for g in range(N_GROUPS):
        blk = sel[g * GROUP_SIZE:(g + 1) * GROUP_SIZE, :]
        m1, i1 = _first_index_of_max(blk, row8, float(GROUP_SIZE))
        m2 = jnp.max(jnp.where(row8 == i1, neg, blk), axis=0, keepdims=True)
        gs = jnp.where(row8 == float(g), m1 + m2, gs)
    keep = jnp.zeros((N_GROUPS, tn), F32)
    cur = gs
    for _ in range(TOPK_GROUPS):
        _, first = _first_index_of_max(cur, row8, float(N_GROUPS))
        pick = row8 == first
        keep = jnp.where(pick, 1.0, keep)
        cur = jnp.where(pick, neg, cur)
    masked = jnp.concatenate(
        [jnp.where(jnp.max(jnp.where(row8 == float(g), keep, 0.0), axis=0, keepdims=True) > 0.5,
                   sel[g * GROUP_SIZE:(g + 1) * GROUP_SIZE, :], neg) for g in range(N_GROUPS)],
        axis=0)
    eidx = lax.broadcasted_iota(jnp.int32, (N_EXPERTS, tn), 0).astype(F32)
    e_out = jnp.zeros((8, tn), F32)
    w_out = jnp.zeros((8, tn), F32)
    total = jnp.zeros((1, tn), F32)
    picked = jnp.zeros((N_EXPERTS, tn), F32)
    for k in range(TOP_K):
        _, first = _first_index_of_max(masked, eidx, float(N_EXPERTS))
        pick = eidx == first
        wk = jnp.sum(jnp.where(pick, scores, 0.0), axis=0, keepdims=True)
        masked = jnp.where(pick, neg, masked)
        picked = jnp.where(pick, 1.0, picked)
        e_out = jnp.where(row8 == float(k), first, e_out)
        w_out = jnp.where(row8 == float(k), wk, w_out)
        total = total + wk
    e_ref[...] = e_out.astype(jnp.int32)
    w_ref[...] = w_out / total * ROUTED_SCALE
    c_ref[...] = jnp.broadcast_to(jnp.sum(picked, axis=1, keepdims=True), c_ref.shape).astype(jnp.int32)


def _route(logits_t, bias):
    e, t = logits_t.shape
    tn = ROUTE_BLOCK
    nblk = t // tn
    top_e, wts, cnt = pl.pallas_call(
        _route_kernel,
        grid=(nblk,),
        in_specs=[pl.BlockSpec((e, tn), lambda i: (0, i)), pl.BlockSpec((e, 1), lambda i: (0, 0))],
        out_specs=[pl.BlockSpec((8, tn), lambda i: (0, i))] * 2 + [pl.BlockSpec((e, LANES), lambda i: (0, i))],
        out_shape=[jax.ShapeDtypeStruct((8, t), jnp.int32), jax.ShapeDtypeStruct((8, t), F32),
                   jax.ShapeDtypeStruct((e, nblk * LANES), jnp.int32)],
        compiler_params=_cparams(("parallel",), 32),
        name="route",
    )(logits_t, bias.reshape(e, 1))
    counts = jnp.sum(cnt.reshape(e, nblk, LANES)[:, :, 0], axis=1)
    return top_e, wts, counts


def _dispatch_tables(top_e, wts, counts, n_tok):
    bm = EXPERT_BLOCK
    n_asg = n_tok * TOP_K
    n_blk = n_asg // bm
    n_steps = n_blk + N_EXPERTS
    i32 = jnp.int32
    flat_e = top_e[:TOP_K].T.reshape(n_asg)
    flat_w = wts[:TOP_K].T.reshape(n_asg)
    _, order, w_sorted = lax.sort((flat_e, jnp.arange(n_asg, dtype=i32), flat_w), num_keys=1, is_stable=True)
    w_rows = w_sorted.reshape(n_blk, bm, 1)
    tok = order // TOP_K
    k = order - tok * TOP_K
    src = jnp.concatenate([tok * TILE_ROWS, jnp.zeros((bm,), i32)]).reshape(n_blk + 1, 1, bm)
    spare = (n_asg + jnp.arange(bm, dtype=i32)) * TILE_ROWS
    dst = jnp.concatenate([spare, (k * n_tok + tok) * TILE_ROWS]).reshape(n_blk + 1, 1, bm)

    experts = jnp.arange(N_EXPERTS, dtype=i32)
    ends = jnp.cumsum(counts).astype(i32)
    starts = ends - counts
    first_blk = starts // bm
    n_e_blk = jnp.where(counts > 0, (ends - 1) // bm - first_blk + 1, 0)
    item_end = jnp.cumsum(n_e_blk).astype(i32)
    item_start = item_end - n_e_blk
    n_items = item_end[-1]
    e_last = jnp.max(jnp.where(counts > 0, experts, 0))
    w = jnp.arange(n_steps, dtype=i32)
    active = w < n_items
    e_w = jnp.where(active, jnp.sum((item_end[None, :] <= w[:, None]).astype(i32), axis=1), e_last)
    onehot = e_w[:, None] == experts[None, :]
    of_item = lambda per_expert: jnp.sum(jnp.where(onehot, per_expert[None, :], 0), axis=1)
    blk_w = jnp.where(active, of_item(first_blk) + w - of_item(item_start), n_blk)
    lo_w = jnp.clip(of_item(starts) - blk_w * bm, 0, bm)
    hi_w = jnp.clip(of_item(ends) - blk_w * bm, 0, bm)
    neg1 = jnp.full((1,), -1, i32)
    first_of_blk = (blk_w != jnp.concatenate([neg1, blk_w[:-1]])).astype(i32)
    first_of_e = ((e_w != jnp.concatenate([neg1, e_w[:-1]])) & active).astype(i32)
    cand = jnp.where(counts > 0, experts, N_EXPERTS)
    suffix_min = lax.cummin(cand[::-1])[::-1]
    nxt = jnp.concatenate([suffix_min[1:], jnp.full((1,), N_EXPERTS, i32)])
    nxt = jnp.where(nxt >= N_EXPERTS, -1, nxt)
    return src, dst, w_rows, (blk_w, e_w, lo_w, hi_w, first_of_blk, first_of_e, of_item(nxt), n_items.reshape(1))


FF_CHUNK = 256
DOWN_CHUNK = 512


CHUNKS_PER_HALF = HALF // DOWN_CHUNK
TILES_PER_CHUNK = DOWN_CHUNK // LANES


def _swiglu_tiles(x_ref, m, wg_ref, wu_ref, wd_ref, between=lambda: None, on_pair=None):
    words = _load_token_tiles(x_ref, m)
    x_lo = jnp.concatenate([lax.bitcast_convert_type(w << 16, F32).astype(BF16) for w in words], axis=1)
    first = _dot(x_lo, wg_ref[:HALF, :FF_CHUNK])
    x_hi = jnp.concatenate(
        [lax.bitcast_convert_type(w & jnp.uint32(0xFFFF0000), F32).astype(BF16) for w in words], axis=1)
    hs = []
    for c in range(FF_PAD // FF_CHUNK):
        cols = slice(c * FF_CHUNK, (c + 1) * FF_CHUNK)
        g_lo = first if c == 0 else _dot(x_lo, wg_ref[:HALF, cols])
        g = g_lo + _dot(x_hi, wg_ref[HALF:, cols])
        between()
        u = _dot(x_lo, wu_ref[:HALF, cols]) + _dot(x_hi, wu_ref[HALF:, cols])
        between()
        hs.append((_silu(g) * u).astype(BF16))
    h = jnp.concatenate(hs, axis=1)
    down = lambda n: _dot(h, wd_ref[:, n * DOWN_CHUNK:(n + 1) * DOWN_CHUNK])
    ys = [None] * (2 * CHUNKS_PER_HALF)
    done = None
    for j in range(CHUNKS_PER_HALF):
        ys[j] = down(j)
        between()
        ys[CHUNKS_PER_HALF + j] = down(CHUNKS_PER_HALF + j)
        between()
        if on_pair is not None and done is not None:
            on_pair(done, ys[done], ys[CHUNKS_PER_HALF + done])
        done = j
    if on_pair is not None:
        on_pair(done, ys[done], ys[CHUNKS_PER_HALF + done])
    return ys


def _packed_pair(j, lo_chunk, hi_chunk):
    out = []
    for t in range(TILES_PER_CHUNK):
        cols = slice(t * LANES, (t + 1) * LANES)
        out.append((j * TILES_PER_CHUNK + t, _pack_halves(lo_chunk[:, cols], hi_chunk[:, cols])))
    return out


N_DMA_SLOTS = (FF_PAD // FF_CHUNK) * 2 + D_MODEL // DOWN_CHUNK
GATHER_DMA_PRIORITY = 0
OTHER_DMA_PRIORITY = 1


def _expert_kernel(blk_ref, exp_ref, lo_ref, hi_ref, fb_ref, fe_ref, nxt_ref, ni_ref,
                   src0_ref, src1_ref, dst_ref, wrow_ref, hx_hbm, wg_hbm, wu_hbm, wd_hbm,
                   ys_hbm,
                   xbuf0, xbuf1, ybuf0, ybuf1, sg, su, sd, wg, wu, wd, gsem, ssem, wsem, *, layer, n_blk):
    bm = EXPERT_BLOCK
    w = pl.program_id(0)
    n_items = ni_ref[0]
    blk = blk_ref[w]
    expert = exp_ref[w]
    active = w < n_items
    xbufs = (xbuf0, xbuf1)
    ybufs = (ybuf0, ybuf1)
    tile = lambda r: pl.ds(r * TILE_ROWS, TILE_ROWS)

    def weight_copies(e):
        return (pltpu.make_async_copy(wg_hbm.at[layer, e], sg.at[pl.ds(0, EXPERT_FF), :], wsem.at[0]),
                pltpu.make_async_copy(wu_hbm.at[layer, e], su.at[pl.ds(0, EXPERT_FF), :], wsem.at[1]),
                pltpu.make_async_copy(wd_hbm.at[layer, e], sd, wsem.at[2]))

    def gather_starts(src_ref, p):
        def one(r):
            row = pl.multiple_of(src_ref[0, 0, r], TILE_ROWS)
            pltpu.make_async_copy(hx_hbm.at[pl.ds(row, TILE_ROWS), :], xbufs[p].at[tile(r), :],
                                  gsem.at[p]).start(priority=GATHER_DMA_PRIORITY)
        return [functools.partial(one, r) for r in range(bm)]

    def scatter_starts(p):
        def one(r):
            row = pl.multiple_of(dst_ref[0, 0, r], TILE_ROWS)
            pltpu.make_async_copy(ybufs[p].at[tile(r), :], ys_hbm.at[pl.ds(row, TILE_ROWS), :],
                                  ssem.at[0]).start(priority=OTHER_DMA_PRIORITY)
        return [functools.partial(one, r) for r in range(bm)]

    def wait_gather(p):
        pltpu.make_async_copy(hx_hbm.at[pl.ds(0, bm * TILE_ROWS), :], xbufs[p], gsem.at[p]).wait()

    def wait_scatter(p):
        pltpu.make_async_copy(ybufs[p], ys_hbm.at[pl.ds(0, bm * TILE_ROWS), :], ssem.at[0]).wait()

    @pl.when(w == 0)
    def _():
        sg[EXPERT_FF:, :] = jnp.zeros((FF_PAD - EXPERT_FF, D_MODEL), F32)
        su[EXPERT_FF:, :] = jnp.zeros((FF_PAD - EXPERT_FF, D_MODEL), F32)
        wd[EXPERT_FF:, :] = jnp.zeros((FF_PAD - EXPERT_FF, D_MODEL), BF16)
        ybuf1[...] = jnp.zeros(ybuf1.shape, U32)
        for c in weight_copies(expert):
            c.start(priority=OTHER_DMA_PRIORITY)
        for start in gather_starts(src0_ref, 0):
            start()

    @pl.when(jnp.logical_and(active, fe_ref[w] == 1))
    def _():
        for c in weight_copies(expert):
            c.wait()
        for c in range(FF_PAD // LANES):
            cols = slice(c * LANES, (c + 1) * LANES)
            wg[:, cols] = sg[cols, :].T.astype(BF16)
            wu[:, cols] = su[cols, :].T.astype(BF16)
        rows_d = 64

        def cast_down(i, carry):
            r = pl.ds(pl.multiple_of(i * rows_d, rows_d), rows_d)
            wd[r, :] = sd[r, :].astype(BF16)
            return carry

        lax.fori_loop(0, EXPERT_FF // rows_d, cast_down, 0)

        @pl.when(nxt_ref[w] >= 0)
        def _():
            for c in weight_copies(nxt_ref[w]):
                c.start(priority=OTHER_DMA_PRIORITY)

    def first_item_of_block(p):
        @pl.when(blk > 0)
        def _():
            wait_scatter(p)
        wait_gather(p)
        pending = gather_starts(src1_ref, 1 - p) + scatter_starts(1 - p)
        per_slot = -(-len(pending) // N_DMA_SLOTS)

        def between():
            for start in pending[:per_slot]:
                start()
            del pending[:per_slot]

        def store_pair(j, lo_chunk, hi_chunk):
            wcol = wrow_ref[0]
            for s, chunk in _packed_pair(j, lo_chunk * wcol, hi_chunk * wcol):
                ybufs[p][pl.ds(s, bm, stride=TILE_ROWS), :] = chunk

        _swiglu_tiles(xbufs[p], bm, wg, wu, wd, between, store_pair)
        assert not pending

    def later_item_of_block(p):
        row = lax.broadcasted_iota(jnp.int32, (bm, 1), 0)
        mine = jnp.logical_and(row >= lo_ref[w], row < hi_ref[w])

        def merge_pair(j, lo_chunk, hi_chunk):
            wcol = wrow_ref[0]
            for s, chunk in _packed_pair(j, lo_chunk * wcol, hi_chunk * wcol):
                idx = pl.ds(s, bm, stride=TILE_ROWS)
                ybufs[p][idx, :] = jnp.where(mine, chunk, ybufs[p][idx, :])

        _swiglu_tiles(xbufs[p], bm, wg, wu, wd, on_pair=merge_pair)

    parity = lax.rem(blk, 2)
    first = fb_ref[w] == 1
    for p in range(2):
        pl.when(jnp.logical_and(active, jnp.logical_and(first, parity == p)))(
            functools.partial(first_item_of_block, p))
        pl.when(jnp.logical_and(active, jnp.logical_and(jnp.logical_not(first), parity == p)))(
            functools.partial(later_item_of_block, p))

    @pl.when(w == n_items)
    def _():
        last = (n_blk - 1) % 2
        wait_scatter(1 - last)
        wait_gather(1 - last)
        for start in scatter_starts(last):
            start()
        wait_scatter(last)


def _experts(hx, src, dst, w_rows, items, w_gate, w_up, w_down, layer):
    bm = EXPERT_BLOCK
    n_blk = src.shape[0] - 1
    n_steps = items[0].shape[0]
    n_out_rows = (n_blk + 1) * bm * TILE_ROWS
    smem = functools.partial(pl.BlockSpec, (1, 1, bm), memory_space=pltpu.SMEM)
    grid_spec = pltpu.PrefetchScalarGridSpec(
        num_scalar_prefetch=len(items),
        grid=(n_steps,),
        in_specs=[
            smem(index_map=lambda w, *_: (0, 0, 0)),
            smem(index_map=lambda w, blk, *_: (jnp.minimum(blk[w] + 1, n_blk), 0, 0)),
            smem(index_map=lambda w, blk, *_: (blk[w], 0, 0)),
            pl.BlockSpec((1, bm, 1), lambda w, blk, *_: (jnp.minimum(blk[w], n_blk - 1), 0, 0)),
            pl.BlockSpec(memory_space=pl.ANY),
            pl.BlockSpec(memory_space=pl.ANY),
            pl.BlockSpec(memory_space=pl.ANY),
            pl.BlockSpec(memory_space=pl.ANY),
        ],
        out_specs=pl.BlockSpec(memory_space=pl.ANY),
        scratch_shapes=[
            pltpu.VMEM((bm * TILE_ROWS, LANES), U32), pltpu.VMEM((bm * TILE_ROWS, LANES), U32),
            pltpu.VMEM((bm * TILE_ROWS, LANES), U32), pltpu.VMEM((bm * TILE_ROWS, LANES), U32),
            pltpu.VMEM((FF_PAD, D_MODEL), F32), pltpu.VMEM((FF_PAD, D_MODEL), F32),
            pltpu.VMEM((EXPERT_FF, D_MODEL), F32),
            pltpu.VMEM((D_MODEL, FF_PAD), BF16), pltpu.VMEM((D_MODEL, FF_PAD), BF16),
            pltpu.VMEM((FF_PAD, D_MODEL), BF16),
            pltpu.SemaphoreType.DMA((2,)),
            pltpu.SemaphoreType.DMA((1,)),
            pltpu.SemaphoreType.DMA((3,)),
        ],
    )
    return pl.pallas_call(
        functools.partial(_expert_kernel, layer=layer, n_blk=n_blk),
        grid_spec=grid_spec,
        out_shape=jax.ShapeDtypeStruct((n_out_rows, LANES), U32),
        compiler_params=_cparams(("arbitrary",), 48),
        name="experts",
    )(*items, src, src, dst, w_rows, hx, w_gate, w_up, w_down)


def _combine_kernel(hx_ref, y0, y1, y2, y3, y4, y5, wg_ref, wu_ref, wd_ref, x_ref, gt_ref, g_ref, o_ref,
                    acc_ref):
    tm = x_ref.shape[1]
    y_refs = (y0, y1, y2, y3, y4, y5)
    todo = list(range(TILE_ROWS))

    def combine_one_chunk():
        if not todo:
            return
        s = todo.pop(0)
        idx = pl.ds(s, tm, stride=TILE_ROWS)
        acc = None
        for y_ref in y_refs:
            pair = pltpu.bitcast(y_ref[idx, :], BF16)
            acc = pair if acc is None else acc + pair
        lo, hi = _unpack_halves(pltpu.bitcast(acc, U32))
        acc_ref[:, s * LANES:(s + 1) * LANES] = lo
        acc_ref[:, HALF + s * LANES:HALF + (s + 1) * LANES] = hi

    shared = jnp.concatenate(_swiglu_tiles(hx_ref, tm, wg_ref, wu_ref, wd_ref, combine_one_chunk), axis=1)
    while todo:
        combine_one_chunk()
    ff = acc_ref[...] + shared
    o_ref[0] = x_ref[0] + gt_ref[0] * _rms(ff, g_ref[...])


def _combine(hx, ys, wg, wu, wd, x, gt, g):
    b, s, d = x.shape
    tm = ROW_BLOCK
    nblk = s // tm
    n_tok = b * s
    row = lambda bi, i: (bi, i, 0)
    tiles = (tm * TILE_ROWS, LANES)
    y_specs = [pl.BlockSpec(tiles, functools.partial(lambda bi, i, k: (k * (n_tok // tm) + bi * nblk + i, 0), k=k))
               for k in range(TOP_K)]
    return pl.pallas_call(
        _combine_kernel,
        grid=(b, nblk),
        in_specs=[pl.BlockSpec(tiles, lambda bi, i: (bi * nblk + i, 0))] + y_specs + [
            _resident(wg.shape), _resident(wu.shape), _resident(wd.shape),
            pl.BlockSpec((1, tm, d), row),
            pl.BlockSpec((1, 1, d), lambda bi, i: (bi, 0, 0)),
            _resident(g.shape),
        ],
        out_specs=pl.BlockSpec((1, tm, d), row),
        out_shape=jax.ShapeDtypeStruct((b, s, d), F32),
        scratch_shapes=[pltpu.VMEM((tm, d), F32)],
        compiler_params=_cparams(("parallel", "parallel"), 48),
        name="combine",
    )(hx, ys, ys, ys, ys, ys, ys, wg, wu, wd, x, gt, g)


def _rope_tables(seq):
    t = jnp.arange(seq, dtype=jnp.int32)
    row = (t // GRID_W).astype(F32)
    col = (t % GRID_W).astype(F32)
    per_axis = A_QK_ROPE // 4
    inv = ROPE_BASE ** (-jnp.arange(per_axis, dtype=F32) / per_axis)
    ang = jnp.concatenate([row[:, None] * inv, col[:, None] * inv], axis=-1)
    cos = jnp.cos(ang)
    sin = jnp.sin(ang)
    cos64 = jnp.concatenate([cos, cos], axis=-1)
    sin64 = jnp.concatenate([-sin, sin], axis=-1)
    return jnp.tile(cos64, (1, A_HEADS)), jnp.tile(sin64, (1, A_HEADS))


def _dft_tables(n):
    k = jnp.arange(n, dtype=jnp.int32)
    ang = ((k[:, None] * k[None, :]) % n).astype(F32) * (2.0 * jnp.pi / n)
    scale = n ** -0.5
    return jnp.cos(ang) * scale, -jnp.sin(ang) * scale


def _swap_halves(w):
    half = w.shape[-1] // 2
    return jnp.concatenate([w[..., half:], w[..., :half]], axis=-1)


def _pad_ff(w, axis):
    pad = [(0, 0)] * w.ndim
    pad[axis] = (0, FF_PAD - EXPERT_FF)
    return jnp.pad(w.astype(BF16), pad)


def _router_parts(w_router):
    w = jnp.pad(w_router, ((0, 0), (0, LANES - N_EXPERTS)))
    hi = w.astype(BF16)
    lo = (w - hi.astype(F32)).astype(BF16)
    return hi, lo


def _moe(hx, logits_t, layer, b_router, w_gate, w_up, w_down, sg, su, sd, x, gt, g_post):
    b, s, d = x.shape
    n_tok = b * s
    top_e, wts, counts = _route(logits_t, b_router)
    src, dst, w_rows, items = _dispatch_tables(top_e, wts, counts, n_tok)
    ys = _experts(hx, src, dst, w_rows, items, jnp.swapaxes(w_gate, 2, 3), jnp.swapaxes(w_up, 2, 3), w_down,
                  layer)
    return _combine(hx, ys, _pad_ff(sg, 1), _pad_ff(su, 1), _pad_ff(sd, 0), x, gt, g_post)


def kernel(x, c, ctx, c_ctx, mod_w, mod_b, norm_g, a_w_in, a_g_q, a_w_uq, a_g_kv, a_w_ukv, a_w_out, o_w_in, o_g_v, o_w_s, o_b_s, o_conv_w, o_w_out, moe_w_router, moe_b_router, moe_w_gate, moe_w_up, moe_w_down, sh_w_gate, sh_w_up, sh_w_down):
    b, s, d = x.shape
    lc = ctx.shape[1]

    cond = jnp.concatenate([c, c_ctx[None, :], jnp.zeros((COND_ROWS - b - 1, d), F32)], axis=0)
    mod = _adaln(cond, mod_w, mod_b)

    def mod_rows(layer, rows):
        m = mod[layer, rows]
        return [m[:, None, j * d:(j + 1) * d] for j in range(6)]

    def moe(layer, hx, lg, x, gt, g_post):
        return _moe(hx, lg, layer, moe_b_router[layer], moe_w_gate, moe_w_up, moe_w_down,
                    sh_w_gate[layer], sh_w_up[layer], sh_w_down[layer], x, gt, g_post)

    sh1, sc1, gt1, sh2, sc2, gt2 = mod_rows(0, slice(0, b))
    csh1, csc1 = [jnp.broadcast_to(m, (b, 1, d)) for m in mod_rows(0, slice(b, b + 1))[:2]]
    g = norm_g[0]
    w_in = a_w_in[0]
    o = A_Q_RANK + A_KV_RANK
    k_pe_w = w_in[:, o:o + A_QK_ROPE]
    win = jnp.concatenate([w_in[:, :o], w_in[:, o + A_QK_ROPE:], k_pe_w, _swap_halves(k_pe_w)], axis=1).astype(BF16)
    wq = a_w_uq[0].reshape(A_Q_RANK, A_HEADS, A_QK_NOPE + A_QK_ROPE)
    wq_pe = wq[:, :, A_QK_NOPE:]
    wuq = jnp.concatenate([wq[:, :, :A_QK_NOPE].reshape(A_Q_RANK, NOPE_W), wq_pe.reshape(A_Q_RANK, ROPE_W),
                           _swap_halves(wq_pe).reshape(A_Q_RANK, ROPE_W)], axis=1).astype(BF16)
    wkv = a_w_ukv[0].reshape(A_KV_RANK, A_HEADS, A_QK_NOPE + A_V_DIM)
    wukv = jnp.concatenate([wkv[:, :, :A_QK_NOPE].reshape(A_KV_RANK, NOPE_W),
                            wkv[:, :, A_QK_NOPE:].reshape(A_KV_RANK, A_WIDTH)], axis=1).astype(BF16)
    gq = a_g_q[0][None, :]
    gkv = a_g_kv[0][None, :]
    cos, sin = _rope_tables(s)
    cc, cs = _dft_tables(B_GROUP_DIM)
    eye = jnp.eye(B_GROUPS, dtype=F32)
    bdc = jnp.kron(eye, cc).astype(BF16)
    bds = jnp.kron(eye, -cs).astype(BF16)
    a_c, a_s = _dft_tables(s)
    a_c = a_c.astype(BF16)
    a_s = a_s.astype(BF16)

    ones = jnp.ones((lc, ROPE_W), F32)
    _, kc, vc, _, _ = _even_in(ctx, csh1, csc1, g[0][None, :], win, gq, wuq, gkv, wukv, ones, jnp.zeros_like(ones),
                               bdc, bds)
    q, ko, vo, uc, us = _even_in(x, sh1, sc1, g[0][None, :], win, gq, wuq, gkv, wukv, cos, sin, bdc, bds)
    o_a = _attention(q, kc, ko, vc, vo)
    o_b = _fourier(a_c, a_s, uc, us)
    wrh, wrl = _router_parts(moe_w_router[0])
    x, hx, lg = _even_out(o_a, o_b, a_w_out[0].astype(BF16), x, gt1, sh2, sc2, g[1][None, :], g[2][None, :], wrh, wrl)
    x = moe(0, hx, lg, x, gt2, g[3][None, :])

    sh1, sc1, gt1, sh2, sc2, gt2 = mod_rows(1, slice(0, b))
    g = norm_g[1]
    oc, s_, gb = _odd_in(x, sh1, sc1, g[0][None, :], o_w_in[0].astype(BF16), o_g_v[0][None, :],
                         o_w_s[0].astype(BF16), o_b_s[0].T)
    wrh, wrl = _router_parts(moe_w_router[1])
    x, hx, lg = _odd_out(oc, s_, gb, o_conv_w[0], o_w_out[0].astype(BF16), x, gt1, sh2, sc2, g[1][None, :],
                         g[2][None, :], wrh, wrl)
    x = moe(1, hx, lg, x, gt2, g[3][None, :])
    return x
```

```python
import functools

import jax
import jax.numpy as jnp
from jax import lax
from jax.experimental import pallas as pl
from jax.experimental.pallas import tpu as pltpu

F32 = jnp.float32
BF16 = jnp.bfloat16

D_MODEL = 2048
GRID_W = 64
EPS = 1e-6
A_HEADS = 12
A_QK_NOPE = 128
A_QK_ROPE = 64
A_V_DIM = 128
A_Q_RANK = 768
A_KV_RANK = 512
ROPE_BASE = 10000.0
B_GROUPS = 4
B_GROUP_DIM = 128
B_WIDTH = B_GROUPS * B_GROUP_DIM
A_WIDTH = A_HEADS * A_V_DIM
C_HEADS = 8
C_HEAD_DIM = 128
CHUNK = 128
C_WIDTH = C_HEADS * C_HEAD_DIM
D_WIDTH = 1024
N_EXPERTS = 64
TOP_K = 6
N_GROUPS = 8
GROUP_SIZE = N_EXPERTS // N_GROUPS
TOPK_GROUPS = 4
EXPERT_FF = 704
ROUTED_SCALE = 2.5

LANES = 128
VMEM_BYTES_V7X = 64 * 1024 * 1024
FF_PAD = 768
HALF = D_MODEL // 2
TILE_ROWS = D_MODEL // LANES
QK_PAD = 256
V_PAD = 256
LOG2_E = 1.4426950408889634
COND_ROWS = 32
ROW_BLOCK = 256
OUT_BLOCK = 512
OUT_SUB_BLOCK = 256
EXPERT_BLOCK = 256
ROUTE_BLOCK = 512


def _cparams(semantics, vmem_mib):
    assert vmem_mib * 1024 * 1024 < VMEM_BYTES_V7X
    return pltpu.CompilerParams(dimension_semantics=semantics, vmem_limit_bytes=vmem_mib * 1024 * 1024)


def _resident(shape):
    nd = len(shape)
    return pl.BlockSpec(shape, lambda *_: (0,) * nd, pipeline_mode=pl.Buffered(1))


def _dot(a, b):
    return jnp.dot(a, b, preferred_element_type=F32)


def _dot_nt(a, b):
    return lax.dot_general(a, b, (((1,), (1,)), ((), ())), preferred_element_type=F32)


def _rms(xf, g):
    return xf * lax.rsqrt(jnp.mean(xf * xf, axis=-1, keepdims=True) + EPS) * g


def _split_bf16(a):
    hi = a.astype(BF16)
    lo = (a - hi.astype(F32)).astype(BF16)
    return hi, lo


def _store_token_tiles(ref, rows, first_row=0):
    m = rows.shape[0]
    for s in range(TILE_ROWS):
        ref[pl.ds(first_row * TILE_ROWS + s, m, stride=TILE_ROWS), :] = rows[:, s * LANES:(s + 1) * LANES]


def _load_token_tiles(ref, m):
    return [ref[pl.ds(s, m, stride=TILE_ROWS), :] for s in range(TILE_ROWS)]


def _silu(x):
    return x * jax.nn.sigmoid(x)


def _adaln_kernel(c_ref, w_ref, b_ref, o_ref):
    a_hi, a_lo = _split_bf16(_silu(c_ref[...]))
    w_hi, w_lo = _split_bf16(w_ref[0])
    o_ref[0] = _dot(a_hi, w_hi) + _dot(a_lo, w_hi) + _dot(a_hi, w_lo) + b_ref[0]


def _adaln(cond, mod_w, mod_b):
    depth, d, n = mod_w.shape
    tn = 512
    return pl.pallas_call(
        _adaln_kernel,
        grid=(depth, n // tn),
        in_specs=[
            pl.BlockSpec((COND_ROWS, d), lambda l, j: (0, 0)),
            pl.BlockSpec((1, d, tn), lambda l, j: (l, 0, j)),
            pl.BlockSpec((1, 1, tn), lambda l, j: (l, 0, j)),
        ],
        out_specs=pl.BlockSpec((1, COND_ROWS, tn), lambda l, j: (l, 0, j)),
        out_shape=jax.ShapeDtypeStruct((depth, COND_ROWS, n), F32),
        compiler_params=_cparams(("parallel", "parallel"), 32),
        name="adaln",
    )(cond, mod_w, mod_b.reshape(depth, 1, n))


NOPE_W = A_HEADS * A_QK_NOPE
ROPE_W = A_HEADS * A_QK_ROPE
EVEN_IN_PAD = A_Q_RANK + A_KV_RANK + B_WIDTH + 2 * A_QK_ROPE


def _even_in_kernel(x_ref, sh_ref, sc_ref, g_ref, win_ref, gq_ref, wuq_ref, gkv_ref, wukv_ref,
                    cos_ref, sin_ref, bdc_ref, bds_ref, q_ref, k_ref, v_ref, uc_ref, us_ref):
    tm = x_ref.shape[1]
    h = _rms(x_ref[0], g_ref[...]) * (1.0 + sc_ref[0]) + sh_ref[0]
    z = _dot(h.astype(BF16), win_ref[...])
    c_q = z[:, :A_Q_RANK]
    c_kv = z[:, A_Q_RANK:A_Q_RANK + A_KV_RANK]
    o = A_Q_RANK + A_KV_RANK
    u_f = z[:, o:o + B_WIDTH].astype(BF16)
    k_pe = z[:, o + B_WIDTH:o + B_WIDTH + A_QK_ROPE]
    k_pe_sw = z[:, o + B_WIDTH + A_QK_ROPE:]
    uc_ref[0] = _dot(u_f, bdc_ref[...]).astype(BF16)
    us_ref[0] = _dot(u_f, bds_ref[...]).astype(BF16)
    cos = cos_ref[...]
    sin = sin_ref[...]
    scale = (A_QK_NOPE + A_QK_ROPE) ** -0.5 * LOG2_E

    q = _dot(_rms(c_q, gq_ref[...]).astype(BF16), wuq_ref[...]) * scale
    q_pe = q[:, NOPE_W:NOPE_W + ROPE_W] * cos + q[:, NOPE_W + ROPE_W:] * sin
    kv = _dot(_rms(c_kv, gkv_ref[...]).astype(BF16), wukv_ref[...])
    k_pe_r = (k_pe * cos[:, :A_QK_ROPE] + k_pe_sw * sin[:, :A_QK_ROPE]).astype(BF16)
    zeros = jnp.zeros((tm, QK_PAD - A_QK_NOPE - A_QK_ROPE), BF16)
    ones_col = (lax.broadcasted_iota(jnp.int32, (tm, V_PAD - A_V_DIM), 1) == 0).astype(BF16)
    for hd in range(A_HEADS):
        n0 = hd * A_QK_NOPE
        r0 = hd * A_QK_ROPE
        q_ref[0, hd, :, :A_QK_NOPE] = q[:, n0:n0 + A_QK_NOPE].astype(BF16)
        q_ref[0, hd, :, A_QK_NOPE:A_QK_NOPE + A_QK_ROPE] = q_pe[:, r0:r0 + A_QK_ROPE].astype(BF16)
        q_ref[0, hd, :, A_QK_NOPE + A_QK_ROPE:] = zeros
        k_ref[0, hd, :, :A_QK_NOPE] = kv[:, n0:n0 + A_QK_NOPE].astype(BF16)
        k_ref[0, hd, :, A_QK_NOPE:A_QK_NOPE + A_QK_ROPE] = k_pe_r
        k_ref[0, hd, :, A_QK_NOPE + A_QK_ROPE:] = zeros
        v_ref[0, hd, :, :A_V_DIM] = kv[:, NOPE_W + n0:NOPE_W + n0 + A_V_DIM].astype(BF16)
        v_ref[0, hd, :, A_V_DIM:] = ones_col


def _even_in(x, sh, sc, g, win, gq, wuq, gkv, wukv, cos, sin, bdc, bds):
    b, l, d = x.shape
    tm = ROW_BLOCK
    row = lambda bi, i: (bi, i, 0)
    per_b = lambda bi, i: (bi, 0, 0)
    heads = lambda bi, i: (bi, 0, i, 0)
    return pl.pallas_call(
        _even_in_kernel,
        grid=(b, l // tm),
        in_specs=[
            pl.BlockSpec((1, tm, d), row),
            pl.BlockSpec((1, 1, d), per_b),
            pl.BlockSpec((1, 1, d), per_b),
            _resident(g.shape), _resident(win.shape), _resident(gq.shape), _resident(wuq.shape),
            _resident(gkv.shape), _resident(wukv.shape),
            pl.BlockSpec((tm, ROPE_W), lambda bi, i: (i, 0)),
            pl.BlockSpec((tm, ROPE_W), lambda bi, i: (i, 0)),
            _resident(bdc.shape), _resident(bds.shape),
        ],
        out_specs=[
            pl.BlockSpec((1, A_HEADS, tm, QK_PAD), heads),
            pl.BlockSpec((1, A_HEADS, tm, QK_PAD), heads),
            pl.BlockSpec((1, A_HEADS, tm, V_PAD), heads),
            pl.BlockSpec((1, tm, B_WIDTH), row),
            pl.BlockSpec((1, tm, B_WIDTH), row),
        ],
        out_shape=[
            jax.ShapeDtypeStruct((b, A_HEADS, l, QK_PAD), BF16),
            jax.ShapeDtypeStruct((b, A_HEADS, l, QK_PAD), BF16),
            jax.ShapeDtypeStruct((b, A_HEADS, l, V_PAD), BF16),
            jax.ShapeDtypeStruct((b, l, B_WIDTH), BF16),
            jax.ShapeDtypeStruct((b, l, B_WIDTH), BF16),
        ],
        compiler_params=_cparams(("parallel", "parallel"), 56),
        name="even_in",
    )(x, sh, sc, g, win, gq, wuq, gkv, wukv, cos, sin, bdc, bds)


ATTN_Q_BLOCK = 512


def _attn_kernel(q_ref, kc_ref, ko_ref, vc_ref, vo_ref, o_ref):
    kc = kc_ref[0, 0]
    ko = ko_ref[0, 0]
    vc = vc_ref[0, 0]
    vo = vo_ref[0, 0]
    tq = ATTN_Q_BLOCK
    n_chunks = q_ref.shape[2] // tq

    def scores(i):
        q = q_ref[0, 0, i * tq:(i + 1) * tq, :]
        return _dot_nt(q, kc), _dot_nt(q, ko)

    nxt = scores(0)
    for i in range(n_chunks):
        rows = slice(i * tq, (i + 1) * tq)
        s_c, s_o = nxt
        if i + 1 < n_chunks:
            nxt = scores(i + 1)
        m = jnp.maximum(jnp.max(s_c, axis=-1, keepdims=True), jnp.max(s_o, axis=-1, keepdims=True))
        p_c = jnp.exp2(s_c - m).astype(BF16)
        p_o = jnp.exp2(s_o - m).astype(BF16)
        o = _dot(p_c, vc) + _dot(p_o, vo)
        o_ref[0, rows, :] = (o[:, :A_V_DIM] / o[:, A_V_DIM:A_V_DIM + 1]).astype(BF16)


def _attention(q, kc, ko, vc, vo):
    b, h, s, _ = q.shape
    lc = kc.shape[2]
    bh = lambda bi, hi: (bi, hi, 0, 0)
    return pl.pallas_call(
        _attn_kernel,
        grid=(b, h),
        in_specs=[
            pl.BlockSpec((1, 1, s, QK_PAD), bh),
            pl.BlockSpec((1, 1, lc, QK_PAD), bh),
            pl.BlockSpec((1, 1, s, QK_PAD), bh),
            pl.BlockSpec((1, 1, lc, V_PAD), bh),
            pl.BlockSpec((1, 1, s, V_PAD), bh),
        ],
        out_specs=pl.BlockSpec((1, s, A_V_DIM), lambda bi, hi: (bi, 0, hi)),
        out_shape=jax.ShapeDtypeStruct((b, s, h * A_V_DIM), BF16),
        compiler_params=_cparams(("parallel", "parallel"), 48),
        name="attention",
    )(q, kc, ko, vc, vo)


def _fourier_kernel(ac_ref, as_ref, uc_ref, us_ref, o_ref):
    o_ref[0] = (_dot(ac_ref[...], uc_ref[0]) + _dot(as_ref[...], us_ref[0])).astype(BF16)


def _fourier(a_c, a_s, uc, us):
    b, s, w = uc.shape
    tm = 512
    return pl.pallas_call(
        _fourier_kernel,
        grid=(b, s // tm),
        in_specs=[
            pl.BlockSpec((tm, s), lambda bi, i: (i, 0)),
            pl.BlockSpec((tm, s), lambda bi, i: (i, 0)),
            pl.BlockSpec((1, s, w), lambda bi, i: (bi, 0, 0)),
            pl.BlockSpec((1, s, w), lambda bi, i: (bi, 0, 0)),
        ],
        out_specs=pl.BlockSpec((1, tm, w), lambda bi, i: (bi, i, 0)),
        out_shape=jax.ShapeDtypeStruct((b, s, w), BF16),
        compiler_params=_cparams(("parallel", "parallel"), 32),
        name="fourier",
    )(a_c, a_s, uc, us)


def _mix_epilogue(mix, r0, x_ref, gt_ref, sh_ref, sc_ref, gpost_ref, gffn_ref, wrh_ref, wrl_ref,
                  xo_ref, hx_ref, lg_ref):
    m = mix.shape[0]
    rows = slice(r0, r0 + m)
    xn = x_ref[0, rows, :] + gt_ref[0] * _rms(mix, gpost_ref[...])
    xo_ref[0, rows, :] = xn
    t = _rms(xn, gffn_ref[...]) * (1.0 + sc_ref[0]) + sh_ref[0]
    t_hi, t_lo = _split_bf16(t)
    wrh = wrh_ref[...]
    lg = _dot(t_hi, wrh) + _dot(t_lo, wrh) + _dot(t_hi, wrl_ref[...])
    lg_ref[:, rows] = lg.T[:N_EXPERTS, :]
    _store_token_tiles(hx_ref, t_hi.astype(F32), r0)


def _pipelined_sub_blocks(mix, tm, epilogue_refs):
    starts = list(range(0, tm, OUT_SUB_BLOCK))
    nxt = mix(starts[0])
    for j, r0 in enumerate(starts):
        cur = nxt
        if j + 1 < len(starts):
            nxt = mix(starts[j + 1])
        _mix_epilogue(cur, r0, *epilogue_refs)


def _even_out_kernel(oa_ref, ob_ref, w_ref, *rest):
    def mix(r0):
        rows = slice(r0, r0 + OUT_SUB_BLOCK)
        return _dot(oa_ref[0, rows, :], w_ref[:A_WIDTH, :]) + _dot(ob_ref[0, rows, :], w_ref[A_WIDTH:, :])

    _pipelined_sub_blocks(mix, oa_ref.shape[1], rest)


def _odd_out_kernel(oc_ref, s_ref, sp_ref, sn_ref, gb_ref, cw_ref, w_ref, *rest):
    i = pl.program_id(1)
    tm = s_ref.shape[1]
    s = s_ref[0].astype(F32)
    prev_row = jnp.where(i > 0, sp_ref[0, 15:16, :].astype(F32), 0.0)
    next_row = jnp.where(i < pl.num_programs(1) - 1, sn_ref[0, 0:1, :].astype(F32), 0.0)
    row = lax.broadcasted_iota(jnp.int32, (tm, 1), 0)
    s_dn = jnp.where(row == 0, prev_row, pltpu.roll(s, 1, axis=0))
    s_up = jnp.where(row == tm - 1, next_row, pltpu.roll(s, tm - 1, axis=0))
    cw = cw_ref[...]
    conv = cw[0:1, :] * s_dn + cw[1:2, :] * s + cw[2:3, :] * s_up
    od = (gb_ref[0].astype(F32) * conv).astype(BF16)
    def mix(r0):
        rows = slice(r0, r0 + OUT_SUB_BLOCK)
        return _dot(oc_ref[0, rows, :], w_ref[:C_WIDTH, :]) + _dot(od[rows, :], w_ref[C_WIDTH:, :])

    _pipelined_sub_blocks(mix, tm, rest)


def _epilogue_specs(b, s, d, tm):
    per_b = lambda bi, i: (bi, 0, 0)
    row = lambda bi, i: (bi, i, 0)
    nblk = s // tm
    in_specs = [
        pl.BlockSpec((1, tm, d), row),
        pl.BlockSpec((1, 1, d), per_b),
        pl.BlockSpec((1, 1, d), per_b),
        pl.BlockSpec((1, 1, d), per_b),
        _resident((1, d)), _resident((1, d)),
        _resident((d, LANES)), _resident((d, LANES)),
    ]
    out_specs = [
        pl.BlockSpec((1, tm, d), row),
        pl.BlockSpec((tm * TILE_ROWS, LANES), lambda bi, i: (bi * nblk + i, 0)),
        pl.BlockSpec((N_EXPERTS, tm), lambda bi, i: (0, bi * nblk + i)),
    ]
    out_shape = [
        jax.ShapeDtypeStruct((b, s, d), F32),
        jax.ShapeDtypeStruct((b * s * TILE_ROWS, LANES), F32),
        jax.ShapeDtypeStruct((N_EXPERTS, b * s), F32),
    ]
    return in_specs, out_specs, out_shape


def _even_out(oa, ob, w_out, x, gt, sh, sc, gpost, gffn, wrh, wrl):
    b, s, d = x.shape
    tm = OUT_BLOCK
    row = lambda bi, i: (bi, i, 0)
    ep_in, out_specs, out_shape = _epilogue_specs(b, s, d, tm)
    return pl.pallas_call(
        _even_out_kernel,
        grid=(b, s // tm),
        in_specs=[pl.BlockSpec((1, tm, A_WIDTH), row), pl.BlockSpec((1, tm, B_WIDTH), row),
                  _resident(w_out.shape)] + ep_in,
        out_specs=out_specs, out_shape=out_shape,
        compiler_params=_cparams(("parallel", "parallel"), 56),
        name="even_out",
    )(oa, ob, w_out, x, gt, sh, sc, gpost, gffn, wrh, wrl)


def _odd_out(oc, s_, gb, conv_w, w_out, x, gt, sh, sc, gpost, gffn, wrh, wrl):
    b, s, d = x.shape
    tm = OUT_BLOCK
    halo = 16
    nh = s // halo
    row = lambda bi, i: (bi, i, 0)
    ep_in, out_specs, out_shape = _epilogue_specs(b, s, d, tm)
    return pl.pallas_call(
        _odd_out_kernel,
        grid=(b, s // tm),
        in_specs=[
            pl.BlockSpec((1, tm, C_WIDTH), row),
            pl.BlockSpec((1, tm, D_WIDTH), row),
            pl.BlockSpec((1, halo, D_WIDTH), lambda bi, i: (bi, jnp.maximum(i * (tm // halo) - 1, 0), 0)),
            pl.BlockSpec((1, halo, D_WIDTH), lambda bi, i: (bi, jnp.minimum((i + 1) * (tm // halo), nh - 1), 0)),
            pl.BlockSpec((1, tm, D_WIDTH), row),
            _resident(conv_w.shape),
            _resident(w_out.shape),
        ] + ep_in,
        out_specs=out_specs, out_shape=out_shape,
        compiler_params=_cparams(("parallel", "parallel"), 56),
        name="odd_out",
    )(oc, s_, s_, s_, gb, conv_w, w_out, x, gt, sh, sc, gpost, gffn, wrh, wrl)


def _odd_in_kernel(x_ref, sh_ref, sc_ref, g_ref, win_ref, gv_ref, ws_ref, bs_ref, oc_ref, s_ref, gb_ref):
    tm = x_ref.shape[1]
    hb = (_rms(x_ref[0], g_ref[...]) * (1.0 + sc_ref[0]) + sh_ref[0]).astype(BF16)
    o = 2 * C_WIDTH
    u = _dot(hb, win_ref[:, :C_WIDTH])
    v = _dot(hb, win_ref[:, C_WIDTH:o])
    hd_ = _dot(hb, win_ref[:, o:o + D_WIDTH])
    gb_ref[0] = _dot(hb, win_ref[:, o + D_WIDTH:o + 2 * D_WIDTH]).astype(BF16)
    gc = _dot(hb, win_ref[:, o + 2 * D_WIDTH:])
    s_ref[0] = (gc * hd_).astype(BF16)
    u = jax.nn.gelu(u)
    v = jax.nn.gelu(v)
    gv = gv_ref[...]
    bs = bs_ref[...]
    for hd in range(C_HEADS):
        c0 = hd * C_HEAD_DIM
        vh = _rms(v[:, c0:c0 + C_HEAD_DIM], gv[:, c0:c0 + C_HEAD_DIM]).astype(BF16)
        w = ws_ref[hd]
        bias = bs[:, hd:hd + 1]
        for n in range(tm // CHUNK):
            p0 = n * CHUNK
            sv = _dot(w, vh[p0:p0 + CHUNK, :]) + bias
            oc_ref[0, p0:p0 + CHUNK, c0:c0 + C_HEAD_DIM] = (u[p0:p0 + CHUNK, c0:c0 + C_HEAD_DIM] * sv).astype(BF16)


def _odd_in(x, sh, sc, g, win, gv, ws, bs_t):
    b, s, d = x.shape
    tm = ROW_BLOCK
    row = lambda bi, i: (bi, i, 0)
    per_b = lambda bi, i: (bi, 0, 0)
    out = jax.ShapeDtypeStruct((b, s, C_WIDTH), BF16)
    return pl.pallas_call(
        _odd_in_kernel,
        grid=(b, s // tm),
        in_specs=[
            pl.BlockSpec((1, tm, d), row),
            pl.BlockSpec((1, 1, d), per_b),
            pl.BlockSpec((1, 1, d), per_b),
            _resident(g.shape), _resident(win.shape), _resident(gv.shape), _resident(ws.shape),
            _resident(bs_t.shape),
        ],
        out_specs=[pl.BlockSpec((1, tm, C_WIDTH), row)] * 3,
        out_shape=[out, out, out],
        compiler_params=_cparams(("parallel", "parallel"), 56),
        name="odd_in",
    )(x, sh, sc, g, win, gv, ws, bs_t)


def _first_index_of_max(vals, idx, sentinel):
    m = jnp.max(vals, axis=0, keepdims=True)
    first = jnp.min(jnp.where(vals == m, idx, sentinel), axis=0, keepdims=True)
    return m, first


def _route_kernel(lg_ref, b_ref, e_ref, w_ref, c_ref):
    tn = lg_ref.shape[1]
    scores = jax.nn.sigmoid(lg_ref[...])
    sel = scores + b_ref[...]
    row8 = lax.broadcasted_iota(jnp.int32, (8, tn), 0).astype(F32)
    neg = -jnp.inf
    gs = jnp.zeros((N_GROUPS, tn), F32)
    for g in range(N_GROUPS):
        blk = sel[g * GROUP_SIZE:(g + 1) * GROUP_SIZE, :]
        m1, i1 = _first_index_of_max(blk, row8, float(GROUP_SIZE))
        m2 = jnp.max(jnp.where(row8 == i1, neg, blk), axis=0, keepdims=True)
        gs = jnp.where(row8 == float(g), m1 + m2, gs)
    keep = jnp.zeros((N_GROUPS, tn), F32)
    cur = gs
    for _ in range(TOPK_GROUPS):
        _, first = _first_index_of_max(cur, row8, float(N_GROUPS))
        pick = row8 == first
        keep = jnp.where(pick, 1.0, keep)
        cur = jnp.where(pick, neg, cur)
    masked = jnp.concatenate(
        [jnp.where(jnp.max(jnp.where(row8 == float(g), keep, 0.0), axis=0, keepdims=True) > 0.5,
                   sel[g * GROUP_SIZE:(g + 1) * GROUP_SIZE, :], neg) for g in range(N_GROUPS)],
        axis=0)
    eidx = lax.broadcasted_iota(jnp.int32, (N_EXPERTS, tn), 0).astype(F32)
    e_out = jnp.zeros((8, tn), F32)
    w_out = jnp.zeros((8, tn), F32)
    total = jnp.zeros((1, tn), F32)
    picked = jnp.zeros((N_EXPERTS, tn), F32)
    for k in range(TOP_K):
        _, first = _first_index_of_max(masked, eidx, float(N_EXPERTS))
        pick = eidx == first
        wk = jnp.sum(jnp.where(pick, scores, 0.0), axis=0, keepdims=True)
        masked = jnp.where(pick, neg, masked)
        picked = jnp.where(pick, 1.0, picked)
        e_out = jnp.where(row8 == float(k), first, e_out)
        w_out = jnp.where(row8 == float(k), wk, w_out)
        total = total + wk
    e_ref[...] = e_out.astype(jnp.int32)
    w_ref[...] = w_out / total * ROUTED_SCALE
    c_ref[...] = jnp.broadcast_to(jnp.sum(picked, axis=1, keepdims=True), c_ref.shape).astype(jnp.int32)


def _route(logits_t, bias):
    e, t = logits_t.shape
    tn = ROUTE_BLOCK
    nblk = t // tn
    top_e, wts, cnt = pl.pallas_call(
        _route_kernel,
        grid=(nblk,),
        in_specs=[pl.BlockSpec((e, tn), lambda i: (0, i)), pl.BlockSpec((e, 1), lambda i: (0, 0))],
        out_specs=[pl.BlockSpec((8, tn), lambda i: (0, i))] * 2 + [pl.BlockSpec((e, LANES), lambda i: (0, i))],
        out_shape=[jax.ShapeDtypeStruct((8, t), jnp.int32), jax.ShapeDtypeStruct((8, t), F32),
                   jax.ShapeDtypeStruct((e, nblk * LANES), jnp.int32)],
        compiler_params=_cparams(("parallel",), 32),
        name="route",
    )(logits_t, bias.reshape(e, 1))
    counts = jnp.sum(cnt.reshape(e, nblk, LANES)[:, :, 0], axis=1)
    return top_e, wts, counts


def _dispatch_tables(top_e, counts, n_tok):
    bm = EXPERT_BLOCK
    n_asg = n_tok * TOP_K
    n_blk = n_asg // bm
    n_steps = n_blk + N_EXPERTS
    i32 = jnp.int32
    flat_e = top_e[:TOP_K].T.reshape(n_asg)
    order = jnp.argsort(flat_e, stable=True).astype(i32)
    tok = order // TOP_K
    k = order - tok * TOP_K
    src = jnp.concatenate([tok * TILE_ROWS, jnp.zeros((bm,), i32)]).reshape(n_blk + 1, 1, bm)
    spare = (n_asg + jnp.arange(bm, dtype=i32)) * TILE_ROWS
    dst = jnp.concatenate([spare, (k * n_tok + tok) * TILE_ROWS]).reshape(n_blk + 1, 1, bm)

    experts = jnp.arange(N_EXPERTS, dtype=i32)
    ends = jnp.cumsum(counts).astype(i32)
    starts = ends - counts
    first_blk = starts // bm
    n_e_blk = jnp.where(counts > 0, (ends - 1) // bm - first_blk + 1, 0)
    item_end = jnp.cumsum(n_e_blk).astype(i32)
    item_start = item_end - n_e_blk
    n_items = item_end[-1]
    e_last = jnp.max(jnp.where(counts > 0, experts, 0))
    w = jnp.arange(n_steps, dtype=i32)
    active = w < n_items
    e_w = jnp.where(active, jnp.sum((item_end[None, :] <= w[:, None]).astype(i32), axis=1), e_last)
    onehot = e_w[:, None] == experts[None, :]
    of_item = lambda per_expert: jnp.sum(jnp.where(onehot, per_expert[None, :], 0), axis=1)
    blk_w = jnp.where(active, of_item(first_blk) + w - of_item(item_start), n_blk)
    lo_w = jnp.clip(of_item(starts) - blk_w * bm, 0, bm)
    hi_w = jnp.clip(of_item(ends) - blk_w * bm, 0, bm)
    neg1 = jnp.full((1,), -1, i32)
    first_of_blk = (blk_w != jnp.concatenate([neg1, blk_w[:-1]])).astype(i32)
    first_of_e = ((e_w != jnp.concatenate([neg1, e_w[:-1]])) & active).astype(i32)
    cand = jnp.where(counts > 0, experts, N_EXPERTS)
    suffix_min = lax.cummin(cand[::-1])[::-1]
    nxt = jnp.concatenate([suffix_min[1:], jnp.full((1,), N_EXPERTS, i32)])
    nxt = jnp.where(nxt >= N_EXPERTS, -1, nxt)
    return src, dst, (blk_w, e_w, lo_w, hi_w, first_of_blk, first_of_e, of_item(nxt), n_items.reshape(1))


FF_CHUNK = 256
DOWN_CHUNK = 512


CHUNKS_PER_HALF = HALF // DOWN_CHUNK
TILES_PER_CHUNK = DOWN_CHUNK // LANES


def _swiglu_tiles(x_ref, m, wg_ref, wu_ref, wd_ref, between=lambda: None, on_pair=None):
    half_tiles = TILE_ROWS // 2
    load = lambda s: x_ref[pl.ds(s, m, stride=TILE_ROWS), :].astype(BF16)
    x_lo = jnp.concatenate([load(s) for s in range(half_tiles)], axis=1)
    first = _dot(x_lo, wg_ref[:HALF, :FF_CHUNK])
    x_hi = jnp.concatenate([load(s) for s in range(half_tiles, TILE_ROWS)], axis=1)
    hs = []
    for c in range(FF_PAD // FF_CHUNK):
        cols = slice(c * FF_CHUNK, (c + 1) * FF_CHUNK)
        g_lo = first if c == 0 else _dot(x_lo, wg_ref[:HALF, cols])
        g = g_lo + _dot(x_hi, wg_ref[HALF:, cols])
        between()
        u = _dot(x_lo, wu_ref[:HALF, cols]) + _dot(x_hi, wu_ref[HALF:, cols])
        between()
        hs.append((_silu(g) * u).astype(BF16))
    h = jnp.concatenate(hs, axis=1)
    down = lambda n: _dot(h, wd_ref[:, n * DOWN_CHUNK:(n + 1) * DOWN_CHUNK])
    ys = [None] * (2 * CHUNKS_PER_HALF)
    done = None
    for j in range(CHUNKS_PER_HALF):
        ys[j] = down(j)
        between()
        ys[CHUNKS_PER_HALF + j] = down(CHUNKS_PER_HALF + j)
        between()
        if on_pair is not None and done is not None:
            on_pair(done, ys[done], ys[CHUNKS_PER_HALF + done])
        done = j
    if on_pair is not None:
        on_pair(done, ys[done], ys[CHUNKS_PER_HALF + done])
    return ys


def _packed_pair(j, lo_chunk, hi_chunk):
    out = []
    for half, chunk in enumerate((lo_chunk, hi_chunk)):
        for t in range(TILES_PER_CHUNK):
            cols = slice(t * LANES, (t + 1) * LANES)
            out.append((half * (TILE_ROWS // 2) + j * TILES_PER_CHUNK + t, chunk[:, cols]))
    return out


N_DMA_SLOTS = (FF_PAD // FF_CHUNK) * 2 + D_MODEL // DOWN_CHUNK
GATHER_DMA_PRIORITY = 0
OTHER_DMA_PRIORITY = 1


def _expert_kernel(blk_ref, exp_ref, lo_ref, hi_ref, fb_ref, fe_ref, nxt_ref, ni_ref,
                   src0_ref, src1_ref, dst_ref, hx_hbm, wg_hbm, wu_hbm, wd_hbm,
                   ys_hbm,
                   xbuf0, xbuf1, ybuf0, ybuf1, sg, su, sd, wg, wu, wd, gsem, ssem, wsem, *, layer, n_blk):
    bm = EXPERT_BLOCK
    w = pl.program_id(0)
    n_items = ni_ref[0]
    blk = blk_ref[w]
    expert = exp_ref[w]
    active = w < n_items
    xbufs = (xbuf0, xbuf1)
    ybufs = (ybuf0, ybuf1)
    tile = lambda r: pl.ds(r * TILE_ROWS, TILE_ROWS)

    def weight_copies(e):
        return (pltpu.make_async_copy(wg_hbm.at[layer, e], sg.at[pl.ds(0, EXPERT_FF), :], wsem.at[0]),
                pltpu.make_async_copy(wu_hbm.at[layer, e], su.at[pl.ds(0, EXPERT_FF), :], wsem.at[1]),
                pltpu.make_async_copy(wd_hbm.at[layer, e], sd, wsem.at[2]))

    def gather_starts(src_ref, p):
        def one(r):
            row = pl.multiple_of(src_ref[0, 0, r], TILE_ROWS)
            pltpu.make_async_copy(hx_hbm.at[pl.ds(row, TILE_ROWS), :], xbufs[p].at[tile(r), :],
                                  gsem.at[p]).start(priority=GATHER_DMA_PRIORITY)
        return [functools.partial(one, r) for r in range(bm)]

    def scatter_starts(p):
        def one(r):
            row = pl.multiple_of(dst_ref[0, 0, r], TILE_ROWS)
            pltpu.make_async_copy(ybufs[p].at[tile(r), :], ys_hbm.at[pl.ds(row, TILE_ROWS), :],
                                  ssem.at[0]).start(priority=OTHER_DMA_PRIORITY)
        return [functools.partial(one, r) for r in range(bm)]

    def wait_gather(p):
        pltpu.make_async_copy(hx_hbm.at[pl.ds(0, bm * TILE_ROWS), :], xbufs[p], gsem.at[p]).wait()

    def wait_scatter(p):
        pltpu.make_async_copy(ybufs[p], ys_hbm.at[pl.ds(0, bm * TILE_ROWS), :], ssem.at[0]).wait()

    @pl.when(w == 0)
    def _():
        sg[EXPERT_FF:, :] = jnp.zeros((FF_PAD - EXPERT_FF, D_MODEL), F32)
        su[EXPERT_FF:, :] = jnp.zeros((FF_PAD - EXPERT_FF, D_MODEL), F32)
        wd[EXPERT_FF:, :] = jnp.zeros((FF_PAD - EXPERT_FF, D_MODEL), BF16)
        ybuf1[...] = jnp.zeros(ybuf1.shape, F32)
        for c in weight_copies(expert):
            c.start(priority=OTHER_DMA_PRIORITY)
        for start in gather_starts(src0_ref, 0):
            start()

    @pl.when(jnp.logical_and(active, fe_ref[w] == 1))
    def _():
        for c in weight_copies(expert):
            c.wait()
        for c in range(FF_PAD // LANES):
            cols = slice(c * LANES, (c + 1) * LANES)
            wg[:, cols] = sg[cols, :].T.astype(BF16)
            wu[:, cols] = su[cols, :].T.astype(BF16)
        rows_d = 64

        def cast_down(i, carry):
            r = pl.ds(pl.multiple_of(i * rows_d, rows_d), rows_d)
            wd[r, :] = sd[r, :].astype(BF16)
            return carry

        lax.fori_loop(0, EXPERT_FF // rows_d, cast_down, 0)

        @pl.when(nxt_ref[w] >= 0)
        def _():
            for c in weight_copies(nxt_ref[w]):
                c.start(priority=OTHER_DMA_PRIORITY)

    def first_item_of_block(p):
        @pl.when(blk > 0)
        def _():
            wait_scatter(p)
        wait_gather(p)
        pending = gather_starts(src1_ref, 1 - p) + scatter_starts(1 - p)
        per_slot = -(-len(pending) // N_DMA_SLOTS)

        def between():
            for start in pending[:per_slot]:
                start()
            del pending[:per_slot]

        def store_pair(j, lo_chunk, hi_chunk):
            for s, chunk in _packed_pair(j, lo_chunk, hi_chunk):
                ybufs[p][pl.ds(s, bm, stride=TILE_ROWS), :] = chunk

        _swiglu_tiles(xbufs[p], bm, wg, wu, wd, between, store_pair)
        assert not pending

    def later_item_of_block(p):
        row = lax.broadcasted_iota(jnp.int32, (bm, 1), 0)
        mine = jnp.logical_and(row >= lo_ref[w], row < hi_ref[w])

        def merge_pair(j, lo_chunk, hi_chunk):
            for s, chunk in _packed_pair(j, lo_chunk, hi_chunk):
                idx = pl.ds(s, bm, stride=TILE_ROWS)
                ybufs[p][idx, :] = jnp.where(mine, chunk, ybufs[p][idx, :])

        _swiglu_tiles(xbufs[p], bm, wg, wu, wd, on_pair=merge_pair)

    parity = lax.rem(blk, 2)
    first = fb_ref[w] == 1
    for p in range(2):
        pl.when(jnp.logical_and(active, jnp.logical_and(first, parity == p)))(
            functools.partial(first_item_of_block, p))
        pl.when(jnp.logical_and(active, jnp.logical_and(jnp.logical_not(first), parity == p)))(
            functools.partial(later_item_of_block, p))

    @pl.when(w == n_items)
    def _():
        last = (n_blk - 1) % 2
        wait_scatter(1 - last)
        wait_gather(1 - last)
        for start in scatter_starts(last):
            start()
        wait_scatter(last)


def _experts(hx, src, dst, items, w_gate, w_up, w_down, layer):
    bm = EXPERT_BLOCK
    n_blk = src.shape[0] - 1
    n_steps = items[0].shape[0]
    n_out_rows = (n_blk + 1) * bm * TILE_ROWS
    smem = functools.partial(pl.BlockSpec, (1, 1, bm), memory_space=pltpu.SMEM)
    grid_spec = pltpu.PrefetchScalarGridSpec(
        num_scalar_prefetch=len(items),
        grid=(n_steps,),
        in_specs=[
            smem(index_map=lambda w, *_: (0, 0, 0)),
            smem(index_map=lambda w, blk, *_: (jnp.minimum(blk[w] + 1, n_blk), 0, 0)),
            smem(index_map=lambda w, blk, *_: (blk[w], 0, 0)),
            pl.BlockSpec(memory_space=pl.ANY),
            pl.BlockSpec(memory_space=pl.ANY),
            pl.BlockSpec(memory_space=pl.ANY),
            pl.BlockSpec(memory_space=pl.ANY),
        ],
        out_specs=pl.BlockSpec(memory_space=pl.ANY),
        scratch_shapes=[
            pltpu.VMEM((bm * TILE_ROWS, LANES), F32), pltpu.VMEM((bm * TILE_ROWS, LANES), F32),
            pltpu.VMEM((bm * TILE_ROWS, LANES), F32), pltpu.VMEM((bm * TILE_ROWS, LANES), F32),
            pltpu.VMEM((FF_PAD, D_MODEL), F32), pltpu.VMEM((FF_PAD, D_MODEL), F32),
            pltpu.VMEM((EXPERT_FF, D_MODEL), F32),
            pltpu.VMEM((D_MODEL, FF_PAD), BF16), pltpu.VMEM((D_MODEL, FF_PAD), BF16),
            pltpu.VMEM((FF_PAD, D_MODEL), BF16),
            pltpu.SemaphoreType.DMA((2,)),
            pltpu.SemaphoreType.DMA((1,)),
            pltpu.SemaphoreType.DMA((3,)),
        ],
    )
    return pl.pallas_call(
        functools.partial(_expert_kernel, layer=layer, n_blk=n_blk),
        grid_spec=grid_spec,
        out_shape=jax.ShapeDtypeStruct((n_out_rows, LANES), F32),
        compiler_params=_cparams(("arbitrary",), 48),
        name="experts",
    )(*items, src, src, dst, hx, w_gate, w_up, w_down)


def _combine_kernel(hx_ref, y0, y1, y2, y3, y4, y5, wt_ref, wg_ref, wu_ref, wd_ref, x_ref, gt_ref, g_ref, o_ref,
                    acc_ref):
    tm = x_ref.shape[1]
    y_refs = (y0, y1, y2, y3, y4, y5)
    wt = wt_ref[...]
    wk = [jnp.broadcast_to(wt[:, k:k + 1], (tm, LANES)) for k in range(TOP_K)]
    todo = list(range(TILE_ROWS))

    def combine_one_chunk():
        if not todo:
            return
        s = todo.pop(0)
        idx = pl.ds(s, tm, stride=TILE_ROWS)
        acc = None
        for k, y_ref in enumerate(y_refs):
            term = wk[k] * y_ref[idx, :]
            acc = term if acc is None else acc + term
        acc_ref[:, s * LANES:(s + 1) * LANES] = acc

    shared = jnp.concatenate(_swiglu_tiles(hx_ref, tm, wg_ref, wu_ref, wd_ref, combine_one_chunk), axis=1)
    while todo:
        combine_one_chunk()
    ff = acc_ref[...] + shared
    o_ref[0] = x_ref[0] + gt_ref[0] * _rms(ff, g_ref[...])


def _combine(hx, ys, wts_t, wg, wu, wd, x, gt, g):
    b, s, d = x.shape
    tm = ROW_BLOCK
    nblk = s // tm
    n_tok = b * s
    row = lambda bi, i: (bi, i, 0)
    tiles = (tm * TILE_ROWS, LANES)
    y_specs = [pl.BlockSpec(tiles, functools.partial(lambda bi, i, k: (k * (n_tok // tm) + bi * nblk + i, 0), k=k))
               for k in range(TOP_K)]
    return pl.pallas_call(
        _combine_kernel,
        grid=(b, nblk),
        in_specs=[pl.BlockSpec(tiles, lambda bi, i: (bi * nblk + i, 0))] + y_specs + [
            pl.BlockSpec((tm, 8), lambda bi, i: (bi * nblk + i, 0)),
            _resident(wg.shape), _resident(wu.shape), _resident(wd.shape),
            pl.BlockSpec((1, tm, d), row),
            pl.BlockSpec((1, 1, d), lambda bi, i: (bi, 0, 0)),
            _resident(g.shape),
        ],
        out_specs=pl.BlockSpec((1, tm, d), row),
        out_shape=jax.ShapeDtypeStruct((b, s, d), F32),
        scratch_shapes=[pltpu.VMEM((tm, d), F32)],
        compiler_params=_cparams(("parallel", "parallel"), 60),
        name="combine",
    )(hx, ys, ys, ys, ys, ys, ys, wts_t, wg, wu, wd, x, gt, g)


def _rope_tables(seq):
    t = jnp.arange(seq, dtype=jnp.int32)
    row = (t // GRID_W).astype(F32)
    col = (t % GRID_W).astype(F32)
    per_axis = A_QK_ROPE // 4
    inv = ROPE_BASE ** (-jnp.arange(per_axis, dtype=F32) / per_axis)
    ang = jnp.concatenate([row[:, None] * inv, col[:, None] * inv], axis=-1)
    cos = jnp.cos(ang)
    sin = jnp.sin(ang)
    cos64 = jnp.concatenate([cos, cos], axis=-1)
    sin64 = jnp.concatenate([-sin, sin], axis=-1)
    return jnp.tile(cos64, (1, A_HEADS)), jnp.tile(sin64, (1, A_HEADS))


def _dft_tables(n):
    k = jnp.arange(n, dtype=jnp.int32)
    ang = ((k[:, None] * k[None, :]) % n).astype(F32) * (2.0 * jnp.pi / n)
    scale = n ** -0.5
    return jnp.cos(ang) * scale, -jnp.sin(ang) * scale


def _swap_halves(w):
    half = w.shape[-1] // 2
    return jnp.concatenate([w[..., half:], w[..., :half]], axis=-1)


def _pad_ff(w, axis):
    pad = [(0, 0)] * w.ndim
    pad[axis] = (0, FF_PAD - EXPERT_FF)
    return jnp.pad(w.astype(BF16), pad)


def _router_parts(w_router):
    w = jnp.pad(w_router, ((0, 0), (0, LANES - N_EXPERTS)))
    hi = w.astype(BF16)
    lo = (w - hi.astype(F32)).astype(BF16)
    return hi, lo


def _moe(hx, logits_t, layer, b_router, w_gate, w_up, w_down, sg, su, sd, x, gt, g_post):
    b, s, d = x.shape
    n_tok = b * s
    top_e, wts, counts = _route(logits_t, b_router)
    src, dst, items = _dispatch_tables(top_e, counts, n_tok)
    ys = _experts(hx, src, dst, items, jnp.swapaxes(w_gate, 2, 3), jnp.swapaxes(w_up, 2, 3), w_down, layer)
    return _combine(hx, ys, wts.T, _pad_ff(sg, 1), _pad_ff(su, 1), _pad_ff(sd, 0), x, gt, g_post)


def kernel(x, c, ctx, c_ctx, mod_w, mod_b, norm_g, a_w_in, a_g_q, a_w_uq, a_g_kv, a_w_ukv, a_w_out, o_w_in, o_g_v, o_w_s, o_b_s, o_conv_w, o_w_out, moe_w_router, moe_b_router, moe_w_gate, moe_w_up, moe_w_down, sh_w_gate, sh_w_up, sh_w_down):
    b, s, d = x.shape
    lc = ctx.shape[1]

    cond = jnp.concatenate([c, c_ctx[None, :], jnp.zeros((COND_ROWS - b - 1, d), F32)], axis=0)
    mod = _adaln(cond, mod_w, mod_b)

    def mod_rows(layer, rows):
        m = mod[layer, rows]
        return [m[:, None, j * d:(j + 1) * d] for j in range(6)]

    def moe(layer, hx, lg, x, gt, g_post):
        return _moe(hx, lg, layer, moe_b_router[layer], moe_w_gate, moe_w_up, moe_w_down,
                    sh_w_gate[layer], sh_w_up[layer], sh_w_down[layer], x, gt, g_post)

    sh1, sc1, gt1, sh2, sc2, gt2 = mod_rows(0, slice(0, b))
    csh1, csc1 = [jnp.broadcast_to(m, (b, 1, d)) for m in mod_rows(0, slice(b, b + 1))[:2]]
    g = norm_g[0]
    w_in = a_w_in[0]
    o = A_Q_RANK + A_KV_RANK
    k_pe_w = w_in[:, o:o + A_QK_ROPE]
    win = jnp.concatenate([w_in[:, :o], w_in[:, o + A_QK_ROPE:], k_pe_w, _swap_halves(k_pe_w)], axis=1).astype(BF16)
    wq = a_w_uq[0].reshape(A_Q_RANK, A_HEADS, A_QK_NOPE + A_QK_ROPE)
    wq_pe = wq[:, :, A_QK_NOPE:]
    wuq = jnp.concatenate([wq[:, :, :A_QK_NOPE].reshape(A_Q_RANK, NOPE_W), wq_pe.reshape(A_Q_RANK, ROPE_W),
                           _swap_halves(wq_pe).reshape(A_Q_RANK, ROPE_W)], axis=1).astype(BF16)
    wkv = a_w_ukv[0].reshape(A_KV_RANK, A_HEADS, A_QK_NOPE + A_V_DIM)
    wukv = jnp.concatenate([wkv[:, :, :A_QK_NOPE].reshape(A_KV_RANK, NOPE_W),
                            wkv[:, :, A_QK_NOPE:].reshape(A_KV_RANK, A_WIDTH)], axis=1).astype(BF16)
    gq = a_g_q[0][None, :]
    gkv = a_g_kv[0][None, :]
    cos, sin = _rope_tables(s)
    cc, cs = _dft_tables(B_GROUP_DIM)
    eye = jnp.eye(B_GROUPS, dtype=F32)
    bdc = jnp.kron(eye, cc).astype(BF16)
    bds = jnp.kron(eye, -cs).astype(BF16)
    a_c, a_s = _dft_tables(s)
    a_c = a_c.astype(BF16)
    a_s = a_s.astype(BF16)

    ones = jnp.ones((lc, ROPE_W), F32)
    _, kc, vc, _, _ = _even_in(ctx, csh1, csc1, g[0][None, :], win, gq, wuq, gkv, wukv, ones, jnp.zeros_like(ones),
                               bdc, bds)
    q, ko, vo, uc, us = _even_in(x, sh1, sc1, g[0][None, :], win, gq, wuq, gkv, wukv, cos, sin, bdc, bds)
    o_a = _attention(q, kc, ko, vc, vo)
    o_b = _fourier(a_c, a_s, uc, us)
    wrh, wrl = _router_parts(moe_w_router[0])
    x, hx, lg = _even_out(o_a, o_b, a_w_out[0].astype(BF16), x, gt1, sh2, sc2, g[1][None, :], g[2][None, :], wrh, wrl)
    x = moe(0, hx, lg, x, gt2, g[3][None, :])

    sh1, sc1, gt1, sh2, sc2, gt2 = mod_rows(1, slice(0, b))
    g = norm_g[1]
    oc, s_, gb = _odd_in(x, sh1, sc1, g[0][None, :], o_w_in[0].astype(BF16), o_g_v[0][None, :],
                         o_w_s[0].astype(BF16), o_b_s[0].T)
    wrh, wrl = _router_parts(moe_w_router[1])
    x, hx, lg = _odd_out(oc, s_, gb, o_conv_w[0], o_w_out[0].astype(BF16), x, gt1, sh2, sc2, g[1][None, :],
                         g[2][None, :], wrh, wrl)
    x = moe(1, hx, lg, x, gt2, g[3][None, :])
    return x
```
